```python
import jax, jax.numpy as jnp
from jax import lax
import numpy as np

D_MODEL = 2048
BATCH = 4
SEQ = 4096
DEPTH = 1
DEC_BATCH = 8
DEC_SEQ = 64
PAST_LEN = 2048

CHUNK = 64
N_PAST_CHUNKS = 8
BAND_PAST = N_PAST_CHUNKS * CHUNK
BAND_LEN = BAND_PAST + CHUNK
BAND_HEADS = 8
BAND_HEAD_DIM = 128
BAND_WIDTH = BAND_HEADS * BAND_HEAD_DIM
REL_MAX = 256
REL_SIZE = (CHUNK - 1) + REL_MAX + 1
BAND_SCALE = BAND_HEAD_DIM ** -0.5
MLA_HEADS = 8
MLA_NOPE = 128
MLA_ROPE = 64
MLA_QK = MLA_NOPE + MLA_ROPE
MLA_V = 128
MLA_WIDTH = MLA_HEADS * MLA_V
MLA_KV_RANK = 512
MLA_SCALE = MLA_QK ** -0.5
ROPE_THETA = 10000.0
D_FF = 4 * D_MODEL
Q_BLOCK = 128
EPS = 1e-6
_IN_WIDTHS = (BAND_WIDTH, BAND_WIDTH, BAND_WIDTH, MLA_HEADS * MLA_QK, MLA_KV_RANK, MLA_ROPE, D_MODEL, D_MODEL)
D_IN_PROJ = sum(_IN_WIDTHS)
SPLIT_POINTS = tuple(sum(_IN_WIDTHS[:i + 1]) for i in range(len(_IN_WIDTHS) - 1))

kernel_name = 'streaming_band_mla_hybrid_step'


def rms_norm(x, g):
    xf = x.astype(jnp.float32)
    y = xf * lax.rsqrt(jnp.mean(xf * xf, axis=-1, keepdims=True) + EPS)
    return (y * g.astype(jnp.float32)).astype(x.dtype)


def rope(x, pos):
    half = x.shape[-1] // 2
    freqs = ROPE_THETA ** (-(jnp.arange(half, dtype=jnp.float32) / half))
    ang = pos.astype(jnp.float32)[:, None] * freqs[None, :]
    shape = (1, x.shape[1]) + (1,) * (x.ndim - 3) + (half,)
    cos = jnp.cos(ang).reshape(shape)
    sin = jnp.sin(ang).reshape(shape)
    xf = x.astype(jnp.float32)
    x1, x2 = xf[..., :half], xf[..., half:]
    return jnp.concatenate([x1 * cos - x2 * sin, x2 * cos + x1 * sin], axis=-1).astype(x.dtype)


def band_bias(n_q, n_past, n_k, table):
    dist = n_past + jnp.arange(n_q)[:, None] - jnp.arange(n_k)[None, :]
    idx = jnp.clip(dist, -(CHUNK - 1), REL_MAX) + (CHUNK - 1)
    return table[:, idx].astype(jnp.float32)


def mixer_projections(x, pos, norm_g, w_in, g_aq, g_ak, g_kv, g_kr, g_qn, g_qr):
    b, s = x.shape[0], x.shape[1]
    xn = rms_norm(x, norm_g)
    h = xn @ w_in
    aq, ak, av, bq, ckv, kr, ga, gb = jnp.split(h, SPLIT_POINTS, axis=-1)
    aq = rms_norm(aq.reshape(b, s, BAND_HEADS, BAND_HEAD_DIM), g_aq)
    ak = rms_norm(ak.reshape(b, s, BAND_HEADS, BAND_HEAD_DIM), g_ak)
    av = av.reshape(b, s, BAND_HEADS, BAND_HEAD_DIM)
    bq = bq.reshape(b, s, MLA_HEADS, MLA_QK)
    qn = rms_norm(bq[..., :MLA_NOPE], g_qn)
    qr = rope(rms_norm(bq[..., MLA_NOPE:], g_qr), pos)
    ckv = rms_norm(ckv, g_kv)
    kr = rope(rms_norm(kr, g_kr), pos)
    return aq, ak, av, qn, qr, ckv, kr, ga, gb


def band_attention_prompt(q, k, v, table):
    b, s = q.shape[0], q.shape[1]
    nc = s // CHUNK
    pad = jnp.zeros((b, BAND_PAST, BAND_HEADS, BAND_HEAD_DIM), k.dtype)
    kp = jnp.concatenate([pad, k], axis=1)
    vp = jnp.concatenate([pad, v], axis=1)
    valid = jnp.arange(BAND_PAST + s) >= BAND_PAST
    bias = band_bias(CHUNK, BAND_PAST, BAND_LEN, table)
    qc = q.reshape(b, nc, CHUNK, BAND_HEADS, BAND_HEAD_DIM).swapaxes(0, 1)

    def step(args):
        c, qb = args
        start = c * CHUNK
        kb = lax.dynamic_slice_in_dim(kp, start, BAND_LEN, axis=1)
        vb = lax.dynamic_slice_in_dim(vp, start, BAND_LEN, axis=1)
        mb = lax.dynamic_slice_in_dim(valid, start, BAND_LEN, axis=0)
        sc = jnp.einsum('bqhd,bkhd->bhqk', qb, kb).astype(jnp.float32) * BAND_SCALE + bias[None]
        sc = jnp.where(mb[None, None, None, :], sc, -jnp.inf)
        p = jax.nn.softmax(sc, axis=-1).astype(vb.dtype)
        return jnp.einsum('bhqk,bkhd->bqhd', p, vb)

    out = lax.map(step, (jnp.arange(nc), qc))
    return out.swapaxes(0, 1).reshape(b, s, BAND_WIDTH)


def band_attention_sample(q, k_new, v_new, k_cache, v_cache, table):
    b, t = q.shape[0], q.shape[1]
    n_past = k_cache.shape[1]
    kb = jnp.concatenate([k_cache.astype(k_new.dtype), k_new], axis=1)
    vb = jnp.concatenate([v_cache.astype(v_new.dtype), v_new], axis=1)
    bias = band_bias(t, n_past, n_past + t, table)
    sc = jnp.einsum('bqhd,bkhd->bhqk', q, kb).astype(jnp.float32) * BAND_SCALE + bias[None]
    p = jax.nn.softmax(sc, axis=-1).astype(vb.dtype)
    return jnp.einsum('bhqk,bkhd->bqhd', p, vb).reshape(b, t, BAND_WIDTH)


def mla_expand(ckv, w_kv_b, g_kn):
    b, l = ckv.shape[0], ckv.shape[1]
    kv = (ckv @ w_kv_b).reshape(b, l, MLA_HEADS, MLA_NOPE + MLA_V)
    kn = rms_norm(kv[..., :MLA_NOPE], g_kn)
    return kn, kv[..., MLA_NOPE:]


def mla_block(qn, qr, qpos, kn, kr, v, kpos):
    sc = (jnp.einsum('bqhd,bkhd->bhqk', qn, kn) + jnp.einsum('bqhr,bkr->bhqk', qr, kr)).astype(jnp.float32) * MLA_SCALE
    mask = (kpos[None, :] // CHUNK) <= (qpos[:, None] // CHUNK)
    sc = jnp.where(mask[None, None], sc, -jnp.inf)
    p = jax.nn.softmax(sc, axis=-1).astype(v.dtype)
    return jnp.einsum('bhqk,bkhd->bqhd', p, v)


def mla_attention_prompt(qn, qr, kn, kr, v, pos):
    b, s = qn.shape[0], qn.shape[1]
    nb = s // Q_BLOCK

    def to_blocks(t):
        return t.reshape((b, nb, Q_BLOCK) + t.shape[2:]).swapaxes(0, 1)

    out = lax.map(lambda a: mla_block(a[0], a[1], a[2], kn, kr, v, pos),
                  (to_blocks(qn), to_blocks(qr), pos.reshape(nb, Q_BLOCK)))
    return out.swapaxes(0, 1).reshape(b, s, MLA_WIDTH)


def merge_and_ffn(x, oa, ob, ga, gb, w_pa, w_pb, w_out, norm_ffn_g, w_up, w_down):
    mix = jax.nn.sigmoid(ga) * (oa @ w_pa) + jax.nn.sigmoid(gb) * (ob @ w_pb)
    h = x + mix @ w_out
    u = jnp.square(jax.nn.relu(rms_norm(h, norm_ffn_g) @ w_up))
    return h + u @ w_down


def setup_inputs(seed: int = 0) -> dict:
    key = jax.random.key(seed)
    ks = jax.random.split(key, 24)
    f32 = jnp.float32

    def nrm(k, shape, scale):
        return jax.random.normal(k, shape, f32) * scale

    def gain(k, n):
        return 1.0 + 0.01 * jax.random.normal(k, (DEPTH, n), f32)

    a_cache_len = min(BAND_PAST, PAST_LEN)
    return {
        'x_prompt': nrm(ks[0], (BATCH, SEQ, D_MODEL), 1.0),
        'x_sample': nrm(ks[1], (DEC_BATCH, DEC_SEQ, D_MODEL), 1.0),
        'cache_a_k': nrm(ks[2], (DEPTH, DEC_BATCH, a_cache_len, BAND_HEADS, BAND_HEAD_DIM), 1.0),
        'cache_a_v': nrm(ks[3], (DEPTH, DEC_BATCH, a_cache_len, BAND_HEADS, BAND_HEAD_DIM), 1.0),
        'cache_mla_ckv': nrm(ks[4], (DEPTH, DEC_BATCH, PAST_LEN, MLA_KV_RANK), 1.0),
        'cache_mla_krope': nrm(ks[5], (DEPTH, DEC_BATCH, PAST_LEN, MLA_ROPE), 1.0),
        'norm_mix_g': gain(ks[6], D_MODEL),
        'w_in': nrm(ks[7], (DEPTH, D_MODEL, D_IN_PROJ), D_MODEL ** -0.5),
        'g_aq': gain(ks[8], BAND_HEAD_DIM),
        'g_ak': gain(ks[9], BAND_HEAD_DIM),
        'rel_bias': nrm(ks[10], (DEPTH, BAND_HEADS, REL_SIZE), 0.5),
        'g_kv': gain(ks[11], MLA_KV_RANK),
        'g_kr': gain(ks[12], MLA_ROPE),
        'g_qn': gain(ks[13], MLA_NOPE),
        'g_qr': gain(ks[14], MLA_ROPE),
        'g_kn': gain(ks[15], MLA_NOPE),
        'w_kv_b': nrm(ks[16], (DEPTH, MLA_KV_RANK, MLA_HEADS * (MLA_NOPE + MLA_V)), MLA_KV_RANK ** -0.5),
        'w_pa': nrm(ks[17], (DEPTH, BAND_WIDTH, D_MODEL), BAND_WIDTH ** -0.5),
        'w_pb': nrm(ks[18], (DEPTH, MLA_WIDTH, D_MODEL), MLA_WIDTH ** -0.5),
        'w_out': nrm(ks[19], (DEPTH, D_MODEL, D_MODEL), D_MODEL ** -0.5),
        'norm_ffn_g': gain(ks[20], D_MODEL),
        'w_up': nrm(ks[21], (DEPTH, D_MODEL, D_FF), D_MODEL ** -0.5),
        'w_down': nrm(ks[22], (DEPTH, D_FF, D_MODEL), D_FF ** -0.5),
    }


def reference(x_prompt, x_sample, cache_a_k, cache_a_v, cache_mla_ckv, cache_mla_krope,
              norm_mix_g, w_in, g_aq, g_ak, rel_bias, g_kv, g_kr, g_qn, g_qr, g_kn,
              w_kv_b, w_pa, w_pb, w_out, norm_ffn_g, w_up, w_down):
    s = x_prompt.shape[1]
    t = x_sample.shape[1]
    past = cache_mla_ckv.shape[2]
    keep = min(BAND_PAST, s)
    pos_p = jnp.arange(s)
    pos_s = past + jnp.arange(t)
    kpos_s = jnp.arange(past + t)
    yp, ys = x_prompt, x_sample
    akp, avp, ckp, krp, aks, avs, cks, krs = [], [], [], [], [], [], [], []
    for l in range(DEPTH):
        aq, ak, av, qn, qr, ckv, kr, ga, gb = mixer_projections(
            yp, pos_p, norm_mix_g[l], w_in[l], g_aq[l], g_ak[l], g_kv[l], g_kr[l], g_qn[l], g_qr[l])
        oa = band_attention_prompt(aq, ak, av, rel_bias[l])
        kn, vb = mla_expand(ckv, w_kv_b[l], g_kn[l])
        ob = mla_attention_prompt(qn, qr, kn, kr, vb, pos_p)
        akp.append(ak[:, s - keep:])
        avp.append(av[:, s - keep:])
        ckp.append(ckv)
        krp.append(kr)
        yp = merge_and_ffn(yp, oa, ob, ga, gb, w_pa[l], w_pb[l], w_out[l], norm_ffn_g[l], w_up[l], w_down[l])
        aq, ak, av, qn, qr, ckv, kr, ga, gb = mixer_projections(
            ys, pos_s, norm_mix_g[l], w_in[l], g_aq[l], g_ak[l], g_kv[l], g_kr[l], g_qn[l], g_qr[l])
        oa = band_attention_sample(aq, ak, av, cache_a_k[l], cache_a_v[l], rel_bias[l])
        ckv_all = jnp.concatenate([cache_mla_ckv[l].astype(ckv.dtype), ckv], axis=1)
        kr_all = jnp.concatenate([cache_mla_krope[l].astype(kr.dtype), kr], axis=1)
        kn, vb = mla_expand(ckv_all, w_kv_b[l], g_kn[l])
        ob = mla_block(qn, qr, pos_s, kn, kr_all, vb, kpos_s).reshape(ys.shape[0], t, MLA_WIDTH)
        aks.append(ak)
        avs.append(av)
        cks.append(ckv)
        krs.append(kr)
        ys = merge_and_ffn(ys, oa, ob, ga, gb, w_pa[l], w_pb[l], w_out[l], norm_ffn_g[l], w_up[l], w_down[l])
    return (yp, ys, jnp.stack(akp), jnp.stack(avp), jnp.stack(ckp), jnp.stack(krp),
            jnp.stack(aks), jnp.stack(avs), jnp.stack(cks), jnp.stack(krs))
```

```python
import functools

import jax
import jax.numpy as jnp
from jax import lax
from jax.experimental import pallas as pl
from jax.experimental.pallas import tpu as pltpu

F32 = jnp.float32
BF16 = jnp.bfloat16

D_MODEL = 2048
CHUNK = 64
BAND_PAST = 8 * CHUNK
BAND_HEADS = 8
BAND_HEAD_DIM = 128
BAND_WIDTH = BAND_HEADS * BAND_HEAD_DIM
REL_MAX = 256
REL_SIZE = (CHUNK - 1) + REL_MAX + 1
BAND_SCALE = BAND_HEAD_DIM ** -0.5
MLA_HEADS = 8
MLA_NOPE = 128
MLA_ROPE = 64
MLA_QK = MLA_NOPE + MLA_ROPE
MLA_V = 128
MLA_WIDTH = MLA_HEADS * MLA_V
MLA_KV_RANK = 512
MLA_SCALE = MLA_QK ** -0.5
ROPE_THETA = 10000.0
D_FF = 4 * D_MODEL
EPS = 1e-6
NEG = -1e30

LANES = 128
MLA_QK_PAD = 2 * LANES
BAND_TQ = 256
BAND_WIN = BAND_TQ + BAND_PAST
VMEM_LIMIT = 56 * 1024 * 1024


def _params(n_axes, vmem=VMEM_LIMIT):
    return pltpu.CompilerParams(dimension_semantics=("arbitrary",) * n_axes,
                                vmem_limit_bytes=vmem)


def _rms(a, g):
    return a * lax.rsqrt(jnp.mean(a * a, axis=-1, keepdims=True) + EPS) * g


def _dot(a, b):
    return jnp.dot(a, b, preferred_element_type=F32)


def _dot_nt(a, b):
    return lax.dot_general(a, b, (((1,), (1,)), ((), ())), preferred_element_type=F32)


def _rope_pair(ar, g, cs):
    lane = lax.broadcasted_iota(jnp.int32, (1, LANES), 1)
    first = (lane < MLA_ROPE).astype(F32)
    ss = jnp.sum(ar * ar * first, axis=-1, keepdims=True) / MLA_ROPE
    return ar * lax.rsqrt(ss + EPS) * g * cs


def _proj_a_kernel(x_ref, gx_ref, w_ref, gh_ref, obf_ref, of_ref, xn_ref, *, n_norm):
    j = pl.program_id(1)

    @pl.when(j == 0)
    def _():
        xn_ref[...] = _rms(x_ref[...], gx_ref[...]).astype(BF16)

    acc = _dot(xn_ref[...], w_ref[...])
    tn = acc.shape[1]

    @pl.when(j < n_norm)
    def _():
        for k in range(tn // LANES):
            sl = slice(k * LANES, (k + 1) * LANES)
            y = _rms(acc[:, sl], gh_ref[:, sl])
            of_ref[:, sl] = y
            obf_ref[:, sl] = y.astype(BF16)

    @pl.when(j >= n_norm)
    def _():
        of_ref[...] = acc
        obf_ref[...] = acc.astype(BF16)


def _proj_a(x, gx, w, gh, tm, tn=512):
    m = x.shape[0]
    n = w.shape[1]
    return pl.pallas_call(
        functools.partial(_proj_a_kernel, n_norm=2 * BAND_WIDTH // tn),
        grid=(m // tm, n // tn),
        in_specs=[pl.BlockSpec((tm, D_MODEL), lambda i, j: (i, 0)),
                  pl.BlockSpec((1, D_MODEL), lambda i, j: (0, 0)),
                  pl.BlockSpec((D_MODEL, tn), lambda i, j: (0, j)),
                  pl.BlockSpec((1, tn), lambda i, j: (0, j))],
        out_specs=[pl.BlockSpec((tm, tn), lambda i, j: (i, j)),
                   pl.BlockSpec((tm, tn), lambda i, j: (i, j))],
        out_shape=[jax.ShapeDtypeStruct((m, n), BF16), jax.ShapeDtypeStruct((m, n), F32)],
        scratch_shapes=[pltpu.VMEM((tm, D_MODEL), BF16)],
        compiler_params=_params(2), name="proj_a")(x, gx, w, gh)


def _proj_q_kernel(x_ref, gx_ref, w_ref, gn_ref, gr_ref, cs_ref, o_ref, xn_ref):
    @pl.when(pl.program_id(1) == 0)
    def _():
        xn_ref[...] = _rms(x_ref[...], gx_ref[...]).astype(BF16)

    acc = _dot(xn_ref[...], w_ref[...])
    cs = cs_ref[...]
    for k in range(acc.shape[1] // MLA_QK_PAD):
        c0 = k * MLA_QK_PAD
        o_ref[:, c0:c0 + LANES] = _rms(acc[:, c0:c0 + LANES], gn_ref[...]).astype(BF16)
        o_ref[:, c0 + LANES:c0 + 2 * LANES] = _rope_pair(
            acc[:, c0 + LANES:c0 + 2 * LANES], gr_ref[...], cs).astype(BF16)


def _proj_q(x, gx, w, gn, gr, cs, tm, tn=512):
    m = x.shape[0]
    n = w.shape[1]
    n_cs = cs.shape[0] // tm
    return pl.pallas_call(
        _proj_q_kernel,
        grid=(m // tm, n // tn),
        in_specs=[pl.BlockSpec((tm, D_MODEL), lambda i, j: (i, 0)),
                  pl.BlockSpec((1, D_MODEL), lambda i, j: (0, 0)),
                  pl.BlockSpec((D_MODEL, tn), lambda i, j: (0, j)),
                  pl.BlockSpec((1, LANES), lambda i, j: (0, 0)),
                  pl.BlockSpec((1, LANES), lambda i, j: (0, 0)),
                  pl.BlockSpec((tm, LANES), lambda i, j: (i % n_cs, 0))],
        out_specs=pl.BlockSpec((tm, tn), lambda i, j: (i, j)),
        out_shape=jax.ShapeDtypeStruct((m, n), BF16),
        scratch_shapes=[pltpu.VMEM((tm, D_MODEL), BF16)],
        compiler_params=_params(2), name="proj_q")(x, gx, w, gn, gr, cs)


def _proj_c_kernel(x_ref, gx_ref, w_ref, gkv_ref, gr_ref, cs_ref, ockv_ref, ockvb_ref, okr_ref):
    xn = _rms(x_ref[...], gx_ref[...]).astype(BF16)
    acc = _dot(xn, w_ref[...])
    cn = _rms(acc[:, :MLA_KV_RANK], gkv_ref[...])
    ockv_ref[...] = cn
    ockvb_ref[...] = cn.astype(BF16)
    t = _rope_pair(acc[:, MLA_KV_RANK:], gr_ref[...], cs_ref[...])
    okr_ref[...] = t + pltpu.roll(t, MLA_ROPE, 1)


def _proj_c(x, gx, w, gkv, gr, cs, tm):
    m = x.shape[0]
    n = w.shape[1]
    n_cs = cs.shape[0] // tm
    return pl.pallas_call(
        _proj_c_kernel,
        grid=(m // tm,),
        in_specs=[pl.BlockSpec((tm, D_MODEL), lambda i: (i, 0)),
                  pl.BlockSpec((1, D_MODEL), lambda i: (0, 0)),
                  pl.BlockSpec((D_MODEL, n), lambda i: (0, 0)),
                  pl.BlockSpec((1, MLA_KV_RANK), lambda i: (0, 0)),
                  pl.BlockSpec((1, LANES), lambda i: (0, 0)),
                  pl.BlockSpec((tm, LANES), lambda i: (i % n_cs, 0))],
        out_specs=[pl.BlockSpec((tm, MLA_KV_RANK), lambda i: (i, 0)),
                   pl.BlockSpec((tm, MLA_KV_RANK), lambda i: (i, 0)),
                   pl.BlockSpec((tm, LANES), lambda i: (i, 0))],
        out_shape=[jax.ShapeDtypeStruct((m, MLA_KV_RANK), F32),
                   jax.ShapeDtypeStruct((m, MLA_KV_RANK), BF16),
                   jax.ShapeDtypeStruct((m, LANES), F32)],
        compiler_params=_params(1), name="proj_c")(x, gx, w, gkv, gr, cs)


def _expand_kernel(c_ref, w_ref, gkn_ref, krr_ref, k_ref, v_ref):
    c = c_ref[...]
    krr = krr_ref[...].astype(BF16)
    for h in range(MLA_HEADS):
        a = _dot(c, w_ref[:, h * 2 * LANES:(h + 1) * 2 * LANES])
        k_ref[:, h * MLA_QK_PAD:h * MLA_QK_PAD + LANES] = _rms(a[:, :LANES], gkn_ref[...]).astype(BF16)
        k_ref[:, h * MLA_QK_PAD + LANES:(h + 1) * MLA_QK_PAD] = krr
        v_ref[:, h * MLA_V:(h + 1) * MLA_V] = a[:, LANES:].astype(BF16)


def _expand(c, w, gkn, krr, tm):
    m = c.shape[0]
    return pl.pallas_call(
        _expand_kernel,
        grid=(m // tm,),
        in_specs=[pl.BlockSpec((tm, MLA_KV_RANK), lambda i: (i, 0)),
                  pl.BlockSpec(w.shape, lambda i: (0, 0)),
                  pl.BlockSpec((1, LANES), lambda i: (0, 0)),
                  pl.BlockSpec((tm, LANES), lambda i: (i, 0))],
        out_specs=[pl.BlockSpec((tm, MLA_HEADS * MLA_QK_PAD), lambda i: (i, 0)),
                   pl.BlockSpec((tm, MLA_WIDTH), lambda i: (i, 0))],
        out_shape=[jax.ShapeDtypeStruct((m, MLA_HEADS * MLA_QK_PAD), BF16),
                   jax.ShapeDtypeStruct((m, MLA_WIDTH), BF16)],
        compiler_params=_params(1), name="expand")(c, w, gkn, krr)


def _band_bias_kernel(b_ref, o_ref):
    t = pltpu.roll(jnp.broadcast_to(b_ref[0], (BAND_TQ, 2 * BAND_PAST)), 0, 1,
                   stride=1, stride_axis=0)[:, :BAND_WIN]
    qc = lax.broadcasted_iota(jnp.int32, (BAND_TQ, BAND_WIN), 0) // CHUNK
    kc = lax.broadcasted_iota(jnp.int32, (BAND_TQ, BAND_WIN), 1) // CHUNK
    o_ref[0] = jnp.where(kc >= qc, jnp.where(kc <= qc + BAND_PAST // CHUNK, t, NEG), NEG)


def _band_bias(rel_bias):
    far = jnp.broadcast_to(rel_bias[:, REL_SIZE - 1:], (BAND_HEADS, BAND_PAST - REL_MAX))
    near = jnp.broadcast_to(rel_bias[:, :1], (BAND_HEADS, BAND_WIN - BAND_PAST - CHUNK))
    wrap = jnp.broadcast_to(rel_bias[:, REL_SIZE - 1:], (BAND_HEADS, 2 * BAND_PAST - BAND_WIN))
    row0 = jnp.concatenate([far, rel_bias[:, ::-1], near, wrap], axis=1)
    assert row0.shape == (BAND_HEADS, 2 * BAND_PAST)
    return pl.pallas_call(
        _band_bias_kernel,
        grid=(BAND_HEADS,),
        in_specs=[pl.BlockSpec((1, 1, 2 * BAND_PAST), lambda h: (h, 0, 0))],
        out_specs=pl.BlockSpec((1, BAND_TQ, BAND_WIN), lambda h: (h, 0, 0)),
        out_shape=jax.ShapeDtypeStruct((BAND_HEADS, BAND_TQ, BAND_WIN), F32),
        compiler_params=_params(1), name="band_bias")(row0[:, None, :])


def _band_prompt_kernel(q_ref, ka_ref, kb_ref, kc_ref, va_ref, vb_ref, vc_ref, bias_ref, o_ref):
    i = pl.program_id(1)
    k_refs = (ka_ref, kb_ref, kc_ref)
    v_refs = (va_ref, vb_ref, vc_ref)
    n_piece = len(k_refs)
    for h in range(BAND_HEADS):
        hs = slice(h * BAND_HEAD_DIM, (h + 1) * BAND_HEAD_DIM)
        q = q_ref[0, :, hs]
        s = []
        for p in range(n_piece):
            sp = _dot_nt(q, k_refs[p][0, :, hs]) * BAND_SCALE + bias_ref[h, :, p * BAND_TQ:(p + 1) * BAND_TQ]
            if p < n_piece - 1:
                sp = jnp.where(i >= n_piece - 1 - p, sp, NEG)
            s.append(sp)
        m = functools.reduce(jnp.maximum, [jnp.max(sp, axis=-1, keepdims=True) for sp in s])
        e = [jnp.exp(sp - m) for sp in s]
        l = functools.reduce(jnp.add, [jnp.sum(ep, axis=-1, keepdims=True) for ep in e])
        o = functools.reduce(jnp.add, [_dot(e[p].astype(BF16), v_refs[p][0, :, hs]) for p in range(n_piece)])
        o_ref[0, :, hs] = (o / l).astype(BF16)


def _band_prompt(qkv, bias):
    b, s, _ = qkv.shape
    n_piece = BAND_WIN // BAND_TQ
    blk = (1, BAND_TQ, BAND_WIDTH)

    def kv_spec(col, back):
        return pl.BlockSpec(blk, lambda bi, i: (bi, jnp.maximum(i - back, 0), col))

    return pl.pallas_call(
        _band_prompt_kernel,
        grid=(b, s // BAND_TQ),
        in_specs=[pl.BlockSpec(blk, lambda bi, i: (bi, i, 0))]
                 + [kv_spec(1, n_piece - 1 - p) for p in range(n_piece)]
                 + [kv_spec(2, n_piece - 1 - p) for p in range(n_piece)]
                 + [pl.BlockSpec(bias.shape, lambda bi, i: (0, 0, 0))],
        out_specs=pl.BlockSpec(blk, lambda bi, i: (bi, i, 0)),
        out_shape=jax.ShapeDtypeStruct((b, s, BAND_WIDTH), BF16),
        compiler_params=_params(2), name="band_prompt")(qkv, qkv, qkv, qkv, qkv, qkv, qkv, bias)


def _band_sample_kernel(q_ref, kn_ref, vn_ref, kc_ref, vc_ref, bias_ref, o_ref):
    n_past = kc_ref.shape[1]
    t = q_ref.shape[1]
    for h in range(BAND_HEADS):
        hs = slice(h * BAND_HEAD_DIM, (h + 1) * BAND_HEAD_DIM)
        q = q_ref[0, :, hs]
        s1 = _dot_nt(q, kc_ref[0, :, hs]) * BAND_SCALE + bias_ref[h, :t, :n_past]
        s2 = _dot_nt(q, kn_ref[0, :, hs]) * BAND_SCALE + bias_ref[h, :t, n_past:n_past + t]
        m = jnp.maximum(jnp.max(s1, axis=-1, keepdims=True), jnp.max(s2, axis=-1, keepdims=True))
        e1 = jnp.exp(s1 - m)
        e2 = jnp.exp(s2 - m)
        l = jnp.sum(e1, axis=-1, keepdims=True) + jnp.sum(e2, axis=-1, keepdims=True)
        o = _dot(e1.astype(BF16), vc_ref[0, :, hs]) + _dot(e2.astype(BF16), vn_ref[0, :, hs])
        o_ref[0, :, hs] = (o / l).astype(BF16)


def _band_sample(qkv, k_cache, v_cache, bias):
    b, t, _ = qkv.shape
    n_past = k_cache.shape[1]
    new = (1, t, BAND_WIDTH)
    old = (1, n_past, BAND_WIDTH)
    return pl.pallas_call(
        _band_sample_kernel,
        grid=(b,),
        in_specs=[pl.BlockSpec(new, lambda bi: (bi, 0, 0)),
                  pl.BlockSpec(new, lambda bi: (bi, 0, 1)),
                  pl.BlockSpec(new, lambda bi: (bi, 0, 2)),
                  pl.BlockSpec(old, lambda bi: (bi, 0, 0)),
                  pl.BlockSpec(old, lambda bi: (bi, 0, 0)),
                  pl.BlockSpec(bias.shape, lambda bi: (0, 0, 0))],
        out_specs=pl.BlockSpec(new, lambda bi: (bi, 0, 0)),
        out_shape=jax.ShapeDtypeStruct((b, t, BAND_WIDTH), BF16),
        compiler_params=_params(1), name="band_sample")(qkv, qkv, qkv, k_cache, v_cache, bias)


def _mla_prompt_kernel(q_ref, k_ref, v_ref, o_ref, m_ref, l_ref, acc_ref, *, tq):
    i = pl.program_id(2)
    q = q_ref[0]
    m_ref[...] = jnp.full(m_ref.shape, NEG, F32)
    l_ref[...] = jnp.zeros(l_ref.shape, F32)
    acc_ref[...] = jnp.zeros(acc_ref.shape, F32)

    def update(j, masked):
        start = pl.multiple_of(j * tq, tq)
        s = _dot_nt(q, k_ref[0, pl.ds(start, tq), :]) * MLA_SCALE
        if masked:
            qc = lax.broadcasted_iota(jnp.int32, (tq, tq), 0) // CHUNK
            kc = lax.broadcasted_iota(jnp.int32, (tq, tq), 1) // CHUNK
            s = jnp.where(kc <= qc, s, NEG)
        m_prev = m_ref[...]
        m_new = jnp.maximum(m_prev, jnp.max(s, axis=-1, keepdims=True))
        alpha = jnp.exp(m_prev - m_new)
        p = jnp.exp(s - jnp.tile(m_new, (1, tq // LANES)))
        l_ref[...] = alpha * l_ref[...] + jnp.sum(p, axis=-1, keepdims=True)
        acc_ref[...] = alpha * acc_ref[...] + _dot(p.astype(BF16), v_ref[0, pl.ds(start, tq), :])
        m_ref[...] = m_new

    def body(j, carry):
        update(j, False)
        return carry

    lax.fori_loop(0, i, body, 0)
    update(i, True)
    o_ref[0] = (acc_ref[...] / l_ref[...]).astype(BF16)


def _mla_prompt(q, k, v, tq=512):
    b, s, _ = q.shape
    return pl.pallas_call(
        functools.partial(_mla_prompt_kernel, tq=tq),
        grid=(b, MLA_HEADS, s // tq),
        in_specs=[pl.BlockSpec((1, tq, MLA_QK_PAD), lambda bi, h, i: (bi, i, h)),
                  pl.BlockSpec((1, s, MLA_QK_PAD), lambda bi, h, i: (bi, 0, h)),
                  pl.BlockSpec((1, s, MLA_V), lambda bi, h, i: (bi, 0, h))],
        out_specs=pl.BlockSpec((1, tq, MLA_V), lambda bi, h, i: (bi, i, h)),
        out_shape=jax.ShapeDtypeStruct((b, s, MLA_WIDTH), BF16),
        scratch_shapes=[pltpu.VMEM((tq, LANES), F32), pltpu.VMEM((tq, LANES), F32),
                        pltpu.VMEM((tq, MLA_V), F32)],
        compiler_params=_params(3), name="mla_prompt")(q, k, v)


def _mla_sample_kernel(q_ref, k_ref, v_ref, o_ref):
    s = _dot_nt(q_ref[0], k_ref[0]) * MLA_SCALE
    e = jnp.exp(s - jnp.max(s, axis=-1, keepdims=True))
    o = _dot(e.astype(BF16), v_ref[0])
    o_ref[0] = (o / jnp.sum(e, axis=-1, keepdims=True)).astype(BF16)


def _mla_sample(q, k, v):
    b, t, _ = q.shape
    l = k.shape[1]
    return pl.pallas_call(
        _mla_sample_kernel,
        grid=(b, MLA_HEADS),
        in_specs=[pl.BlockSpec((1, t, MLA_QK_PAD), lambda bi, h: (bi, 0, h)),
                  pl.BlockSpec((1, l, MLA_QK_PAD), lambda bi, h: (bi, 0, h)),
                  pl.BlockSpec((1, l, MLA_V), lambda bi, h: (bi, 0, h))],
        out_specs=pl.BlockSpec((1, t, MLA_V), lambda bi, h: (bi, 0, h)),
        out_shape=jax.ShapeDtypeStruct((b, t, MLA_WIDTH), BF16),
        compiler_params=_params(2), name="mla_sample")(q, k, v)


def _mix_kernel(x_ref, gx_ref, oa_ref, ob_ref, wga_ref, wgb_ref, wpa_ref, wpb_ref, o_ref, xn_ref):
    @pl.when(pl.program_id(1) == 0)
    def _():
        xn_ref[...] = _rms(x_ref[...], gx_ref[...]).astype(BF16)

    xn = xn_ref[...]
    a = jax.nn.sigmoid(_dot(xn, wga_ref[...])) * _dot(oa_ref[...], wpa_ref[...])
    b = jax.nn.sigmoid(_dot(xn, wgb_ref[...])) * _dot(ob_ref[...], wpb_ref[...])
    o_ref[...] = (a + b).astype(BF16)


def _mix(x, gx, oa, ob, wga, wgb, wpa, wpb, tm, tn=512):
    m = x.shape[0]
    return pl.pallas_call(
        _mix_kernel,
        grid=(m // tm, D_MODEL // tn),
        in_specs=[pl.BlockSpec((tm, D_MODEL), lambda i, j: (i, 0)),
                  pl.BlockSpec((1, D_MODEL), lambda i, j: (0, 0)),
                  pl.BlockSpec((tm, BAND_WIDTH), lambda i, j: (i, 0)),
                  pl.BlockSpec((tm, MLA_WIDTH), lambda i, j: (i, 0)),
                  pl.BlockSpec((D_MODEL, tn), lambda i, j: (0, j)),
                  pl.BlockSpec((D_MODEL, tn), lambda i, j: (0, j)),
                  pl.BlockSpec((BAND_WIDTH, tn), lambda i, j: (0, j)),
                  pl.BlockSpec((MLA_WIDTH, tn), lambda i, j: (0, j))],
        out_specs=pl.BlockSpec((tm, tn), lambda i, j: (i, j)),
        out_shape=jax.ShapeDtypeStruct((m, D_MODEL), BF16),
        scratch_shapes=[pltpu.VMEM((tm, D_MODEL), BF16)],
        compiler_params=_params(2), name="mix")(x, gx, oa, ob, wga, wgb, wpa, wpb)


def _outproj_kernel(x_ref, mix_ref, w_ref, o_ref):
    o_ref[...] = x_ref[...] + _dot(mix_ref[...], w_ref[...])


def _outproj(x, mix, w, tm, tn=512):
    m = x.shape[0]
    return pl.pallas_call(
        _outproj_kernel,
        grid=(m // tm, D_MODEL // tn),
        in_specs=[pl.BlockSpec((tm, tn), lambda i, j: (i, j)),
                  pl.BlockSpec((tm, D_MODEL), lambda i, j: (i, 0)),
                  pl.BlockSpec((D_MODEL, tn), lambda i, j: (0, j))],
        out_specs=pl.BlockSpec((tm, tn), lambda i, j: (i, j)),
        out_shape=jax.ShapeDtypeStruct((m, D_MODEL), F32),
        compiler_params=_params(2), name="outproj")(x, mix, w)


def _ffn_kernel(h_ref, g_ref, wu_ref, wd_ref, o_ref, hn_ref):
    @pl.when(pl.program_id(1) == 0)
    def _():
        h = h_ref[...]
        hn_ref[...] = _rms(h, g_ref[...]).astype(BF16)
        o_ref[...] = h

    u = jnp.maximum(_dot(hn_ref[...], wu_ref[...]), 0.0)
    o_ref[...] += _dot((u * u).astype(BF16), wd_ref[...])


def _ffn(h, g, wu, wd, tm, tf=512):
    m = h.shape[0]
    return pl.pallas_call(
        _ffn_kernel,
        grid=(m // tm, D_FF // tf),
        in_specs=[pl.BlockSpec((tm, D_MODEL), lambda i, f: (i, 0)),
                  pl.BlockSpec((1, D_MODEL), lambda i, f: (0, 0)),
                  pl.BlockSpec((D_MODEL, tf), lambda i, f: (0, f)),
                  pl.BlockSpec((tf, D_MODEL), lambda i, f: (f, 0))],
        out_specs=pl.BlockSpec((tm, D_MODEL), lambda i, f: (i, 0)),
        out_shape=jax.ShapeDtypeStruct((m, D_MODEL), F32),
        scratch_shapes=[pltpu.VMEM((tm, D_MODEL), BF16)],
        compiler_params=_params(2), name="ffn")(h, g, wu, wd)


def _rope_table(pos, rows):
    half = MLA_ROPE // 2
    freqs = ROPE_THETA ** (-(jnp.arange(half, dtype=F32) / half))
    ang = pos.astype(F32)[:, None] * freqs[None, :]
    cos = jnp.cos(ang)
    sin = jnp.sin(ang)
    table = jnp.concatenate([cos, cos, -sin, sin], axis=1)
    return jnp.tile(table, (rows // table.shape[0], 1))


def _with_partner(w):
    half = MLA_ROPE // 2
    return jnp.concatenate([w, w[..., half:], w[..., :half]], axis=-1)


def _layer_weights(l, norm_mix_g, w_in, g_aq, g_ak, g_kv, g_kr, g_qn, g_qr, g_kn,
                   w_kv_b, w_pa, w_pb, w_out, norm_ffn_g, w_up, w_down):
    w = w_in[l]
    c = 0
    parts = []
    for width in (3 * BAND_WIDTH, MLA_HEADS * MLA_QK, MLA_KV_RANK, MLA_ROPE, D_MODEL, D_MODEL):
        parts.append(w[:, c:c + width])
        c += width
    w_a, w_bq, w_ckv, w_kr, w_ga, w_gb = parts
    w_bq = w_bq.reshape(D_MODEL, MLA_HEADS, MLA_QK)
    w_q = jnp.concatenate([w_bq[..., :MLA_NOPE], _with_partner(w_bq[..., MLA_NOPE:])], axis=-1)
    row = lambda g: g[l][None, :].astype(F32)
    return dict(
        gx=row(norm_mix_g),
        w_a=w_a.astype(BF16),
        g_a=jnp.concatenate([jnp.tile(g_aq[l], BAND_HEADS), jnp.tile(g_ak[l], BAND_HEADS),
                             jnp.ones((BAND_WIDTH,), F32)])[None, :],
        w_q=w_q.reshape(D_MODEL, MLA_HEADS * MLA_QK_PAD).astype(BF16),
        g_qn=row(g_qn),
        g_qr=_with_partner(g_qr[l])[None, :],
        w_c=jnp.concatenate([w_ckv, _with_partner(w_kr)], axis=1).astype(BF16),
        g_kv=row(g_kv),
        g_kr=_with_partner(g_kr[l])[None, :],
        w_kv_b=w_kv_b[l].astype(BF16),
        g_kn=row(g_kn),
        w_ga=w_ga.astype(BF16), w_gb=w_gb.astype(BF16),
        w_pa=w_pa[l].astype(BF16), w_pb=w_pb[l].astype(BF16),
        w_out=w_out[l].astype(BF16),
        g_ffn=row(norm_ffn_g),
        w_up=w_up[l].astype(BF16), w_down=w_down[l].astype(BF16),
    )


def _project(x2, p, cs, tm):
    qkv_bf, qkv_f = _proj_a(x2, p["gx"], p["w_a"], p["g_a"], tm)
    q_mla = _proj_q(x2, p["gx"], p["w_q"], p["g_qn"], p["g_qr"], cs, tm)
    ckv, ckv_bf, krr = _proj_c(x2, p["gx"], p["w_c"], p["g_kv"], p["g_kr"], cs, tm)
    return qkv_bf, qkv_f, q_mla, ckv, ckv_bf, krr


def _merge_ffn(x2, oa, ob, p, tm):
    mix = _mix(x2, p["gx"], oa, ob, p["w_ga"], p["w_gb"], p["w_pa"], p["w_pb"], tm)
    h = _outproj(x2, mix, p["w_out"], tm)
    return _ffn(h, p["g_ffn"], p["w_up"], p["w_down"], tm)


def kernel(x_prompt, x_sample, cache_a_k, cache_a_v, cache_mla_ckv, cache_mla_krope, norm_mix_g, w_in, g_aq, g_ak, rel_bias, g_kv, g_kr, g_qn, g_qr, g_kn, w_kv_b, w_pa, w_pb, w_out, norm_ffn_g, w_up, w_down):
    b, s, _ = x_prompt.shape
    bs, t, _ = x_sample.shape
    past = cache_mla_ckv.shape[2]
    depth = w_in.shape[0]
    keep = min(BAND_PAST, s)
    tm_p = 512
    tm_s = bs * t
    assert s % tm_p == 0 and s % BAND_TQ == 0 and t == CHUNK and past % CHUNK == 0
    cs_p = _rope_table(jnp.arange(s), max(s, tm_p))
    cs_s = _rope_table(past + jnp.arange(t), tm_s)

    yp = x_prompt.reshape(b * s, D_MODEL)
    ys = x_sample.reshape(bs * t, D_MODEL)
    outs = [[] for _ in range(8)]
    for l in range(depth):
        p = _layer_weights(l, norm_mix_g, w_in, g_aq, g_ak, g_kv, g_kr, g_qn, g_qr, g_kn,
                           w_kv_b, w_pa, w_pb, w_out, norm_ffn_g, w_up, w_down)
        bias = _band_bias(rel_bias[l])

        qkv_bf, qkv_f, q_mla, ckv, ckv_bf, krr = _project(yp, p, cs_p, tm_p)
        oa = _band_prompt(qkv_bf.reshape(b, s, 3 * BAND_WIDTH), bias)
        k_mla, v_mla = _expand(ckv_bf, p["w_kv_b"], p["g_kn"], krr, tm_p)
        ob = _mla_prompt(q_mla.reshape(b, s, -1), k_mla.reshape(b, s, -1), v_mla.reshape(b, s, -1))
        kv_f = qkv_f.reshape(b, s, 3, BAND_HEADS, BAND_HEAD_DIM)
        outs[0].append(kv_f[:, s - keep:, 1])
        outs[1].append(kv_f[:, s - keep:, 2])
        outs[2].append(ckv.reshape(b, s, MLA_KV_RANK))
        outs[3].append(krr[:, :MLA_ROPE].reshape(b, s, MLA_ROPE))
        yp = _merge_ffn(yp, oa.reshape(b * s, BAND_WIDTH), ob.reshape(b * s, MLA_WIDTH), p, tm_p)

        qkv_bf, qkv_f, q_mla, ckv, ckv_bf, krr = _project(ys, p, cs_s, tm_s)
        n_band = cache_a_k.shape[2]
        oa = _band_sample(qkv_bf.reshape(bs, t, 3 * BAND_WIDTH),
                          cache_a_k[l].reshape(bs, n_band, BAND_WIDTH).astype(BF16),
                          cache_a_v[l].reshape(bs, n_band, BAND_WIDTH).astype(BF16), bias)
        ckv_all = jnp.concatenate([cache_mla_ckv[l].astype(BF16), ckv_bf.reshape(bs, t, MLA_KV_RANK)], axis=1)
        kr_old = cache_mla_krope[l].astype(F32)
        krr_all = jnp.concatenate([jnp.concatenate([kr_old, kr_old], axis=-1), krr.reshape(bs, t, LANES)], axis=1)
        n_all = past + t
        k_mla, v_mla = _expand(ckv_all.reshape(bs * n_all, MLA_KV_RANK), p["w_kv_b"], p["g_kn"],
                               krr_all.reshape(bs * n_all, LANES), n_all // 4)
        ob = _mla_sample(q_mla.reshape(bs, t, -1), k_mla.reshape(bs, n_all, -1), v_mla.reshape(bs, n_all, -1))
        kv_f = qkv_f.reshape(bs, t, 3, BAND_HEADS, BAND_HEAD_DIM)
        outs[4].append(kv_f[:, :, 1])
        outs[5].append(kv_f[:, :, 2])
        outs[6].append(ckv.reshape(bs, t, MLA_KV_RANK))
        outs[7].append(krr[:, :MLA_ROPE].reshape(bs, t, MLA_ROPE))
        ys = _merge_ffn(ys, oa.reshape(bs * t, BAND_WIDTH), ob.reshape(bs * t, MLA_WIDTH), p, tm_s)

    return (yp.reshape(b, s, D_MODEL), ys.reshape(bs, t, D_MODEL)) + tuple(jnp.stack(o) for o in outs)
```

```python
import functools

import jax
import jax.numpy as jnp
from jax import lax
from jax.experimental import pallas as pl
from jax.experimental.pallas import tpu as pltpu

F32 = jnp.float32
BF16 = jnp.bfloat16

D_MODEL = 2048
CHUNK = 64
BAND_PAST = 8 * CHUNK
BAND_HEADS = 8
BAND_HEAD_DIM = 128
BAND_WIDTH = BAND_HEADS * BAND_HEAD_DIM
REL_MAX = 256
REL_SIZE = (CHUNK - 1) + REL_MAX + 1
BAND_SCALE = BAND_HEAD_DIM ** -0.5
MLA_HEADS = 8
MLA_NOPE = 128
MLA_ROPE = 64
MLA_QK = MLA_NOPE + MLA_ROPE
MLA_V = 128
MLA_WIDTH = MLA_HEADS * MLA_V
MLA_KV_RANK = 512
MLA_SCALE = MLA_QK ** -0.5
ROPE_THETA = 10000.0
D_FF = 4 * D_MODEL
EPS = 1e-6
NEG = -1e30
LOG2E = 1.4426950408889634
MLA_QSCALE = MLA_SCALE * LOG2E

LANES = 128
MLA_QK_PAD = 2 * LANES
BAND_TQ = 256
BAND_WIN = BAND_TQ + BAND_PAST
VMEM_LIMIT = 56 * 1024 * 1024


def _params(n_axes, vmem=VMEM_LIMIT):
    return pltpu.CompilerParams(dimension_semantics=("arbitrary",) * n_axes,
                                vmem_limit_bytes=vmem)


def _rms(a, g):
    return a * lax.rsqrt(jnp.mean(a * a, axis=-1, keepdims=True) + EPS) * g


def _dot(a, b):
    return jnp.dot(a, b, preferred_element_type=F32)


def _dot_nt(a, b):
    return lax.dot_general(a, b, (((1,), (1,)), ((), ())), preferred_element_type=F32)


def _rope_pair(ar, g, cs):
    lane = lax.broadcasted_iota(jnp.int32, (1, LANES), 1)
    first = (lane < MLA_ROPE).astype(F32)
    ss = jnp.sum(ar * ar * first, axis=-1, keepdims=True) / MLA_ROPE
    return ar * lax.rsqrt(ss + EPS) * g * cs


def _proj_a_kernel(x_ref, gx_ref, w_ref, gh_ref, obf_ref, of_ref, xn_ref, *, n_norm):
    j = pl.program_id(1)

    @pl.when(j == 0)
    def _():
        xn_ref[...] = _rms(x_ref[...], gx_ref[...]).astype(BF16)

    acc = _dot(xn_ref[...], w_ref[...])
    tn = acc.shape[1]

    @pl.when(j < n_norm)
    def _():
        for k in range(tn // LANES):
            sl = slice(k * LANES, (k + 1) * LANES)
            y = _rms(acc[:, sl], gh_ref[:, sl])
            of_ref[:, sl] = y
            obf_ref[:, sl] = y.astype(BF16)

    @pl.when(j >= n_norm)
    def _():
        of_ref[...] = acc
        obf_ref[...] = acc.astype(BF16)


def _proj_a(x, gx, w, gh, tm, tn=512):
    m = x.shape[0]
    n = w.shape[1]
    return pl.pallas_call(
        functools.partial(_proj_a_kernel, n_norm=2 * BAND_WIDTH // tn),
        grid=(m // tm, n // tn),
        in_specs=[pl.BlockSpec((tm, D_MODEL), lambda i, j: (i, 0)),
                  pl.BlockSpec((1, D_MODEL), lambda i, j: (0, 0)),
                  pl.BlockSpec((D_MODEL, tn), lambda i, j: (0, j)),
                  pl.BlockSpec((1, tn), lambda i, j: (0, j))],
        out_specs=[pl.BlockSpec((tm, tn), lambda i, j: (i, j)),
                   pl.BlockSpec((tm, tn), lambda i, j: (i, j))],
        out_shape=[jax.ShapeDtypeStruct((m, n), BF16), jax.ShapeDtypeStruct((m, n), F32)],
        scratch_shapes=[pltpu.VMEM((tm, D_MODEL), BF16)],
        compiler_params=_params(2), name="proj_a")(x, gx, w, gh)


def _proj_q_kernel(x_ref, gx_ref, w_ref, gn_ref, gr_ref, cs_ref, o_ref, xn_ref):
    @pl.when(pl.program_id(1) == 0)
    def _():
        xn_ref[...] = _rms(x_ref[...], gx_ref[...]).astype(BF16)

    acc = _dot(xn_ref[...], w_ref[...])
    cs = cs_ref[...]
    for k in range(acc.shape[1] // MLA_QK_PAD):
        c0 = k * MLA_QK_PAD
        qn = _rms(acc[:, c0:c0 + LANES], gn_ref[...]) * MLA_QSCALE
        qr = _rope_pair(acc[:, c0 + LANES:c0 + 2 * LANES], gr_ref[...], cs) * MLA_QSCALE
        o_ref[:, c0:c0 + LANES] = qn.astype(BF16)
        o_ref[:, c0 + LANES:c0 + 2 * LANES] = qr.astype(BF16)


def _proj_q(x, gx, w, gn, gr, cs, tm, tn=512):
    m = x.shape[0]
    n = w.shape[1]
    n_cs = cs.shape[0] // tm
    return pl.pallas_call(
        _proj_q_kernel,
        grid=(m // tm, n // tn),
        in_specs=[pl.BlockSpec((tm, D_MODEL), lambda i, j: (i, 0)),
                  pl.BlockSpec((1, D_MODEL), lambda i, j: (0, 0)),
                  pl.BlockSpec((D_MODEL, tn), lambda i, j: (0, j)),
                  pl.BlockSpec((1, LANES), lambda i, j: (0, 0)),
                  pl.BlockSpec((1, LANES), lambda i, j: (0, 0)),
                  pl.BlockSpec((tm, LANES), lambda i, j: (i % n_cs, 0))],
        out_specs=pl.BlockSpec((tm, tn), lambda i, j: (i, j)),
        out_shape=jax.ShapeDtypeStruct((m, n), BF16),
        scratch_shapes=[pltpu.VMEM((tm, D_MODEL), BF16)],
        compiler_params=_params(2), name="proj_q")(x, gx, w, gn, gr, cs)


def _proj_c_kernel(x_ref, gx_ref, w_ref, gkv_ref, gr_ref, cs_ref, ockv_ref, ockvb_ref, okr_ref):
    xn = _rms(x_ref[...], gx_ref[...]).astype(BF16)
    acc = _dot(xn, w_ref[...])
    cn = _rms(acc[:, :MLA_KV_RANK], gkv_ref[...])
    ockv_ref[...] = cn
    ockvb_ref[...] = cn.astype(BF16)
    t = _rope_pair(acc[:, MLA_KV_RANK:], gr_ref[...], cs_ref[...])
    okr_ref[...] = t + pltpu.roll(t, MLA_ROPE, 1)


def _proj_c(x, gx, w, gkv, gr, cs, tm):
    m = x.shape[0]
    n = w.shape[1]
    n_cs = cs.shape[0] // tm
    return pl.pallas_call(
        _proj_c_kernel,
        grid=(m // tm,),
        in_specs=[pl.BlockSpec((tm, D_MODEL), lambda i: (i, 0)),
                  pl.BlockSpec((1, D_MODEL), lambda i: (0, 0)),
                  pl.BlockSpec((D_MODEL, n), lambda i: (0, 0)),
                  pl.BlockSpec((1, MLA_KV_RANK), lambda i: (0, 0)),
                  pl.BlockSpec((1, LANES), lambda i: (0, 0)),
                  pl.BlockSpec((tm, LANES), lambda i: (i % n_cs, 0))],
        out_specs=[pl.BlockSpec((tm, MLA_KV_RANK), lambda i: (i, 0)),
                   pl.BlockSpec((tm, MLA_KV_RANK), lambda i: (i, 0)),
                   pl.BlockSpec((tm, LANES), lambda i: (i, 0))],
        out_shape=[jax.ShapeDtypeStruct((m, MLA_KV_RANK), F32),
                   jax.ShapeDtypeStruct((m, MLA_KV_RANK), BF16),
                   jax.ShapeDtypeStruct((m, LANES), F32)],
        compiler_params=_params(1), name="proj_c")(x, gx, w, gkv, gr, cs)


def _expand_kernel(c_ref, w_ref, gkn_ref, krr_ref, k_ref, v_ref, *, v_transposed):
    c = c_ref[...]
    krr = krr_ref[...].astype(BF16)
    for h in range(MLA_HEADS):
        a = _dot(c, w_ref[:, h * 2 * LANES:(h + 1) * 2 * LANES])
        k_ref[:, h * MLA_QK_PAD:h * MLA_QK_PAD + LANES] = _rms(a[:, :LANES], gkn_ref[...]).astype(BF16)
        k_ref[:, h * MLA_QK_PAD + LANES:(h + 1) * MLA_QK_PAD] = krr
        if v_transposed:
            v_ref[0, h * MLA_V:(h + 1) * MLA_V, :] = a[:, LANES:].T.astype(BF16)
        else:
            v_ref[:, h * MLA_V:(h + 1) * MLA_V] = a[:, LANES:].astype(BF16)


def _expand(c, w, gkn, krr, tm, v_transposed):
    m = c.shape[0]
    if v_transposed:
        v_spec = pl.BlockSpec((1, MLA_WIDTH, tm), lambda i: (i, 0, 0))
        v_shape = jax.ShapeDtypeStruct((m // tm, MLA_WIDTH, tm), BF16)
    else:
        v_spec = pl.BlockSpec((tm, MLA_WIDTH), lambda i: (i, 0))
        v_shape = jax.ShapeDtypeStruct((m, MLA_WIDTH), BF16)
    return pl.pallas_call(
        functools.partial(_expand_kernel, v_transposed=v_transposed),
        grid=(m // tm,),
        in_specs=[pl.BlockSpec((tm, MLA_KV_RANK), lambda i: (i, 0)),
                  pl.BlockSpec(w.shape, lambda i: (0, 0)),
                  pl.BlockSpec((1, LANES), lambda i: (0, 0)),
                  pl.BlockSpec((tm, LANES), lambda i: (i, 0))],
        out_specs=[pl.BlockSpec((tm, MLA_HEADS * MLA_QK_PAD), lambda i: (i, 0)), v_spec],
        out_shape=[jax.ShapeDtypeStruct((m, MLA_HEADS * MLA_QK_PAD), BF16), v_shape],
        compiler_params=_params(1), name="expand")(c, w, gkn, krr)


def _band_bias_kernel(b_ref, o_ref):
    t = pltpu.roll(jnp.broadcast_to(b_ref[0], (BAND_TQ, 2 * BAND_PAST)), 0, 1,
                   stride=1, stride_axis=0)[:, :BAND_WIN]
    qc = lax.broadcasted_iota(jnp.int32, (BAND_TQ, BAND_WIN), 0) // CHUNK
    kc = lax.broadcasted_iota(jnp.int32, (BAND_TQ, BAND_WIN), 1) // CHUNK
    o_ref[0] = jnp.where(kc >= qc, jnp.where(kc <= qc + BAND_PAST // CHUNK, t, NEG), NEG)


def _band_bias(rel_bias):
    far = jnp.broadcast_to(rel_bias[:, REL_SIZE - 1:], (BAND_HEADS, BAND_PAST - REL_MAX))
    near = jnp.broadcast_to(rel_bias[:, :1], (BAND_HEADS, BAND_WIN - BAND_PAST - CHUNK))
    wrap = jnp.broadcast_to(rel_bias[:, REL_SIZE - 1:], (BAND_HEADS, 2 * BAND_PAST - BAND_WIN))
    row0 = jnp.concatenate([far, rel_bias[:, ::-1], near, wrap], axis=1)
    assert row0.shape == (BAND_HEADS, 2 * BAND_PAST)
    return pl.pallas_call(
        _band_bias_kernel,
        grid=(BAND_HEADS,),
        in_specs=[pl.BlockSpec((1, 1, 2 * BAND_PAST), lambda h: (h, 0, 0))],
        out_specs=pl.BlockSpec((1, BAND_TQ, BAND_WIN), lambda h: (h, 0, 0)),
        out_shape=jax.ShapeDtypeStruct((BAND_HEADS, BAND_TQ, BAND_WIN), F32),
        compiler_params=_params(1), name="band_bias")(row0[:, None, :])


def _band_prompt_kernel(q_ref, ka_ref, kb_ref, kc_ref, va_ref, vb_ref, vc_ref, bias_ref, o_ref):
    i = pl.program_id(1)
    k_refs = (ka_ref, kb_ref, kc_ref)
    v_refs = (va_ref, vb_ref, vc_ref)
    n_piece = len(k_refs)
    for h in range(BAND_HEADS):
        hs = slice(h * BAND_HEAD_DIM, (h + 1) * BAND_HEAD_DIM)
        q = q_ref[0, :, hs]
        s = []
        for p in range(n_piece):
            sp = _dot_nt(q, k_refs[p][0, :, hs]) * BAND_SCALE + bias_ref[h, :, p * BAND_TQ:(p + 1) * BAND_TQ]
            if p < n_piece - 1:
                sp = jnp.where(i >= n_piece - 1 - p, sp, NEG)
            s.append(sp)
        m = functools.reduce(jnp.maximum, [jnp.max(sp, axis=-1, keepdims=True) for sp in s])
        e = [jnp.exp(sp - m) for sp in s]
        l = functools.reduce(jnp.add, [jnp.sum(ep, axis=-1, keepdims=True) for ep in e])
        o = functools.reduce(jnp.add, [_dot(e[p].astype(BF16), v_refs[p][0, :, hs]) for p in range(n_piece)])
        o_ref[0, :, hs] = (o / l).astype(BF16)


def _band_prompt(qkv, bias):
    b, s, _ = qkv.shape
    n_piece = BAND_WIN // BAND_TQ
    blk = (1, BAND_TQ, BAND_WIDTH)

    def kv_spec(col, back):
        return pl.BlockSpec(blk, lambda bi, i: (bi, jnp.maximum(i - back, 0), col))

    return pl.pallas_call(
        _band_prompt_kernel,
        grid=(b, s // BAND_TQ),
        in_specs=[pl.BlockSpec(blk, lambda bi, i: (bi, i, 0))]
                 + [kv_spec(1, n_piece - 1 - p) for p in range(n_piece)]
                 + [kv_spec(2, n_piece - 1 - p) for p in range(n_piece)]
                 + [pl.BlockSpec(bias.shape, lambda bi, i: (0, 0, 0))],
        out_specs=pl.BlockSpec(blk, lambda bi, i: (bi, i, 0)),
        out_shape=jax.ShapeDtypeStruct((b, s, BAND_WIDTH), BF16),
        compiler_params=_params(2), name="band_prompt")(qkv, qkv, qkv, qkv, qkv, qkv, qkv, bias)


def _band_sample_kernel(q_ref, kn_ref, vn_ref, kc_ref, vc_ref, bias_ref, o_ref):
    n_past = kc_ref.shape[1]
    t = q_ref.shape[1]
    for h in range(BAND_HEADS):
        hs = slice(h * BAND_HEAD_DIM, (h + 1) * BAND_HEAD_DIM)
        q = q_ref[0, :, hs]
        s1 = _dot_nt(q, kc_ref[0, :, hs]) * BAND_SCALE + bias_ref[h, :t, :n_past]
        s2 = _dot_nt(q, kn_ref[0, :, hs]) * BAND_SCALE + bias_ref[h, :t, n_past:n_past + t]
        m = jnp.maximum(jnp.max(s1, axis=-1, keepdims=True), jnp.max(s2, axis=-1, keepdims=True))
        e1 = jnp.exp(s1 - m)
        e2 = jnp.exp(s2 - m)
        l = jnp.sum(e1, axis=-1, keepdims=True) + jnp.sum(e2, axis=-1, keepdims=True)
        o = _dot(e1.astype(BF16), vc_ref[0, :, hs]) + _dot(e2.astype(BF16), vn_ref[0, :, hs])
        o_ref[0, :, hs] = (o / l).astype(BF16)


def _band_sample(qkv, k_cache, v_cache, bias):
    b, t, _ = qkv.shape
    n_past = k_cache.shape[1]
    new = (1, t, BAND_WIDTH)
    old = (1, n_past, BAND_WIDTH)
    return pl.pallas_call(
        _band_sample_kernel,
        grid=(b,),
        in_specs=[pl.BlockSpec(new, lambda bi: (bi, 0, 0)),
                  pl.BlockSpec(new, lambda bi: (bi, 0, 1)),
                  pl.BlockSpec(new, lambda bi: (bi, 0, 2)),
                  pl.BlockSpec(old, lambda bi: (bi, 0, 0)),
                  pl.BlockSpec(old, lambda bi: (bi, 0, 0)),
                  pl.BlockSpec(bias.shape, lambda bi: (0, 0, 0))],
        out_specs=pl.BlockSpec(new, lambda bi: (bi, 0, 0)),
        out_shape=jax.ShapeDtypeStruct((b, t, BAND_WIDTH), BF16),
        compiler_params=_params(1), name="band_sample")(qkv, qkv, qkv, k_cache, v_cache, bias)


def _mla_prompt_kernel(q_ref, k_ref, vt_ref, o_ref, sa_ref, sb_ref, *, tq):
    i = pl.program_id(2)
    q = q_ref[0]

    def scores(j):
        start = pl.multiple_of(j * tq, tq)
        return _dot_nt(k_ref[0, pl.ds(start, tq), :], q)

    def consume(s, j, carry, masked):
        m, l, acc = carry
        if masked:
            kc = lax.broadcasted_iota(jnp.int32, (tq, tq), 0) // CHUNK
            qc = lax.broadcasted_iota(jnp.int32, (tq, tq), 1) // CHUNK
            s = jnp.where(kc <= qc, s, NEG)
        m_new = jnp.maximum(m, jnp.max(s, axis=0, keepdims=True))
        alpha = jnp.exp2(m - m_new)
        p = jnp.exp2(s - m_new)
        l = alpha * l + jnp.sum(p, axis=0, keepdims=True)
        acc = alpha * acc + _dot(vt_ref[0, j], p.astype(BF16))
        return m_new, l, acc

    def finish(carry):
        _, l, acc = carry
        o_ref[0] = (acc / l).T.astype(BF16)

    def pair(pi, carry):
        j = 2 * pi
        sb_ref[...] = scores(j + 1)
        carry = consume(sa_ref[...], j, carry, False)
        sa_ref[...] = scores(j + 2)
        return consume(sb_ref[...], j + 1, carry, False)

    sa_ref[...] = scores(0)
    init = (jnp.full((1, tq), NEG, F32), jnp.zeros((1, tq), F32), jnp.zeros((MLA_V, tq), F32))
    carry = lax.fori_loop(0, i // 2, pair, init)

    @pl.when(i % 2 == 0)
    def _():
        finish(consume(sa_ref[...], i, carry, True))

    @pl.when(i % 2 == 1)
    def _():
        sb_ref[...] = scores(i)
        finish(consume(sb_ref[...], i, consume(sa_ref[...], i - 1, carry, False), True))


def _mla_prompt(q, k, vt, tq):
    b, s, _ = q.shape
    return pl.pallas_call(
        functools.partial(_mla_prompt_kernel, tq=tq),
        grid=(b, MLA_HEADS, s // tq),
        in_specs=[pl.BlockSpec((1, tq, MLA_QK_PAD), lambda bi, h, i: (bi, i, h)),
                  pl.BlockSpec((1, s, MLA_QK_PAD), lambda bi, h, i: (bi, 0, h)),
                  pl.BlockSpec((1, s // tq, MLA_V, tq), lambda bi, h, i: (bi, 0, h, 0))],
        out_specs=pl.BlockSpec((1, tq, MLA_V), lambda bi, h, i: (bi, i, h)),
        out_shape=jax.ShapeDtypeStruct((b, s, MLA_WIDTH), BF16),
        scratch_shapes=[pltpu.VMEM((tq, tq), F32), pltpu.VMEM((tq, tq), F32)],
        compiler_params=_params(3), name="mla_prompt")(q, k, vt)


def _mla_sample_kernel(q_ref, k_ref, v_ref, o_ref):
    s = _dot_nt(q_ref[0], k_ref[0])
    e = jnp.exp2(s - jnp.max(s, axis=-1, keepdims=True))
    o = _dot(e.astype(BF16), v_ref[0])
    o_ref[0] = (o / jnp.sum(e, axis=-1, keepdims=True)).astype(BF16)


def _mla_sample(q, k, v):
    b, t, _ = q.shape
    l = k.shape[1]
    return pl.pallas_call(
        _mla_sample_kernel,
        grid=(b, MLA_HEADS),
        in_specs=[pl.BlockSpec((1, t, MLA_QK_PAD), lambda bi, h: (bi, 0, h)),
                  pl.BlockSpec((1, l, MLA_QK_PAD), lambda bi, h: (bi, 0, h)),
                  pl.BlockSpec((1, l, MLA_V), lambda bi, h: (bi, 0, h))],
        out_specs=pl.BlockSpec((1, t, MLA_V), lambda bi, h: (bi, 0, h)),
        out_shape=jax.ShapeDtypeStruct((b, t, MLA_WIDTH), BF16),
        compiler_params=_params(2), name="mla_sample")(q, k, v)


def _mix_kernel(x_ref, gx_ref, oa_ref, ob_ref, wga_ref, wgb_ref, wpa_ref, wpb_ref, o_ref, xn_ref):
    @pl.when(pl.program_id(1) == 0)
    def _():
        xn_ref[...] = _rms(x_ref[...], gx_ref[...]).astype(BF16)

    xn = xn_ref[...]
    a = jax.nn.sigmoid(_dot(xn, wga_ref[...])) * _dot(oa_ref[...], wpa_ref[...])
    b = jax.nn.sigmoid(_dot(xn, wgb_ref[...])) * _dot(ob_ref[...], wpb_ref[...])
    o_ref[...] = (a + b).astype(BF16)


def _mix(x, gx, oa, ob, wga, wgb, wpa, wpb, tm, tn=256):
    m = x.shape[0]
    return pl.pallas_call(
        _mix_kernel,
        grid=(m // tm, D_MODEL // tn),
        in_specs=[pl.BlockSpec((tm, D_MODEL), lambda i, j: (i, 0)),
                  pl.BlockSpec((1, D_MODEL), lambda i, j: (0, 0)),
                  pl.BlockSpec((tm, BAND_WIDTH), lambda i, j: (i, 0)),
                  pl.BlockSpec((tm, MLA_WIDTH), lambda i, j: (i, 0)),
                  pl.BlockSpec((D_MODEL, tn), lambda i, j: (0, j)),
                  pl.BlockSpec((D_MODEL, tn), lambda i, j: (0, j)),
                  pl.BlockSpec((BAND_WIDTH, tn), lambda i, j: (0, j)),
                  pl.BlockSpec((MLA_WIDTH, tn), lambda i, j: (0, j))],
        out_specs=pl.BlockSpec((tm, tn), lambda i, j: (i, j)),
        out_shape=jax.ShapeDtypeStruct((m, D_MODEL), BF16),
        scratch_shapes=[pltpu.VMEM((tm, D_MODEL), BF16)],
        compiler_params=_params(2), name="mix")(x, gx, oa, ob, wga, wgb, wpa, wpb)


def _outproj_kernel(x_ref, mix_ref, w_ref, o_ref):
    o_ref[...] = x_ref[...] + _dot(mix_ref[...], w_ref[...])


def _outproj(x, mix, w, tm, tn=512):
    m = x.shape[0]
    return pl.pallas_call(
        _outproj_kernel,
        grid=(m // tm, D_MODEL // tn),
        in_specs=[pl.BlockSpec((tm, tn), lambda i, j: (i, j)),
                  pl.BlockSpec((tm, D_MODEL), lambda i, j: (i, 0)),
                  pl.BlockSpec((D_MODEL, tn), lambda i, j: (0, j))],
        out_specs=pl.BlockSpec((tm, tn), lambda i, j: (i, j)),
        out_shape=jax.ShapeDtypeStruct((m, D_MODEL), F32),
        compiler_params=_params(2), name="outproj")(x, mix, w)


def _ffn_kernel(h_ref, g_ref, wu_ref, wd_ref, o_ref, hn_ref):
    @pl.when(pl.program_id(1) == 0)
    def _():
        h = h_ref[...]
        hn_ref[...] = _rms(h, g_ref[...]).astype(BF16)
        o_ref[...] = h

    u = jnp.maximum(_dot(hn_ref[...], wu_ref[...]), 0.0)
    o_ref[...] += _dot((u * u).astype(BF16), wd_ref[...])


def _ffn(h, g, wu, wd, tm, tf=512):
    m = h.shape[0]
    return pl.pallas_call(
        _ffn_kernel,
        grid=(m // tm, D_FF // tf),
        in_specs=[pl.BlockSpec((tm, D_MODEL), lambda i, f: (i, 0)),
                  pl.BlockSpec((1, D_MODEL), lambda i, f: (0, 0)),
                  pl.BlockSpec((D_MODEL, tf), lambda i, f: (0, f)),
                  pl.BlockSpec((tf, D_MODEL), lambda i, f: (f, 0))],
        out_specs=pl.BlockSpec((tm, D_MODEL), lambda i, f: (i, 0)),
        out_shape=jax.ShapeDtypeStruct((m, D_MODEL), F32),
        scratch_shapes=[pltpu.VMEM((tm, D_MODEL), BF16)],
        compiler_params=_params(2), name="ffn")(h, g, wu, wd)


def _rope_table(pos, rows):
    half = MLA_ROPE // 2
    freqs = ROPE_THETA ** (-(jnp.arange(half, dtype=F32) / half))
    ang = pos.astype(F32)[:, None] * freqs[None, :]
    cos = jnp.cos(ang)
    sin = jnp.sin(ang)
    table = jnp.concatenate([cos, cos, -sin, sin], axis=1)
    return jnp.tile(table, (rows // table.shape[0], 1))


def _with_partner(w):
    half = MLA_ROPE // 2
    return jnp.concatenate([w, w[..., half:], w[..., :half]], axis=-1)


def _layer_weights(l, norm_mix_g, w_in, g_aq, g_ak, g_kv, g_kr, g_qn, g_qr, g_kn,
                   w_kv_b, w_pa, w_pb, w_out, norm_ffn_g, w_up, w_down):
    w = w_in[l]
    c = 0
    parts = []
    for width in (3 * BAND_WIDTH, MLA_HEADS * MLA_QK, MLA_KV_RANK, MLA_ROPE, D_MODEL, D_MODEL):
        parts.append(w[:, c:c + width])
        c += width
    w_a, w_bq, w_ckv, w_kr, w_ga, w_gb = parts
    w_bq = w_bq.reshape(D_MODEL, MLA_HEADS, MLA_QK)
    w_q = jnp.concatenate([w_bq[..., :MLA_NOPE], _with_partner(w_bq[..., MLA_NOPE:])], axis=-1)
    row = lambda g: g[l][None, :].astype(F32)
    return dict(
        gx=row(norm_mix_g),
        w_a=w_a.astype(BF16),
        g_a=jnp.concatenate([jnp.tile(g_aq[l], BAND_HEADS), jnp.tile(g_ak[l], BAND_HEADS),
                             jnp.ones((BAND_WIDTH,), F32)])[None, :],
        w_q=w_q.reshape(D_MODEL, MLA_HEADS * MLA_QK_PAD).astype(BF16),
        g_qn=row(g_qn),
        g_qr=_with_partner(g_qr[l])[None, :],
        w_c=jnp.concatenate([w_ckv, _with_partner(w_kr)], axis=1).astype(BF16),
        g_kv=row(g_kv),
        g_kr=_with_partner(g_kr[l])[None, :],
        w_kv_b=w_kv_b[l].astype(BF16),
        g_kn=row(g_kn),
        w_ga=w_ga.astype(BF16), w_gb=w_gb.astype(BF16),
        w_pa=w_pa[l].astype(BF16), w_pb=w_pb[l].astype(BF16),
        w_out=w_out[l].astype(BF16),
        g_ffn=row(norm_ffn_g),
        w_up=w_up[l].astype(BF16), w_down=w_down[l].astype(BF16),
    )


def _project(x2, p, cs, tm):
    qkv_bf, qkv_f = _proj_a(x2, p["gx"], p["w_a"], p["g_a"], tm)
    q_mla = _proj_q(x2, p["gx"], p["w_q"], p["g_qn"], p["g_qr"], cs, tm)
    ckv, ckv_bf, krr = _proj_c(x2, p["gx"], p["w_c"], p["g_kv"], p["g_kr"], cs, tm)
    return qkv_bf, qkv_f, q_mla, ckv, ckv_bf, krr


def _merge_ffn(x2, oa, ob, p, tm):
    mix = _mix(x2, p["gx"], oa, ob, p["w_ga"], p["w_gb"], p["w_pa"], p["w_pb"], tm)
    h = _outproj(x2, mix, p["w_out"], tm)
    return _ffn(h, p["g_ffn"], p["w_up"], p["w_down"], tm)


def kernel(x_prompt, x_sample, cache_a_k, cache_a_v, cache_mla_ckv, cache_mla_krope, norm_mix_g, w_in, g_aq, g_ak, rel_bias, g_kv, g_kr, g_qn, g_qr, g_kn, w_kv_b, w_pa, w_pb, w_out, norm_ffn_g, w_up, w_down):
    b, s, _ = x_prompt.shape
    bs, t, _ = x_sample.shape
    past = cache_mla_ckv.shape[2]
    depth = w_in.shape[0]
    keep = min(BAND_PAST, s)
    tm_p = 1024
    tq_mla = 512
    tm_s = bs * t
    assert s % tm_p == 0 and s % BAND_TQ == 0 and s % tq_mla == 0 and t == CHUNK and past % CHUNK == 0
    cs_p = _rope_table(jnp.arange(s), max(s, tm_p))
    cs_s = _rope_table(past + jnp.arange(t), tm_s)

    yp = x_prompt.reshape(b * s, D_MODEL)
    ys = x_sample.reshape(bs * t, D_MODEL)
    outs = [[] for _ in range(8)]
    for l in range(depth):
        p = _layer_weights(l, norm_mix_g, w_in, g_aq, g_ak, g_kv, g_kr, g_qn, g_qr, g_kn,
                           w_kv_b, w_pa, w_pb, w_out, norm_ffn_g, w_up, w_down)
        bias = _band_bias(rel_bias[l])

        qkv_bf, qkv_f, q_mla, ckv, ckv_bf, krr = _project(yp, p, cs_p, tm_p)
        oa = _band_prompt(qkv_bf.reshape(b, s, 3 * BAND_WIDTH), bias)
        k_mla, vt_mla = _expand(ckv_bf, p["w_kv_b"], p["g_kn"], krr, tq_mla, True)
        ob = _mla_prompt(q_mla.reshape(b, s, -1), k_mla.reshape(b, s, -1),
                         vt_mla.reshape(b, s // tq_mla, MLA_WIDTH, tq_mla), tq_mla)
        kv_f = qkv_f.reshape(b, s, 3, BAND_HEADS, BAND_HEAD_DIM)
        outs[0].append(kv_f[:, s - keep:, 1])
        outs[1].append(kv_f[:, s - keep:, 2])
        outs[2].append(ckv.reshape(b, s, MLA_KV_RANK))
        outs[3].append(krr[:, :MLA_ROPE].reshape(b, s, MLA_ROPE))
        yp = _merge_ffn(yp, oa.reshape(b * s, BAND_WIDTH), ob.reshape(b * s, MLA_WIDTH), p, tm_p)

        qkv_bf, qkv_f, q_mla, ckv, ckv_bf, krr = _project(ys, p, cs_s, tm_s)
        n_band = cache_a_k.shape[2]
        oa = _band_sample(qkv_bf.reshape(bs, t, 3 * BAND_WIDTH),
                          cache_a_k[l].reshape(bs, n_band, BAND_WIDTH).astype(BF16),
                          cache_a_v[l].reshape(bs, n_band, BAND_WIDTH).astype(BF16), bias)
        ckv_all = jnp.concatenate([cache_mla_ckv[l].astype(BF16), ckv_bf.reshape(bs, t, MLA_KV_RANK)], axis=1)
        kr_old = cache_mla_krope[l].astype(F32)
        krr_all = jnp.concatenate([jnp.concatenate([kr_old, kr_old], axis=-1), krr.reshape(bs, t, LANES)], axis=1)
        n_all = past + t
        k_mla, v_mla = _expand(ckv_all.reshape(bs * n_all, MLA_KV_RANK), p["w_kv_b"], p["g_kn"],
                               krr_all.reshape(bs * n_all, LANES), n_all // 4, False)
        ob = _mla_sample(q_mla.reshape(bs, t, -1), k_mla.reshape(bs, n_all, -1), v_mla.reshape(bs, n_all, -1))
        kv_f = qkv_f.reshape(bs, t, 3, BAND_HEADS, BAND_HEAD_DIM)
        outs[4].append(kv_f[:, :, 1])
        outs[5].append(kv_f[:, :, 2])
        outs[6].append(ckv.reshape(bs, t, MLA_KV_RANK))
        outs[7].append(krr[:, :MLA_ROPE].reshape(bs, t, MLA_ROPE))
        ys = _merge_ffn(ys, oa.reshape(bs * t, BAND_WIDTH), ob.reshape(bs * t, MLA_WIDTH), p, tm_s)

    return (yp.reshape(b, s, D_MODEL), ys.reshape(bs, t, D_MODEL)) + tuple(jnp.stack(o) for o in outs)
```

```python
import functools

import jax
import jax.numpy as jnp
from jax import lax
from jax.experimental import pallas as pl
from jax.experimental.pallas import tpu as pltpu

F32 = jnp.float32
BF16 = jnp.bfloat16

D_MODEL = 2048
CHUNK = 64
BAND_PAST = 8 * CHUNK
BAND_HEADS = 8
BAND_HEAD_DIM = 128
BAND_WIDTH = BAND_HEADS * BAND_HEAD_DIM
REL_MAX = 256
REL_SIZE = (CHUNK - 1) + REL_MAX + 1
BAND_SCALE = BAND_HEAD_DIM ** -0.5
MLA_HEADS = 8
MLA_NOPE = 128
MLA_ROPE = 64
MLA_QK = MLA_NOPE + MLA_ROPE
MLA_V = 128
MLA_WIDTH = MLA_HEADS * MLA_V
MLA_KV_RANK = 512
MLA_SCALE = MLA_QK ** -0.5
ROPE_THETA = 10000.0
D_FF = 4 * D_MODEL
EPS = 1e-6
NEG = -1e30
LOG2E = 1.4426950408889634
MLA_QSCALE = MLA_SCALE * LOG2E
BAND_QSCALE = BAND_SCALE * LOG2E

LANES = 128
MLA_QK_PAD = 2 * LANES
BAND_TQ = 256
BAND_WIN = BAND_TQ + BAND_PAST
BAND_PIECES = BAND_WIN // BAND_TQ
VMEM_LIMIT = 56 * 1024 * 1024
PROJ_TN = 512


def _params(n_axes, vmem=VMEM_LIMIT):
    return pltpu.CompilerParams(dimension_semantics=("arbitrary",) * n_axes,
                                vmem_limit_bytes=vmem)


def _rms(a, g):
    return a * lax.rsqrt(jnp.mean(a * a, axis=-1, keepdims=True) + EPS) * g


def _dot(a, b):
    return jnp.dot(a, b, preferred_element_type=F32)


def _dot_nt(a, b):
    return lax.dot_general(a, b, (((1,), (1,)), ((), ())), preferred_element_type=F32)


def _rope_pair(ar, g, cs):
    lane = lax.broadcasted_iota(jnp.int32, (1, LANES), 1)
    first = (lane < MLA_ROPE).astype(F32)
    ss = jnp.sum(ar * ar * first, axis=-1, keepdims=True) / MLA_ROPE
    return ar * lax.rsqrt(ss + EPS) * g * cs


_NB_BAND_NORM = 2 * BAND_WIDTH // PROJ_TN
_NB_BAND = 3 * BAND_WIDTH // PROJ_TN
_NB_MLA_Q = MLA_HEADS * MLA_QK_PAD // PROJ_TN
_NB_PROJ = _NB_BAND + _NB_MLA_Q + 1


def _proj_kernel(x_ref, gx_ref, w_ref, wkr_ref, gh_ref, gqn_ref, gqr_ref, gkv_ref, gkr_ref, cs_ref,
                 xn_out_ref, qkv_ref, q_ref, ckv_ref, ckvb_ref, krr_ref, kr_ref, xn_ref):
    j = pl.program_id(1)

    @pl.when(j == 0)
    def _():
        xn = _rms(x_ref[...], gx_ref[...]).astype(BF16)
        xn_ref[...] = xn
        xn_out_ref[...] = xn

    acc = _dot(xn_ref[...], w_ref[...])

    @pl.when(j < _NB_BAND_NORM)
    def _():
        for k in range(PROJ_TN // LANES):
            sl = slice(k * LANES, (k + 1) * LANES)
            qkv_ref[:, sl] = _rms(acc[:, sl], gh_ref[:, sl]).astype(BF16)

    @pl.when(jnp.logical_and(j >= _NB_BAND_NORM, j < _NB_BAND))
    def _():
        qkv_ref[...] = acc.astype(BF16)

    @pl.when(jnp.logical_and(j >= _NB_BAND, j < _NB_BAND + _NB_MLA_Q))
    def _():
        cs = cs_ref[...]
        for k in range(PROJ_TN // MLA_QK_PAD):
            c0 = k * MLA_QK_PAD
            qn = _rms(acc[:, c0:c0 + LANES], gqn_ref[...]) * MLA_QSCALE
            qr = _rope_pair(acc[:, c0 + LANES:c0 + 2 * LANES], gqr_ref[...], cs) * MLA_QSCALE
            q_ref[:, c0:c0 + LANES] = qn.astype(BF16)
            q_ref[:, c0 + LANES:c0 + 2 * LANES] = qr.astype(BF16)

    @pl.when(j == _NB_BAND + _NB_MLA_Q)
    def _():
        cn = _rms(acc, gkv_ref[...])
        ckv_ref[...] = cn
        ckvb_ref[...] = cn.astype(BF16)
        t = _rope_pair(_dot(xn_ref[...], wkr_ref[...]), gkr_ref[...], cs_ref[...])
        krr = t + pltpu.roll(t, MLA_ROPE, 1)
        krr_ref[...] = krr
        kr_ref[...] = krr[:, :MLA_ROPE]


def _proj(x, p, cs, tm):
    m = x.shape[0]
    tn = PROJ_TN
    n_cs = cs.shape[0] // tm
    const = lambda i, j: (0, 0)
    rows = lambda i, j: (i, 0)
    return pl.pallas_call(
        _proj_kernel,
        grid=(m // tm, _NB_PROJ),
        in_specs=[pl.BlockSpec((tm, D_MODEL), rows),
                  pl.BlockSpec((1, D_MODEL), const),
                  pl.BlockSpec((D_MODEL, tn), lambda i, j: (0, j)),
                  pl.BlockSpec((D_MODEL, LANES), const),
                  pl.BlockSpec((1, tn), lambda i, j: (0, jnp.minimum(j, _NB_BAND - 1))),
                  pl.BlockSpec((1, LANES), const),
                  pl.BlockSpec((1, LANES), const),
                  pl.BlockSpec((1, MLA_KV_RANK), const),
                  pl.BlockSpec((1, LANES), const),
                  pl.BlockSpec((tm, LANES), lambda i, j: (i % n_cs, 0))],
        out_specs=[pl.BlockSpec((tm, D_MODEL), rows),
                   pl.BlockSpec((tm, tn), lambda i, j: (i, jnp.minimum(j, _NB_BAND - 1))),
                   pl.BlockSpec((tm, tn), lambda i, j: (i, jnp.clip(j - _NB_BAND, 0, _NB_MLA_Q - 1))),
                   pl.BlockSpec((tm, MLA_KV_RANK), rows),
                   pl.BlockSpec((tm, MLA_KV_RANK), rows),
                   pl.BlockSpec((tm, LANES), rows),
                   pl.BlockSpec((tm, MLA_ROPE), rows)],
        out_shape=[jax.ShapeDtypeStruct((m, D_MODEL), BF16),
                   jax.ShapeDtypeStruct((m, 3 * BAND_WIDTH), BF16),
                   jax.ShapeDtypeStruct((m, MLA_HEADS * MLA_QK_PAD), BF16),
                   jax.ShapeDtypeStruct((m, MLA_KV_RANK), F32),
                   jax.ShapeDtypeStruct((m, MLA_KV_RANK), BF16),
                   jax.ShapeDtypeStruct((m, LANES), F32),
                   jax.ShapeDtypeStruct((m, MLA_ROPE), F32)],
        scratch_shapes=[pltpu.VMEM((tm, D_MODEL), BF16)],
        compiler_params=_params(2), name="proj")(
            x, p["gx"], p["w_proj"], p["w_kr"], p["g_a"], p["g_qn"], p["g_qr"], p["g_kv"], p["g_kr"], cs)


def _kv_f32_kernel(x_ref, gx_ref, w_ref, g_ref, o_ref, xn_ref, *, n_norm):
    j = pl.program_id(1)

    @pl.when(j == 0)
    def _():
        xn_ref[...] = _rms(x_ref[...], gx_ref[...]).astype(BF16)

    acc = _dot(xn_ref[...], w_ref[...])

    @pl.when(j < n_norm)
    def _():
        for k in range(acc.shape[1] // LANES):
            sl = slice(k * LANES, (k + 1) * LANES)
            o_ref[0, :, sl] = _rms(acc[:, sl], g_ref[...])

    @pl.when(j >= n_norm)
    def _():
        o_ref[0] = acc


def _kv_f32(x, gx, w, g, tm, tn=512):
    m = x.shape[0]
    nb = BAND_WIDTH // tn
    return pl.pallas_call(
        functools.partial(_kv_f32_kernel, n_norm=nb),
        grid=(m // tm, 2 * nb),
        in_specs=[pl.BlockSpec((tm, D_MODEL), lambda i, j: (i, 0)),
                  pl.BlockSpec((1, D_MODEL), lambda i, j: (0, 0)),
                  pl.BlockSpec((D_MODEL, tn), lambda i, j: (0, j)),
                  pl.BlockSpec((1, LANES), lambda i, j: (0, 0))],
        out_specs=pl.BlockSpec((1, tm, tn), lambda i, j: (j // nb, i, j % nb)),
        out_shape=jax.ShapeDtypeStruct((2, m, BAND_WIDTH), F32),
        scratch_shapes=[pltpu.VMEM((tm, D_MODEL), BF16)],
        compiler_params=_params(2), name="kv_f32")(x, gx, w, g)


def _expand_kernel(c_ref, w_ref, gkn_ref, krr_ref, k_ref, v_ref, *, v_transposed):
    c = c_ref[...]
    krr = krr_ref[...].astype(BF16)
    for h in range(MLA_HEADS):
        a = _dot(c, w_ref[:, h * 2 * LANES:(h + 1) * 2 * LANES])
        k_ref[:, h * MLA_QK_PAD:h * MLA_QK_PAD + LANES] = _rms(a[:, :LANES], gkn_ref[...]).astype(BF16)
        k_ref[:, h * MLA_QK_PAD + LANES:(h + 1) * MLA_QK_PAD] = krr
        if v_transposed:
            v_ref[0, h * MLA_V:(h + 1) * MLA_V, :] = a[:, LANES:].T.astype(BF16)
        else:
            v_ref[:, h * MLA_V:(h + 1) * MLA_V] = a[:, LANES:].astype(BF16)


def _expand(c, w, gkn, krr, tm, v_transposed):
    m = c.shape[0]
    if v_transposed:
        v_spec = pl.BlockSpec((1, MLA_WIDTH, tm), lambda i: (i, 0, 0))
        v_shape = jax.ShapeDtypeStruct((m // tm, MLA_WIDTH, tm), BF16)
    else:
        v_spec = pl.BlockSpec((tm, MLA_WIDTH), lambda i: (i, 0))
        v_shape = jax.ShapeDtypeStruct((m, MLA_WIDTH), BF16)
    return pl.pallas_call(
        functools.partial(_expand_kernel, v_transposed=v_transposed),
        grid=(m // tm,),
        in_specs=[pl.BlockSpec((tm, MLA_KV_RANK), lambda i: (i, 0)),
                  pl.BlockSpec(w.shape, lambda i: (0, 0)),
                  pl.BlockSpec((1, LANES), lambda i: (0, 0)),
                  pl.BlockSpec((tm, LANES), lambda i: (i, 0))],
        out_specs=[pl.BlockSpec((tm, MLA_HEADS * MLA_QK_PAD), lambda i: (i, 0)), v_spec],
        out_shape=[jax.ShapeDtypeStruct((m, MLA_HEADS * MLA_QK_PAD), BF16), v_shape],
        compiler_params=_params(1), name="expand")(c, w, gkn, krr)


def _band_bias_kernel(b_ref, o_ref):
    n_missing = BAND_PIECES - 1 - pl.program_id(0)
    t = pltpu.roll(jnp.broadcast_to(b_ref[0], (BAND_TQ, 2 * BAND_PAST)), 0, 1,
                   stride=1, stride_axis=0)[:, :BAND_WIN] * LOG2E
    col = lax.broadcasted_iota(jnp.int32, (BAND_TQ, BAND_WIN), 1)
    qc = lax.broadcasted_iota(jnp.int32, (BAND_TQ, BAND_WIN), 0) // CHUNK
    kc = col // CHUNK
    t = jnp.where(kc >= qc, jnp.where(kc <= qc + BAND_PAST // CHUNK, t, NEG), NEG)
    o_ref[0, 0] = jnp.where(col >= n_missing * BAND_TQ, t, NEG)


def _band_bias(rel_bias):
    far = jnp.broadcast_to(rel_bias[:, REL_SIZE - 1:], (BAND_HEADS, BAND_PAST - REL_MAX))
    near = jnp.broadcast_to(rel_bias[:, :1], (BAND_HEADS, BAND_WIN - BAND_PAST - CHUNK))
    wrap = jnp.broadcast_to(rel_bias[:, REL_SIZE - 1:], (BAND_HEADS, 2 * BAND_PAST - BAND_WIN))
    row0 = jnp.concatenate([far, rel_bias[:, ::-1], near, wrap], axis=1)
    assert row0.shape == (BAND_HEADS, 2 * BAND_PAST)
    return pl.pallas_call(
        _band_bias_kernel,
        grid=(BAND_PIECES, BAND_HEADS),
        in_specs=[pl.BlockSpec((1, 1, 2 * BAND_PAST), lambda v, h: (h, 0, 0))],
        out_specs=pl.BlockSpec((1, 1, BAND_TQ, BAND_WIN), lambda v, h: (v, h, 0, 0)),
        out_shape=jax.ShapeDtypeStruct((BAND_PIECES, BAND_HEADS, BAND_TQ, BAND_WIN), F32),
        compiler_params=_params(2), name="band_bias")(row0[:, None, :])


def _softmax_pv(s, v):
    m = functools.reduce(jnp.maximum, [jnp.max(sp, axis=-1, keepdims=True) for sp in s])
    e = [jnp.exp2(sp - m) for sp in s]
    l = functools.reduce(jnp.add, [jnp.sum(ep, axis=-1, keepdims=True) for ep in e])
    o = functools.reduce(jnp.add, [_dot(ep.astype(BF16), vp) for ep, vp in zip(e, v)])
    return (o / l).astype(BF16)


def _band_prompt_kernel(q_ref, ka_ref, kb_ref, kc_ref, va_ref, vb_ref, vc_ref, bias_ref, o_ref):
    k_refs = (ka_ref, kb_ref, kc_ref)
    v_refs = (va_ref, vb_ref, vc_ref)
    for h in range(BAND_HEADS):
        hs = slice(h * BAND_HEAD_DIM, (h + 1) * BAND_HEAD_DIM)
        q = q_ref[0, :, hs]
        s = [_dot_nt(q, k_refs[p][0, :, hs]) + bias_ref[0, h, :, p * BAND_TQ:(p + 1) * BAND_TQ]
             for p in range(BAND_PIECES)]
        o_ref[0, :, hs] = _softmax_pv(s, [v_refs[p][0, :, hs] for p in range(BAND_PIECES)])


def _band_prompt(qkv, bias):
    b, s, _ = qkv.shape
    blk = (1, BAND_TQ, BAND_WIDTH)

    def kv_spec(col, back):
        return pl.BlockSpec(blk, lambda bi, i: (bi, jnp.maximum(i - back, 0), col))

    return pl.pallas_call(
        _band_prompt_kernel,
        grid=(b, s // BAND_TQ),
        in_specs=[pl.BlockSpec(blk, lambda bi, i: (bi, i, 0))]
                 + [kv_spec(1, BAND_PIECES - 1 - p) for p in range(BAND_PIECES)]
                 + [kv_spec(2, BAND_PIECES - 1 - p) for p in range(BAND_PIECES)]
                 + [pl.BlockSpec((1,) + bias.shape[1:],
                                 lambda bi, i: (jnp.minimum(i, BAND_PIECES - 1), 0, 0, 0))],
        out_specs=pl.BlockSpec(blk, lambda bi, i: (bi, i, 0)),
        out_shape=jax.ShapeDtypeStruct((b, s, BAND_WIDTH), BF16),
        compiler_params=_params(2), name="band_prompt")(qkv, qkv, qkv, qkv, qkv, qkv, qkv, bias)


def _band_sample_kernel(q_ref, kn_ref, vn_ref, kc_ref, vc_ref, bias_ref, o_ref, *, n_past):
    t = q_ref.shape[1]
    for h in range(BAND_HEADS):
        hs = slice(h * BAND_HEAD_DIM, (h + 1) * BAND_HEAD_DIM)
        q = q_ref[0, :, hs]
        k_old = kc_ref[0, pl.ds(h, n_past, stride=BAND_HEADS), :].astype(BF16)
        v_old = vc_ref[0, pl.ds(h, n_past, stride=BAND_HEADS), :].astype(BF16)
        s = [_dot_nt(q, k_old) + bias_ref[0, h, :t, :n_past],
             _dot_nt(q, kn_ref[0, :, hs]) + bias_ref[0, h, :t, n_past:n_past + t]]
        o_ref[0, :, hs] = _softmax_pv(s, [v_old, vn_ref[0, :, hs]])


def _band_sample(qkv, k_cache, v_cache, bias):
    b, t, _ = qkv.shape
    n_past = k_cache.shape[1] // BAND_HEADS
    new = (1, t, BAND_WIDTH)
    old = (1, n_past * BAND_HEADS, BAND_HEAD_DIM)
    return pl.pallas_call(
        functools.partial(_band_sample_kernel, n_past=n_past),
        grid=(b,),
        in_specs=[pl.BlockSpec(new, lambda bi: (bi, 0, 0)),
                  pl.BlockSpec(new, lambda bi: (bi, 0, 1)),
                  pl.BlockSpec(new, lambda bi: (bi, 0, 2)),
                  pl.BlockSpec(old, lambda bi: (bi, 0, 0)),
                  pl.BlockSpec(old, lambda bi: (bi, 0, 0)),
                  pl.BlockSpec((1,) + bias.shape[1:], lambda bi: (BAND_PIECES - 1, 0, 0, 0))],
        out_specs=pl.BlockSpec(new, lambda bi: (bi, 0, 0)),
        out_shape=jax.ShapeDtypeStruct((b, t, BAND_WIDTH), BF16),
        compiler_params=_params(1), name="band_sample")(qkv, qkv, qkv, k_cache, v_cache, bias)


def _mla_prompt_kernel(q_ref, k_ref, vt_ref, o_ref, sa_ref, sb_ref, *, tq):
    i = pl.program_id(2)
    q = q_ref[0]

    def scores(j):
        start = pl.multiple_of(j * tq, tq)
        return _dot_nt(k_ref[0, pl.ds(start, tq), :], q)

    def consume(s, j, carry, masked):
        m, l, acc = carry
        if masked:
            kc = lax.broadcasted_iota(jnp.int32, (tq, tq), 0) // CHUNK
            qc = lax.broadcasted_iota(jnp.int32, (tq, tq), 1) // CHUNK
            s = jnp.where(kc <= qc, s, NEG)
        m_new = jnp.maximum(m, jnp.max(s, axis=0, keepdims=True))
        alpha = jnp.exp2(m - m_new)
        p = jnp.exp2(s - m_new)
        l = alpha * l + jnp.sum(p, axis=0, keepdims=True)
        acc = alpha * acc + _dot(vt_ref[0, j], p.astype(BF16))
        return m_new, l, acc

    def finish(carry):
        _, l, acc = carry
        o_ref[0] = (acc / l).T.astype(BF16)

    def pair(pi, carry):
        j = 2 * pi
        sb_ref[...] = scores(j + 1)
        carry = consume(sa_ref[...], j, carry, False)
        sa_ref[...] = scores(j + 2)
        return consume(sb_ref[...], j + 1, carry, False)

    sa_ref[...] = scores(0)
    init = (jnp.full((1, tq), NEG, F32), jnp.zeros((1, tq), F32), jnp.zeros((MLA_V, tq), F32))
    carry = lax.fori_loop(0, i // 2, pair, init)

    @pl.when(i % 2 == 0)
    def _():
        finish(consume(sa_ref[...], i, carry, True))

    @pl.when(i % 2 == 1)
    def _():
        sb_ref[...] = scores(i)
        finish(consume(sb_ref[...], i, consume(sa_ref[...], i - 1, carry, False), True))


def _mla_prompt(q, k, vt, tq):
    b, s, _ = q.shape
    return pl.pallas_call(
        functools.partial(_mla_prompt_kernel, tq=tq),
        grid=(b, MLA_HEADS, s // tq),
        in_specs=[pl.BlockSpec((1, tq, MLA_QK_PAD), lambda bi, h, i: (bi, i, h)),
                  pl.BlockSpec((1, s, MLA_QK_PAD), lambda bi, h, i: (bi, 0, h)),
                  pl.BlockSpec((1, s // tq, MLA_V, tq), lambda bi, h, i: (bi, 0, h, 0))],
        out_specs=pl.BlockSpec((1, tq, MLA_V), lambda bi, h, i: (bi, i, h)),
        out_shape=jax.ShapeDtypeStruct((b, s, MLA_WIDTH), BF16),
        scratch_shapes=[pltpu.VMEM((tq, tq), F32), pltpu.VMEM((tq, tq), F32)],
        compiler_params=_params(3), name="mla_prompt")(q, k, vt)


def _mla_sample_kernel(q_ref, k_ref, v_ref, o_ref):
    s = _dot_nt(q_ref[0], k_ref[0])
    e = jnp.exp2(s - jnp.max(s, axis=-1, keepdims=True))
    o = _dot(e.astype(BF16), v_ref[0])
    o_ref[0] = (o / jnp.sum(e, axis=-1, keepdims=True)).astype(BF16)


def _mla_sample(q, k, v):
    b, t, _ = q.shape
    l = k.shape[1]
    return pl.pallas_call(
        _mla_sample_kernel,
        grid=(b, MLA_HEADS),
        in_specs=[pl.BlockSpec((1, t, MLA_QK_PAD), lambda bi, h: (bi, 0, h)),
                  pl.BlockSpec((1, l, MLA_QK_PAD), lambda bi, h: (bi, 0, h)),
                  pl.BlockSpec((1, l, MLA_V), lambda bi, h: (bi, 0, h))],
        out_specs=pl.BlockSpec((1, t, MLA_V), lambda bi, h: (bi, 0, h)),
        out_shape=jax.ShapeDtypeStruct((b, t, MLA_WIDTH), BF16),
        compiler_params=_params(2), name="mla_sample")(q, k, v)


def _mix_kernel(xn_ref, oa_ref, ob_ref, wga_ref, wgb_ref, wpa_ref, wpb_ref, o_ref):
    xn = xn_ref[...]
    a = jax.nn.sigmoid(_dot(xn, wga_ref[...])) * _dot(oa_ref[...], wpa_ref[...])
    b = jax.nn.sigmoid(_dot(xn, wgb_ref[...])) * _dot(ob_ref[...], wpb_ref[...])
    o_ref[...] = (a + b).astype(BF16)


def _mix(xn, oa, ob, wga, wgb, wpa, wpb, tm, tn=512):
    m = xn.shape[0]
    return pl.pallas_call(
        _mix_kernel,
        grid=(m // tm, D_MODEL // tn),
        in_specs=[pl.BlockSpec((tm, D_MODEL), lambda i, j: (i, 0)),
                  pl.BlockSpec((tm, BAND_WIDTH), lambda i, j: (i, 0)),
                  pl.BlockSpec((tm, MLA_WIDTH), lambda i, j: (i, 0)),
                  pl.BlockSpec((D_MODEL, tn), lambda i, j: (0, j)),
                  pl.BlockSpec((D_MODEL, tn), lambda i, j: (0, j)),
                  pl.BlockSpec((BAND_WIDTH, tn), lambda i, j: (0, j)),
                  pl.BlockSpec((MLA_WIDTH, tn), lambda i, j: (0, j))],
        out_specs=pl.BlockSpec((tm, tn), lambda i, j: (i, j)),
        out_shape=jax.ShapeDtypeStruct((m, D_MODEL), BF16),
        compiler_params=_params(2), name="mix")(xn, oa, ob, wga, wgb, wpa, wpb)


def _outproj_kernel(x_ref, mix_ref, w_ref, o_ref):
    o_ref[...] = x_ref[...] + _dot(mix_ref[...], w_ref[...])


def _outproj(x, mix, w, tm, tn=512):
    m = x.shape[0]
    return pl.pallas_call(
        _outproj_kernel,
        grid=(m // tm, D_MODEL // tn),
        in_specs=[pl.BlockSpec((tm, tn), lambda i, j: (i, j)),
                  pl.BlockSpec((tm, D_MODEL), lambda i, j: (i, 0)),
                  pl.BlockSpec((D_MODEL, tn), lambda i, j: (0, j))],
        out_specs=pl.BlockSpec((tm, tn), lambda i, j: (i, j)),
        out_shape=jax.ShapeDtypeStruct((m, D_MODEL), F32),
        compiler_params=_params(2), name="outproj")(x, mix, w)


def _ffn_kernel(h_ref, g_ref, wu_ref, wd_ref, o_ref, hn_ref):
    @pl.when(pl.program_id(1) == 0)
    def _():
        h = h_ref[...]
        hn_ref[...] = _rms(h, g_ref[...]).astype(BF16)
        o_ref[...] = h

    u = jnp.maximum(_dot(hn_ref[...], wu_ref[...]), 0.0)
    o_ref[...] += _dot((u * u).astype(BF16), wd_ref[...])


def _ffn(h, g, wu, wd, tm, tf=512):
    m = h.shape[0]
    return pl.pallas_call(
        _ffn_kernel,
        grid=(m // tm, D_FF // tf),
        in_specs=[pl.BlockSpec((tm, D_MODEL), lambda i, f: (i, 0)),
                  pl.BlockSpec((1, D_MODEL), lambda i, f: (0, 0)),
                  pl.BlockSpec((D_MODEL, tf), lambda i, f: (0, f)),
                  pl.BlockSpec((tf, D_MODEL), lambda i, f: (f, 0))],
        out_specs=pl.BlockSpec((tm, D_MODEL), lambda i, f: (i, 0)),
        out_shape=jax.ShapeDtypeStruct((m, D_MODEL), F32),
        scratch_shapes=[pltpu.VMEM((tm, D_MODEL), BF16)],
        compiler_params=_params(2), name="ffn")(h, g, wu, wd)


def _rope_table(pos, rows):
    half = MLA_ROPE // 2
    freqs = ROPE_THETA ** (-(jnp.arange(half, dtype=F32) / half))
    ang = pos.astype(F32)[:, None] * freqs[None, :]
    cos = jnp.cos(ang)
    sin = jnp.sin(ang)
    table = jnp.concatenate([cos, cos, -sin, sin], axis=1)
    return jnp.tile(table, (rows // table.shape[0], 1))


def _with_partner(w):
    half = MLA_ROPE // 2
    return jnp.concatenate([w, w[..., half:], w[..., :half]], axis=-1)


def _layer_weights(l, norm_mix_g, w_in, g_aq, g_ak, g_kv, g_kr, g_qn, g_qr, g_kn,
                   w_kv_b, w_pa, w_pb, w_out, norm_ffn_g, w_up, w_down):
    w = w_in[l]
    c = 0
    parts = []
    for width in (3 * BAND_WIDTH, MLA_HEADS * MLA_QK, MLA_KV_RANK, MLA_ROPE, D_MODEL, D_MODEL):
        parts.append(w[:, c:c + width])
        c += width
    w_a, w_bq, w_ckv, w_kr, w_ga, w_gb = parts
    w_bq = w_bq.reshape(D_MODEL, MLA_HEADS, MLA_QK)
    w_q = jnp.concatenate([w_bq[..., :MLA_NOPE], _with_partner(w_bq[..., MLA_NOPE:])], axis=-1)
    w_q = w_q.reshape(D_MODEL, MLA_HEADS * MLA_QK_PAD)
    row = lambda g: g[l][None, :].astype(F32)
    return dict(
        gx=row(norm_mix_g),
        w_proj=jnp.concatenate([w_a, w_q, w_ckv], axis=1).astype(BF16),
        w_kr=_with_partner(w_kr).astype(BF16),
        w_kv=w_a[:, BAND_WIDTH:].astype(BF16),
        g_a=jnp.concatenate([jnp.tile(g_aq[l] * BAND_QSCALE, BAND_HEADS), jnp.tile(g_ak[l], BAND_HEADS),
                             jnp.ones((BAND_WIDTH,), F32)])[None, :],
        g_ak=row(g_ak),
        g_qn=row(g_qn),
        g_qr=_with_partner(g_qr[l])[None, :],
        g_kv=row(g_kv),
        g_kr=_with_partner(g_kr[l])[None, :],
        w_kv_b=w_kv_b[l].astype(BF16),
        g_kn=row(g_kn),
        w_ga=w_ga.astype(BF16), w_gb=w_gb.astype(BF16),
        w_pa=w_pa[l].astype(BF16), w_pb=w_pb[l].astype(BF16),
        w_out=w_out[l].astype(BF16),
        g_ffn=row(norm_ffn_g),
        w_up=w_up[l].astype(BF16), w_down=w_down[l].astype(BF16),
    )


def _merge_ffn(x2, xn, oa, ob, p, tm):
    mix = _mix(xn, oa, ob, p["w_ga"], p["w_gb"], p["w_pa"], p["w_pb"], tm)
    h = _outproj(x2, mix, p["w_out"], tm)
    return _ffn(h, p["g_ffn"], p["w_up"], p["w_down"], tm)


def _cache_shape(kv, lead):
    return tuple(kv[n].reshape(lead + (BAND_HEADS, BAND_HEAD_DIM)) for n in range(2))


def kernel(x_prompt, x_sample, cache_a_k, cache_a_v, cache_mla_ckv, cache_mla_krope, norm_mix_g, w_in, g_aq, g_ak, rel_bias, g_kv, g_kr, g_qn, g_qr, g_kn, w_kv_b, w_pa, w_pb, w_out, norm_ffn_g, w_up, w_down):
    b, s, _ = x_prompt.shape
    bs, t, _ = x_sample.shape
    past = cache_mla_ckv.shape[2]
    n_band = cache_a_k.shape[2]
    depth = w_in.shape[0]
    keep = min(BAND_PAST, s)
    tm_p = 1024
    tq_mla = 512
    tm_s = bs * t
    assert s % tm_p == 0 and s % BAND_TQ == 0 and s % tq_mla == 0 and t == CHUNK and past % CHUNK == 0
    assert (b * keep) % tm_s == 0
    cs_p = _rope_table(jnp.arange(s), max(s, tm_p))
    cs_s = _rope_table(past + jnp.arange(t), tm_s)

    yp = x_prompt.reshape(b * s, D_MODEL)
    ys = x_sample.reshape(bs * t, D_MODEL)
    outs = [[] for _ in range(8)]
    for l in range(depth):
        p = _layer_weights(l, norm_mix_g, w_in, g_aq, g_ak, g_kv, g_kr, g_qn, g_qr, g_kn,
                           w_kv_b, w_pa, w_pb, w_out, norm_ffn_g, w_up, w_down)
        bias = _band_bias(rel_bias[l])

        xn, qkv, q_mla, ckv, ckv_bf, krr, kr = _proj(yp, p, cs_p, tm_p)
        x_keep = yp.reshape(b, s, D_MODEL)[:, s - keep:].reshape(b * keep, D_MODEL)
        ak, av = _cache_shape(_kv_f32(x_keep, p["gx"], p["w_kv"], p["g_ak"], tm_s), (b, keep))
        oa = _band_prompt(qkv.reshape(b, s, 3 * BAND_WIDTH), bias)
        k_mla, vt_mla = _expand(ckv_bf, p["w_kv_b"], p["g_kn"], krr, tq_mla, True)
        ob = _mla_prompt(q_mla.reshape(b, s, -1), k_mla.reshape(b, s, -1),
                         vt_mla.reshape(b, s // tq_mla, MLA_WIDTH, tq_mla), tq_mla)
        for o, val in zip(outs[:4], (ak, av, ckv.reshape(b, s, MLA_KV_RANK), kr.reshape(b, s, MLA_ROPE))):
            o.append(val)
        yp = _merge_ffn(yp, xn, oa.reshape(b * s, BAND_WIDTH), ob.reshape(b * s, MLA_WIDTH), p, tm_p)

        xn, qkv, q_mla, ckv, ckv_bf, krr, kr = _proj(ys, p, cs_s, tm_s)
        ak, av = _cache_shape(_kv_f32(ys, p["gx"], p["w_kv"], p["g_ak"], tm_s), (bs, t))
        oa = _band_sample(qkv.reshape(bs, t, 3 * BAND_WIDTH),
                          cache_a_k[l].reshape(bs, n_band * BAND_HEADS, BAND_HEAD_DIM),
                          cache_a_v[l].reshape(bs, n_band * BAND_HEADS, BAND_HEAD_DIM), bias)
        ckv_all = jnp.concatenate([cache_mla_ckv[l].astype(BF16), ckv_bf.reshape(bs, t, MLA_KV_RANK)], axis=1)
        kr_old = cache_mla_krope[l].astype(F32)
        krr_all = jnp.concatenate([jnp.concatenate([kr_old, kr_old], axis=-1), krr.reshape(bs, t, LANES)], axis=1)
        n_all = past + t
        k_mla, v_mla = _expand(ckv_all.reshape(bs * n_all, MLA_KV_RANK), p["w_kv_b"], p["g_kn"],
                               krr_all.reshape(bs * n_all, LANES), n_all // 4, False)
        ob = _mla_sample(q_mla.reshape(bs, t, -1), k_mla.reshape(bs, n_all, -1), v_mla.reshape(bs, n_all, -1))
        for o, val in zip(outs[4:], (ak, av, ckv.reshape(bs, t, MLA_KV_RANK), kr.reshape(bs, t, MLA_ROPE))):
            o.append(val)
        ys = _merge_ffn(ys, xn, oa.reshape(bs * t, BAND_WIDTH), ob.reshape(bs * t, MLA_WIDTH), p, tm_s)

    return (yp.reshape(b, s, D_MODEL), ys.reshape(bs, t, D_MODEL)) + tuple(jnp.stack(o) for o in outs)
```

```python
import functools

import jax
import jax.numpy as jnp
from jax import lax
from jax.experimental import pallas as pl
from jax.experimental.pallas import tpu as pltpu

F32 = jnp.float32
BF16 = jnp.bfloat16

D_MODEL = 2048
CHUNK = 64
BAND_PAST = 8 * CHUNK
BAND_HEADS = 8
BAND_HEAD_DIM = 128
BAND_WIDTH = BAND_HEADS * BAND_HEAD_DIM
REL_MAX = 256
REL_SIZE = (CHUNK - 1) + REL_MAX + 1
BAND_SCALE = BAND_HEAD_DIM ** -0.5
MLA_HEADS = 8
MLA_NOPE = 128
MLA_ROPE = 64
MLA_QK = MLA_NOPE + MLA_ROPE
MLA_V = 128
MLA_WIDTH = MLA_HEADS * MLA_V
MLA_KV_RANK = 512
MLA_SCALE = MLA_QK ** -0.5
ROPE_THETA = 10000.0
D_FF = 4 * D_MODEL
EPS = 1e-6
NEG = -1e30
LOG2E = 1.4426950408889634
MLA_QSCALE = MLA_SCALE * LOG2E
BAND_QSCALE = BAND_SCALE * LOG2E

LANES = 128
MLA_QK_PAD = 2 * LANES
BAND_TQ = 256
BAND_WIN = BAND_TQ + BAND_PAST
BAND_PIECES = BAND_WIN // BAND_TQ
VMEM_LIMIT = 56 * 1024 * 1024
PROJ_TN = 512


def _params(n_axes, vmem=VMEM_LIMIT):
    return pltpu.CompilerParams(dimension_semantics=("arbitrary",) * n_axes,
                                vmem_limit_bytes=vmem)


def _rms(a, g):
    return a * lax.rsqrt(jnp.mean(a * a, axis=-1, keepdims=True) + EPS) * g


def _dot(a, b):
    return jnp.dot(a, b, preferred_element_type=F32)


def _dot_nt(a, b):
    return lax.dot_general(a, b, (((1,), (1,)), ((), ())), preferred_element_type=F32)


def _rope_pair(ar, g, cs):
    lane = lax.broadcasted_iota(jnp.int32, (1, LANES), 1)
    first = (lane < MLA_ROPE).astype(F32)
    ss = jnp.sum(ar * ar * first, axis=-1, keepdims=True) / MLA_ROPE
    return ar * lax.rsqrt(ss + EPS) * g * cs


_NB_BAND_NORM = 2 * BAND_WIDTH // PROJ_TN
_NB_BAND = 3 * BAND_WIDTH // PROJ_TN
_NB_MLA_Q = MLA_HEADS * MLA_QK_PAD // PROJ_TN
_NB_PROJ = _NB_BAND + _NB_MLA_Q + 1


def _proj_kernel(x_ref, gx_ref, w_ref, wkr_ref, gh_ref, gqn_ref, gqr_ref, gkv_ref, gkr_ref, cs_ref,
                 xn_out_ref, qkv_ref, q_ref, ckv_ref, ckvb_ref, krr_ref, kr_ref, xn_ref):
    j = pl.program_id(1)

    @pl.when(j == 0)
    def _():
        xn = _rms(x_ref[...], gx_ref[...]).astype(BF16)
        xn_ref[...] = xn
        xn_out_ref[...] = xn

    def block_product():
        return _dot(xn_ref[...], w_ref[...])

    @pl.when(j < _NB_BAND_NORM)
    def _():
        acc = block_product()
        for k in range(PROJ_TN // LANES):
            sl = slice(k * LANES, (k + 1) * LANES)
            qkv_ref[:, sl] = _rms(acc[:, sl], gh_ref[:, sl]).astype(BF16)

    @pl.when(jnp.logical_and(j >= _NB_BAND_NORM, j < _NB_BAND))
    def _():
        qkv_ref[...] = block_product().astype(BF16)

    @pl.when(jnp.logical_and(j >= _NB_BAND, j < _NB_BAND + _NB_MLA_Q))
    def _():
        acc = block_product()
        cs = cs_ref[...]
        for k in range(PROJ_TN // MLA_QK_PAD):
            c0 = k * MLA_QK_PAD
            qn = _rms(acc[:, c0:c0 + LANES], gqn_ref[...]) * MLA_QSCALE
            qr = _rope_pair(acc[:, c0 + LANES:c0 + 2 * LANES], gqr_ref[...], cs) * MLA_QSCALE
            q_ref[:, c0:c0 + LANES] = qn.astype(BF16)
            q_ref[:, c0 + LANES:c0 + 2 * LANES] = qr.astype(BF16)

    @pl.when(j == _NB_BAND + _NB_MLA_Q)
    def _():
        cn = _rms(block_product(), gkv_ref[...])
        ckv_ref[...] = cn
        ckvb_ref[...] = cn.astype(BF16)
        t = _rope_pair(_dot(xn_ref[...], wkr_ref[...]), gkr_ref[...], cs_ref[...])
        krr = t + pltpu.roll(t, MLA_ROPE, 1)
        krr_ref[...] = krr
        kr_ref[...] = krr[:, :MLA_ROPE]


def _proj(x, p, cs, tm):
    m = x.shape[0]
    tn = PROJ_TN
    n_cs = cs.shape[0] // tm
    const = lambda i, j: (0, 0)
    rows = lambda i, j: (i, 0)
    return pl.pallas_call(
        _proj_kernel,
        grid=(m // tm, _NB_PROJ),
        in_specs=[pl.BlockSpec((tm, D_MODEL), rows),
                  pl.BlockSpec((1, D_MODEL), const),
                  pl.BlockSpec((D_MODEL, tn), lambda i, j: (0, j)),
                  pl.BlockSpec((D_MODEL, LANES), const),
                  pl.BlockSpec((1, tn), lambda i, j: (0, jnp.minimum(j, _NB_BAND - 1))),
                  pl.BlockSpec((1, LANES), const),
                  pl.BlockSpec((1, LANES), const),
                  pl.BlockSpec((1, MLA_KV_RANK), const),
                  pl.BlockSpec((1, LANES), const),
                  pl.BlockSpec((tm, LANES), lambda i, j: (i % n_cs, 0))],
        out_specs=[pl.BlockSpec((tm, D_MODEL), rows),
                   pl.BlockSpec((tm, tn), lambda i, j: (i, jnp.minimum(j, _NB_BAND - 1))),
                   pl.BlockSpec((tm, tn), lambda i, j: (i, jnp.clip(j - _NB_BAND, 0, _NB_MLA_Q - 1))),
                   pl.BlockSpec((tm, MLA_KV_RANK), rows),
                   pl.BlockSpec((tm, MLA_KV_RANK), rows),
                   pl.BlockSpec((tm, LANES), rows),
                   pl.BlockSpec((tm, MLA_ROPE), rows)],
        out_shape=[jax.ShapeDtypeStruct((m, D_MODEL), BF16),
                   jax.ShapeDtypeStruct((m, 3 * BAND_WIDTH), BF16),
                   jax.ShapeDtypeStruct((m, MLA_HEADS * MLA_QK_PAD), BF16),
                   jax.ShapeDtypeStruct((m, MLA_KV_RANK), F32),
                   jax.ShapeDtypeStruct((m, MLA_KV_RANK), BF16),
                   jax.ShapeDtypeStruct((m, LANES), F32),
                   jax.ShapeDtypeStruct((m, MLA_ROPE), F32)],
        scratch_shapes=[pltpu.VMEM((tm, D_MODEL), BF16)],
        compiler_params=_params(2), name="proj")(
            x, p["gx"], p["w_proj"], p["w_kr"], p["g_a"], p["g_qn"], p["g_qr"], p["g_kv"], p["g_kr"], cs)


def _kv_f32_kernel(x_ref, gx_ref, w_ref, g_ref, o_ref, xn_ref, *, n_norm):
    j = pl.program_id(1)

    @pl.when(j == 0)
    def _():
        xn_ref[...] = _rms(x_ref[...], gx_ref[...]).astype(BF16)

    @pl.when(j < n_norm)
    def _():
        acc = _dot(xn_ref[...], w_ref[...])
        for k in range(acc.shape[1] // LANES):
            sl = slice(k * LANES, (k + 1) * LANES)
            o_ref[0, :, sl] = _rms(acc[:, sl], g_ref[...])

    @pl.when(j >= n_norm)
    def _():
        o_ref[0] = _dot(xn_ref[...], w_ref[...])


def _kv_f32(x, gx, w, g, tm, tn=512):
    m = x.shape[0]
    nb = BAND_WIDTH // tn
    return pl.pallas_call(
        functools.partial(_kv_f32_kernel, n_norm=nb),
        grid=(m // tm, 2 * nb),
        in_specs=[pl.BlockSpec((tm, D_MODEL), lambda i, j: (i, 0)),
                  pl.BlockSpec((1, D_MODEL), lambda i, j: (0, 0)),
                  pl.BlockSpec((D_MODEL, tn), lambda i, j: (0, j)),
                  pl.BlockSpec((1, LANES), lambda i, j: (0, 0))],
        out_specs=pl.BlockSpec((1, tm, tn), lambda i, j: (j // nb, i, j % nb)),
        out_shape=jax.ShapeDtypeStruct((2, m, BAND_WIDTH), F32),
        scratch_shapes=[pltpu.VMEM((tm, D_MODEL), BF16)],
        compiler_params=_params(2), name="kv_f32")(x, gx, w, g)


def _expand_kernel(c_ref, w_ref, gkn_ref, krr_ref, k_ref, v_ref, *, v_transposed):
    c = c_ref[...]
    krr = krr_ref[...].astype(BF16)
    for h in range(MLA_HEADS):
        a = _dot(c, w_ref[:, h * 2 * LANES:(h + 1) * 2 * LANES])
        k_ref[:, h * MLA_QK_PAD:h * MLA_QK_PAD + LANES] = _rms(a[:, :LANES], gkn_ref[...]).astype(BF16)
        k_ref[:, h * MLA_QK_PAD + LANES:(h + 1) * MLA_QK_PAD] = krr
        if v_transposed:
            v_ref[0, h * MLA_V:(h + 1) * MLA_V, :] = a[:, LANES:].T.astype(BF16)
        else:
            v_ref[:, h * MLA_V:(h + 1) * MLA_V] = a[:, LANES:].astype(BF16)


def _expand(c, w, gkn, krr, tm, v_transposed):
    m = c.shape[0]
    if v_transposed:
        v_spec = pl.BlockSpec((1, MLA_WIDTH, tm), lambda i: (i, 0, 0))
        v_shape = jax.ShapeDtypeStruct((m // tm, MLA_WIDTH, tm), BF16)
    else:
        v_spec = pl.BlockSpec((tm, MLA_WIDTH), lambda i: (i, 0))
        v_shape = jax.ShapeDtypeStruct((m, MLA_WIDTH), BF16)
    return pl.pallas_call(
        functools.partial(_expand_kernel, v_transposed=v_transposed),
        grid=(m // tm,),
        in_specs=[pl.BlockSpec((tm, MLA_KV_RANK), lambda i: (i, 0)),
                  pl.BlockSpec(w.shape, lambda i: (0, 0)),
                  pl.BlockSpec((1, LANES), lambda i: (0, 0)),
                  pl.BlockSpec((tm, LANES), lambda i: (i, 0))],
        out_specs=[pl.BlockSpec((tm, MLA_HEADS * MLA_QK_PAD), lambda i: (i, 0)), v_spec],
        out_shape=[jax.ShapeDtypeStruct((m, MLA_HEADS * MLA_QK_PAD), BF16), v_shape],
        compiler_params=_params(1), name="expand")(c, w, gkn, krr)


def _band_bias_kernel(b_ref, o_ref):
    n_missing = BAND_PIECES - 1 - pl.program_id(0)
    t = pltpu.roll(jnp.broadcast_to(b_ref[0], (BAND_TQ, 2 * BAND_PAST)), 0, 1,
                   stride=1, stride_axis=0)[:, :BAND_WIN] * LOG2E
    col = lax.broadcasted_iota(jnp.int32, (BAND_TQ, BAND_WIN), 1)
    qc = lax.broadcasted_iota(jnp.int32, (BAND_TQ, BAND_WIN), 0) // CHUNK
    kc = col // CHUNK
    t = jnp.where(kc >= qc, jnp.where(kc <= qc + BAND_PAST // CHUNK, t, NEG), NEG)
    o_ref[0, 0] = jnp.where(col >= n_missing * BAND_TQ, t, NEG)


def _band_bias(rel_bias):
    far = jnp.broadcast_to(rel_bias[:, REL_SIZE - 1:], (BAND_HEADS, BAND_PAST - REL_MAX))
    near = jnp.broadcast_to(rel_bias[:, :1], (BAND_HEADS, BAND_WIN - BAND_PAST - CHUNK))
    wrap = jnp.broadcast_to(rel_bias[:, REL_SIZE - 1:], (BAND_HEADS, 2 * BAND_PAST - BAND_WIN))
    row0 = jnp.concatenate([far, rel_bias[:, ::-1], near, wrap], axis=1)
    assert row0.shape == (BAND_HEADS, 2 * BAND_PAST)
    return pl.pallas_call(
        _band_bias_kernel,
        grid=(BAND_PIECES, BAND_HEADS),
        in_specs=[pl.BlockSpec((1, 1, 2 * BAND_PAST), lambda v, h: (h, 0, 0))],
        out_specs=pl.BlockSpec((1, 1, BAND_TQ, BAND_WIN), lambda v, h: (v, h, 0, 0)),
        out_shape=jax.ShapeDtypeStruct((BAND_PIECES, BAND_HEADS, BAND_TQ, BAND_WIN), F32),
        compiler_params=_params(2), name="band_bias")(row0[:, None, :])


def _softmax_pv(s, v):
    m = functools.reduce(jnp.maximum, [jnp.max(sp, axis=-1, keepdims=True) for sp in s])
    e = [jnp.exp2(sp - m) for sp in s]
    l = functools.reduce(jnp.add, [jnp.sum(ep, axis=-1, keepdims=True) for ep in e])
    o = functools.reduce(jnp.add, [_dot(ep.astype(BF16), vp) for ep, vp in zip(e, v)])
    return (o / l).astype(BF16)


def _band_prompt_kernel(q_ref, ka_ref, kb_ref, kc_ref, va_ref, vb_ref, vc_ref, bias_ref, o_ref):
    k_refs = (ka_ref, kb_ref, kc_ref)
    v_refs = (va_ref, vb_ref, vc_ref)
    for h in range(BAND_HEADS):
        hs = slice(h * BAND_HEAD_DIM, (h + 1) * BAND_HEAD_DIM)
        q = q_ref[0, :, hs]
        s = [_dot_nt(q, k_refs[p][0, :, hs]) + bias_ref[0, h, :, p * BAND_TQ:(p + 1) * BAND_TQ]
             for p in range(BAND_PIECES)]
        o_ref[0, :, hs] = _softmax_pv(s, [v_refs[p][0, :, hs] for p in range(BAND_PIECES)])


def _band_prompt(qkv, bias):
    b, s, _ = qkv.shape
    blk = (1, BAND_TQ, BAND_WIDTH)

    def kv_spec(col, back):
        return pl.BlockSpec(blk, lambda bi, i: (bi, jnp.maximum(i - back, 0), col))

    return pl.pallas_call(
        _band_prompt_kernel,
        grid=(b, s // BAND_TQ),
        in_specs=[pl.BlockSpec(blk, lambda bi, i: (bi, i, 0))]
                 + [kv_spec(1, BAND_PIECES - 1 - p) for p in range(BAND_PIECES)]
                 + [kv_spec(2, BAND_PIECES - 1 - p) for p in range(BAND_PIECES)]
                 + [pl.BlockSpec((1,) + bias.shape[1:],
                                 lambda bi, i: (jnp.minimum(i, BAND_PIECES - 1), 0, 0, 0))],
        out_specs=pl.BlockSpec(blk, lambda bi, i: (bi, i, 0)),
        out_shape=jax.ShapeDtypeStruct((b, s, BAND_WIDTH), BF16),
        compiler_params=_params(2), name="band_prompt")(qkv, qkv, qkv, qkv, qkv, qkv, qkv, bias)


def _band_sample_kernel(q_ref, kn_ref, vn_ref, kc_ref, vc_ref, bias_ref, o_ref, *, n_past):
    t = q_ref.shape[1]
    for h in range(BAND_HEADS):
        hs = slice(h * BAND_HEAD_DIM, (h + 1) * BAND_HEAD_DIM)
        q = q_ref[0, :, hs]
        k_old = kc_ref[0, pl.ds(h, n_past, stride=BAND_HEADS), :].astype(BF16)
        v_old = vc_ref[0, pl.ds(h, n_past, stride=BAND_HEADS), :].astype(BF16)
        s = [_dot_nt(q, k_old) + bias_ref[0, h, :t, :n_past],
             _dot_nt(q, kn_ref[0, :, hs]) + bias_ref[0, h, :t, n_past:n_past + t]]
        o_ref[0, :, hs] = _softmax_pv(s, [v_old, vn_ref[0, :, hs]])


def _band_sample(qkv, k_cache, v_cache, bias):
    b, t, _ = qkv.shape
    n_past = k_cache.shape[1] // BAND_HEADS
    new = (1, t, BAND_WIDTH)
    old = (1, n_past * BAND_HEADS, BAND_HEAD_DIM)
    return pl.pallas_call(
        functools.partial(_band_sample_kernel, n_past=n_past),
        grid=(b,),
        in_specs=[pl.BlockSpec(new, lambda bi: (bi, 0, 0)),
                  pl.BlockSpec(new, lambda bi: (bi, 0, 1)),
                  pl.BlockSpec(new, lambda bi: (bi, 0, 2)),
                  pl.BlockSpec(old, lambda bi: (bi, 0, 0)),
                  pl.BlockSpec(old, lambda bi: (bi, 0, 0)),
                  pl.BlockSpec((1,) + bias.shape[1:], lambda bi: (BAND_PIECES - 1, 0, 0, 0))],
        out_specs=pl.BlockSpec(new, lambda bi: (bi, 0, 0)),
        out_shape=jax.ShapeDtypeStruct((b, t, BAND_WIDTH), BF16),
        compiler_params=_params(1), name="band_sample")(qkv, qkv, qkv, k_cache, v_cache, bias)


def _mla_prompt_kernel(q_ref, k_ref, vt_ref, o_ref, sa_ref, sb_ref, mask_ref, *, tq, nq):
    def scores(i, j):
        return _dot_nt(k_ref[0, j * tq:(j + 1) * tq, :], q_ref[0, i * tq:(i + 1) * tq, :])

    kc = lax.broadcasted_iota(jnp.int32, (tq, tq), 0) // CHUNK
    qc = lax.broadcasted_iota(jnp.int32, (tq, tq), 1) // CHUNK
    mask_ref[...] = jnp.where(kc <= qc, 0.0, NEG)

    pairs = [(i, j) for i in range(nq) for j in range(i + 1)]
    bufs = (sa_ref, sb_ref)
    bufs[0][...] = scores(*pairs[0])
    for n, (i, j) in enumerate(pairs):
        cur, nxt = bufs[n % 2], bufs[(n + 1) % 2]
        if n + 1 < len(pairs):
            nxt[...] = scores(*pairs[n + 1])
        s = cur[...]
        if j == i:
            s = s + mask_ref[...]
        if j == 0:
            m = jnp.max(s, axis=0, keepdims=True)
            p = jnp.exp2(s - m)
            l = jnp.sum(p, axis=0, keepdims=True)
            acc = _dot(vt_ref[0, j], p.astype(BF16))
        else:
            m_new = jnp.maximum(m, jnp.max(s, axis=0, keepdims=True))
            alpha = jnp.exp2(m - m_new)
            p = jnp.exp2(s - m_new)
            l = alpha * l + jnp.sum(p, axis=0, keepdims=True)
            acc = alpha * acc + _dot(vt_ref[0, j], p.astype(BF16))
            m = m_new
        if j == i:
            o_ref[0, i * tq:(i + 1) * tq, :] = (acc / l).T.astype(BF16)


def _mla_prompt(q, k, vt, tq):
    b, s, _ = q.shape
    nq = s // tq
    return pl.pallas_call(
        functools.partial(_mla_prompt_kernel, tq=tq, nq=nq),
        grid=(b, MLA_HEADS),
        in_specs=[pl.BlockSpec((1, s, MLA_QK_PAD), lambda bi, h: (bi, 0, h)),
                  pl.BlockSpec((1, s, MLA_QK_PAD), lambda bi, h: (bi, 0, h)),
                  pl.BlockSpec((1, nq, MLA_V, tq), lambda bi, h: (bi, 0, h, 0))],
        out_specs=pl.BlockSpec((1, s, MLA_V), lambda bi, h: (bi, 0, h)),
        out_shape=jax.ShapeDtypeStruct((b, s, MLA_WIDTH), BF16),
        scratch_shapes=[pltpu.VMEM((tq, tq), F32), pltpu.VMEM((tq, tq), F32),
                        pltpu.VMEM((tq, tq), F32)],
        compiler_params=_params(2), name="mla_prompt")(q, k, vt)


def _mla_sample_kernel(q_ref, k_ref, v_ref, o_ref):
    s = _dot_nt(q_ref[0], k_ref[0])
    e = jnp.exp2(s - jnp.max(s, axis=-1, keepdims=True))
    o = _dot(e.astype(BF16), v_ref[0])
    o_ref[0] = (o / jnp.sum(e, axis=-1, keepdims=True)).astype(BF16)


def _mla_sample(q, k, v):
    b, t, _ = q.shape
    l = k.shape[1]
    return pl.pallas_call(
        _mla_sample_kernel,
        grid=(b, MLA_HEADS),
        in_specs=[pl.BlockSpec((1, t, MLA_QK_PAD), lambda bi, h: (bi, 0, h)),
                  pl.BlockSpec((1, l, MLA_QK_PAD), lambda bi, h: (bi, 0, h)),
                  pl.BlockSpec((1, l, MLA_V), lambda bi, h: (bi, 0, h))],
        out_specs=pl.BlockSpec((1, t, MLA_V), lambda bi, h: (bi, 0, h)),
        out_shape=jax.ShapeDtypeStruct((b, t, MLA_WIDTH), BF16),
        compiler_params=_params(2), name="mla_sample")(q, k, v)


def _mix_kernel(xn_ref, oa_ref, ob_ref, wga_ref, wgb_ref, wpa_ref, wpb_ref, o_ref):
    xn = xn_ref[...]
    a = jax.nn.sigmoid(_dot(xn, wga_ref[...])) * _dot(oa_ref[...], wpa_ref[...])
    b = jax.nn.sigmoid(_dot(xn, wgb_ref[...])) * _dot(ob_ref[...], wpb_ref[...])
    o_ref[...] = (a + b).astype(BF16)


def _mix(xn, oa, ob, wga, wgb, wpa, wpb, tm, tn=512):
    m = xn.shape[0]
    return pl.pallas_call(
        _mix_kernel,
        grid=(m // tm, D_MODEL // tn),
        in_specs=[pl.BlockSpec((tm, D_MODEL), lambda i, j: (i, 0)),
                  pl.BlockSpec((tm, BAND_WIDTH), lambda i, j: (i, 0)),
                  pl.BlockSpec((tm, MLA_WIDTH), lambda i, j: (i, 0)),
                  pl.BlockSpec((D_MODEL, tn), lambda i, j: (0, j)),
                  pl.BlockSpec((D_MODEL, tn), lambda i, j: (0, j)),
                  pl.BlockSpec((BAND_WIDTH, tn), lambda i, j: (0, j)),
                  pl.BlockSpec((MLA_WIDTH, tn), lambda i, j: (0, j))],
        out_specs=pl.BlockSpec((tm, tn), lambda i, j: (i, j)),
        out_shape=jax.ShapeDtypeStruct((m, D_MODEL), BF16),
        compiler_params=_params(2), name="mix")(xn, oa, ob, wga, wgb, wpa, wpb)


def _outproj_kernel(x_ref, mix_ref, w_ref, o_ref):
    o_ref[...] = x_ref[...] + _dot(mix_ref[...], w_ref[...])


def _outproj(x, mix, w, tm, tn=1024):
    m = x.shape[0]
    return pl.pallas_call(
        _outproj_kernel,
        grid=(m // tm, D_MODEL // tn),
        in_specs=[pl.BlockSpec((tm, tn), lambda i, j: (i, j)),
                  pl.BlockSpec((tm, D_MODEL), lambda i, j: (i, 0)),
                  pl.BlockSpec((D_MODEL, tn), lambda i, j: (0, j))],
        out_specs=pl.BlockSpec((tm, tn), lambda i, j: (i, j)),
        out_shape=jax.ShapeDtypeStruct((m, D_MODEL), F32),
        compiler_params=_params(2), name="outproj")(x, mix, w)


def _ffn_kernel(h_ref, g_ref, wu_ref, wd_ref, o_ref, hn_ref):
    @pl.when(pl.program_id(1) == 0)
    def _():
        h = h_ref[...]
        hn_ref[...] = _rms(h, g_ref[...]).astype(BF16)
        o_ref[...] = h

    u = jnp.maximum(_dot(hn_ref[...], wu_ref[...]), 0.0)
    o_ref[...] += _dot((u * u).astype(BF16), wd_ref[...])


def _ffn(h, g, wu, wd, tm, tf=512):
    m = h.shape[0]
    return pl.pallas_call(
        _ffn_kernel,
        grid=(m // tm, D_FF // tf),
        in_specs=[pl.BlockSpec((tm, D_MODEL), lambda i, f: (i, 0)),
                  pl.BlockSpec((1, D_MODEL), lambda i, f: (0, 0)),
                  pl.BlockSpec((D_MODEL, tf), lambda i, f: (0, f)),
                  pl.BlockSpec((tf, D_MODEL), lambda i, f: (f, 0))],
        out_specs=pl.BlockSpec((tm, D_MODEL), lambda i, f: (i, 0)),
        out_shape=jax.ShapeDtypeStruct((m, D_MODEL), F32),
        scratch_shapes=[pltpu.VMEM((tm, D_MODEL), BF16)],
        compiler_params=_params(2), name="ffn")(h, g, wu, wd)


def _rope_table(pos, rows):
    half = MLA_ROPE // 2
    freqs = ROPE_THETA ** (-(jnp.arange(half, dtype=F32) / half))
    ang = pos.astype(F32)[:, None] * freqs[None, :]
    cos = jnp.cos(ang)
    sin = jnp.sin(ang)
    table = jnp.concatenate([cos, cos, -sin, sin], axis=1)
    return jnp.tile(table, (rows // table.shape[0], 1))


def _with_partner(w):
    half = MLA_ROPE // 2
    return jnp.concatenate([w, w[..., half:], w[..., :half]], axis=-1)


def _layer_weights(l, norm_mix_g, w_in, g_aq, g_ak, g_kv, g_kr, g_qn, g_qr, g_kn,
                   w_kv_b, w_pa, w_pb, w_out, norm_ffn_g, w_up, w_down):
    w = w_in[l]
    c = 0
    parts = []
    for width in (3 * BAND_WIDTH, MLA_HEADS * MLA_QK, MLA_KV_RANK, MLA_ROPE, D_MODEL, D_MODEL):
        parts.append(w[:, c:c + width])
        c += width
    w_a, w_bq, w_ckv, w_kr, w_ga, w_gb = parts
    w_bq = w_bq.reshape(D_MODEL, MLA_HEADS, MLA_QK)
    w_q = jnp.concatenate([w_bq[..., :MLA_NOPE], _with_partner(w_bq[..., MLA_NOPE:])], axis=-1)
    w_q = w_q.reshape(D_MODEL, MLA_HEADS * MLA_QK_PAD)
    row = lambda g: g[l][None, :].astype(F32)
    return dict(
        gx=row(norm_mix_g),
        w_proj=jnp.concatenate([w_a, w_q, w_ckv], axis=1).astype(BF16),
        w_kr=_with_partner(w_kr).astype(BF16),
        w_kv=w_a[:, BAND_WIDTH:].astype(BF16),
        g_a=jnp.concatenate([jnp.tile(g_aq[l] * BAND_QSCALE, BAND_HEADS), jnp.tile(g_ak[l], BAND_HEADS),
                             jnp.ones((BAND_WIDTH,), F32)])[None, :],
        g_ak=row(g_ak),
        g_qn=row(g_qn),
        g_qr=_with_partner(g_qr[l])[None, :],
        g_kv=row(g_kv),
        g_kr=_with_partner(g_kr[l])[None, :],
        w_kv_b=w_kv_b[l].astype(BF16),
        g_kn=row(g_kn),
        w_ga=w_ga.astype(BF16), w_gb=w_gb.astype(BF16),
        w_pa=w_pa[l].astype(BF16), w_pb=w_pb[l].astype(BF16),
        w_out=w_out[l].astype(BF16),
        g_ffn=row(norm_ffn_g),
        w_up=w_up[l].astype(BF16), w_down=w_down[l].astype(BF16),
    )


def _merge_ffn(x2, xn, oa, ob, p, tm):
    mix = _mix(xn, oa, ob, p["w_ga"], p["w_gb"], p["w_pa"], p["w_pb"], tm)
    h = _outproj(x2, mix, p["w_out"], tm)
    return _ffn(h, p["g_ffn"], p["w_up"], p["w_down"], tm)


def _cache_shape(kv, lead):
    return tuple(kv[n].reshape(lead + (BAND_HEADS, BAND_HEAD_DIM)) for n in range(2))


def kernel(x_prompt, x_sample, cache_a_k, cache_a_v, cache_mla_ckv, cache_mla_krope, norm_mix_g, w_in, g_aq, g_ak, rel_bias, g_kv, g_kr, g_qn, g_qr, g_kn, w_kv_b, w_pa, w_pb, w_out, norm_ffn_g, w_up, w_down):
    b, s, _ = x_prompt.shape
    bs, t, _ = x_sample.shape
    past = cache_mla_ckv.shape[2]
    n_band = cache_a_k.shape[2]
    depth = w_in.shape[0]
    keep = min(BAND_PAST, s)
    tm_p = 1024
    tq_mla = 512
    tm_s = bs * t
    assert s % tm_p == 0 and s % BAND_TQ == 0 and s % tq_mla == 0 and t == CHUNK and past % CHUNK == 0
    assert (b * keep) % tm_s == 0
    cs_p = _rope_table(jnp.arange(s), max(s, tm_p))
    cs_s = _rope_table(past + jnp.arange(t), tm_s)

    yp = x_prompt.reshape(b * s, D_MODEL)
    ys = x_sample.reshape(bs * t, D_MODEL)
    outs = [[] for _ in range(8)]
    for l in range(depth):
        p = _layer_weights(l, norm_mix_g, w_in, g_aq, g_ak, g_kv, g_kr, g_qn, g_qr, g_kn,
                           w_kv_b, w_pa, w_pb, w_out, norm_ffn_g, w_up, w_down)
        bias = _band_bias(rel_bias[l])

        xn, qkv, q_mla, ckv, ckv_bf, krr, kr = _proj(yp, p, cs_p, tm_p)
        x_keep = yp.reshape(b, s, D_MODEL)[:, s - keep:].reshape(b * keep, D_MODEL)
        ak, av = _cache_shape(_kv_f32(x_keep, p["gx"], p["w_kv"], p["g_ak"], tm_s), (b, keep))
        oa = _band_prompt(qkv.reshape(b, s, 3 * BAND_WIDTH), bias)
        k_mla, vt_mla = _expand(ckv_bf, p["w_kv_b"], p["g_kn"], krr, tq_mla, True)
        ob = _mla_prompt(q_mla.reshape(b, s, -1), k_mla.reshape(b, s, -1),
                         vt_mla.reshape(b, s // tq_mla, MLA_WIDTH, tq_mla), tq_mla)
        for o, val in zip(outs[:4], (ak, av, ckv.reshape(b, s, MLA_KV_RANK), kr.reshape(b, s, MLA_ROPE))):
            o.append(val)
        yp = _merge_ffn(yp, xn, oa.reshape(b * s, BAND_WIDTH), ob.reshape(b * s, MLA_WIDTH), p, tm_p)

        xn, qkv, q_mla, ckv, ckv_bf, krr, kr = _proj(ys, p, cs_s, tm_s)
        ak, av = _cache_shape(_kv_f32(ys, p["gx"], p["w_kv"], p["g_ak"], tm_s), (bs, t))
        oa = _band_sample(qkv.reshape(bs, t, 3 * BAND_WIDTH),
                          cache_a_k[l].reshape(bs, n_band * BAND_HEADS, BAND_HEAD_DIM),
                          cache_a_v[l].reshape(bs, n_band * BAND_HEADS, BAND_HEAD_DIM), bias)
        ckv_all = jnp.concatenate([cache_mla_ckv[l].astype(BF16), ckv_bf.reshape(bs, t, MLA_KV_RANK)], axis=1)
        kr_old = cache_mla_krope[l].astype(F32)
        krr_all = jnp.concatenate([jnp.concatenate([kr_old, kr_old], axis=-1), krr.reshape(bs, t, LANES)], axis=1)
        n_all = past + t
        k_mla, v_mla = _expand(ckv_all.reshape(bs * n_all, MLA_KV_RANK), p["w_kv_b"], p["g_kn"],
                               krr_all.reshape(bs * n_all, LANES), n_all // 4, False)
        ob = _mla_sample(q_mla.reshape(bs, t, -1), k_mla.reshape(bs, n_all, -1), v_mla.reshape(bs, n_all, -1))
        for o, val in zip(outs[4:], (ak, av, ckv.reshape(bs, t, MLA_KV_RANK), kr.reshape(bs, t, MLA_ROPE))):
            o.append(val)
        ys = _merge_ffn(ys, xn, oa.reshape(bs * t, BAND_WIDTH), ob.reshape(bs * t, MLA_WIDTH), p, tm_s)

    return (yp.reshape(b, s, D_MODEL), ys.reshape(bs, t, D_MODEL)) + tuple(jnp.stack(o) for o in outs)
```

```python
import functools

import jax
import jax.numpy as jnp
from jax import lax
from jax.experimental import pallas as pl
from jax.experimental.pallas import tpu as pltpu

F32 = jnp.float32
BF16 = jnp.bfloat16

D_MODEL = 2048
CHUNK = 64
BAND_PAST = 8 * CHUNK
BAND_HEADS = 8
BAND_HEAD_DIM = 128
BAND_WIDTH = BAND_HEADS * BAND_HEAD_DIM
REL_MAX = 256
REL_SIZE = (CHUNK - 1) + REL_MAX + 1
BAND_SCALE = BAND_HEAD_DIM ** -0.5
MLA_HEADS = 8
MLA_NOPE = 128
MLA_ROPE = 64
MLA_QK = MLA_NOPE + MLA_ROPE
MLA_V = 128
MLA_WIDTH = MLA_HEADS * MLA_V
MLA_KV_RANK = 512
MLA_SCALE = MLA_QK ** -0.5
ROPE_THETA = 10000.0
D_FF = 4 * D_MODEL
EPS = 1e-6
NEG = -1e30
LOG2E = 1.4426950408889634
MLA_QSCALE = MLA_SCALE * LOG2E
BAND_QSCALE = BAND_SCALE * LOG2E

LANES = 128
MLA_QK_PAD = 2 * LANES
BAND_TQ = 256
BAND_WIN = BAND_TQ + BAND_PAST
BAND_PIECES = BAND_WIN // BAND_TQ
VMEM_LIMIT = 56 * 1024 * 1024
PROJ_TN = 512
MLA_SCORE_BUFFERS = 3


def _params(n_axes, vmem=VMEM_LIMIT):
    return pltpu.CompilerParams(dimension_semantics=("arbitrary",) * n_axes,
                                vmem_limit_bytes=vmem)


def _rms(a, g):
    return a * lax.rsqrt(jnp.mean(a * a, axis=-1, keepdims=True) + EPS) * g


def _dot(a, b):
    return jnp.dot(a, b, preferred_element_type=F32)


def _dot_nt(a, b):
    return lax.dot_general(a, b, (((1,), (1,)), ((), ())), preferred_element_type=F32)


def _rope_pair(ar, g, cs):
    lane = lax.broadcasted_iota(jnp.int32, (1, LANES), 1)
    first = (lane < MLA_ROPE).astype(F32)
    ss = jnp.sum(ar * ar * first, axis=-1, keepdims=True) / MLA_ROPE
    return ar * lax.rsqrt(ss + EPS) * g * cs


_NB_BAND_NORM = 2 * BAND_WIDTH // PROJ_TN
_NB_BAND = 3 * BAND_WIDTH // PROJ_TN
_NB_MLA_Q = MLA_HEADS * MLA_QK_PAD // PROJ_TN
_NB_PROJ = _NB_BAND + _NB_MLA_Q + 1


def _proj_kernel(x_ref, gx_ref, w_ref, wkr_ref, gh_ref, gqn_ref, gqr_ref, gkv_ref, gkr_ref, cs_ref,
                 xn_out_ref, qkv_ref, vt_ref, q_ref, ckv_ref, ckvb_ref, krr_ref, kr_ref, xn_ref):
    j = pl.program_id(1)

    @pl.when(j == 0)
    def _():
        xn = _rms(x_ref[...], gx_ref[...]).astype(BF16)
        xn_ref[...] = xn
        xn_out_ref[...] = xn

    def block_product():
        return _dot(xn_ref[...], w_ref[...])

    @pl.when(j < _NB_BAND_NORM)
    def _():
        acc = block_product()
        for k in range(PROJ_TN // LANES):
            sl = slice(k * LANES, (k + 1) * LANES)
            qkv_ref[:, sl] = _rms(acc[:, sl], gh_ref[:, sl]).astype(BF16)

    @pl.when(jnp.logical_and(j >= _NB_BAND_NORM, j < _NB_BAND))
    def _():
        acc = block_product()
        qkv_ref[...] = acc.astype(BF16)
        vt_ref[...] = acc.T.astype(BF16)

    @pl.when(jnp.logical_and(j >= _NB_BAND, j < _NB_BAND + _NB_MLA_Q))
    def _():
        acc = block_product()
        cs = cs_ref[...]
        for k in range(PROJ_TN // MLA_QK_PAD):
            c0 = k * MLA_QK_PAD
            qn = _rms(acc[:, c0:c0 + LANES], gqn_ref[...]) * MLA_QSCALE
            qr = _rope_pair(acc[:, c0 + LANES:c0 + 2 * LANES], gqr_ref[...], cs) * MLA_QSCALE
            q_ref[:, c0:c0 + LANES] = qn.astype(BF16)
            q_ref[:, c0 + LANES:c0 + 2 * LANES] = qr.astype(BF16)

    @pl.when(j == _NB_BAND + _NB_MLA_Q)
    def _():
        cn = _rms(block_product(), gkv_ref[...])
        ckv_ref[...] = cn
        ckvb_ref[...] = cn.astype(BF16)
        t = _rope_pair(_dot(xn_ref[...], wkr_ref[...]), gkr_ref[...], cs_ref[...])
        krr = t + pltpu.roll(t, MLA_ROPE, 1)
        krr_ref[...] = krr
        kr_ref[...] = krr[:, :MLA_ROPE]


def _proj(x, p, cs, tm):
    m = x.shape[0]
    tn = PROJ_TN
    n_cs = cs.shape[0] // tm
    const = lambda i, j: (0, 0)
    rows = lambda i, j: (i, 0)
    return pl.pallas_call(
        _proj_kernel,
        grid=(m // tm, _NB_PROJ),
        in_specs=[pl.BlockSpec((tm, D_MODEL), rows),
                  pl.BlockSpec((1, D_MODEL), const),
                  pl.BlockSpec((D_MODEL, tn), lambda i, j: (0, j)),
                  pl.BlockSpec((D_MODEL, LANES), const),
                  pl.BlockSpec((1, tn), lambda i, j: (0, jnp.minimum(j, _NB_BAND - 1))),
                  pl.BlockSpec((1, LANES), const),
                  pl.BlockSpec((1, LANES), const),
                  pl.BlockSpec((1, MLA_KV_RANK), const),
                  pl.BlockSpec((1, LANES), const),
                  pl.BlockSpec((tm, LANES), lambda i, j: (i % n_cs, 0))],
        out_specs=[pl.BlockSpec((tm, D_MODEL), rows),
                   pl.BlockSpec((tm, tn), lambda i, j: (i, jnp.minimum(j, _NB_BAND - 1))),
                   pl.BlockSpec((tn, tm), lambda i, j: (jnp.clip(j - _NB_BAND_NORM, 0, _NB_BAND - _NB_BAND_NORM - 1), i)),
                   pl.BlockSpec((tm, tn), lambda i, j: (i, jnp.clip(j - _NB_BAND, 0, _NB_MLA_Q - 1))),
                   pl.BlockSpec((tm, MLA_KV_RANK), rows),
                   pl.BlockSpec((tm, MLA_KV_RANK), rows),
                   pl.BlockSpec((tm, LANES), rows),
                   pl.BlockSpec((tm, MLA_ROPE), rows)],
        out_shape=[jax.ShapeDtypeStruct((m, D_MODEL), BF16),
                   jax.ShapeDtypeStruct((m, 3 * BAND_WIDTH), BF16),
                   jax.ShapeDtypeStruct((BAND_WIDTH, m), BF16),
                   jax.ShapeDtypeStruct((m, MLA_HEADS * MLA_QK_PAD), BF16),
                   jax.ShapeDtypeStruct((m, MLA_KV_RANK), F32),
                   jax.ShapeDtypeStruct((m, MLA_KV_RANK), BF16),
                   jax.ShapeDtypeStruct((m, LANES), F32),
                   jax.ShapeDtypeStruct((m, MLA_ROPE), F32)],
        scratch_shapes=[pltpu.VMEM((tm, D_MODEL), BF16)],
        compiler_params=_params(2), name="proj")(
            x, p["gx"], p["w_proj"], p["w_kr"], p["g_a"], p["g_qn"], p["g_qr"], p["g_kv"], p["g_kr"], cs)


def _kv_f32_kernel(x_ref, gx_ref, w_ref, g_ref, o_ref, xn_ref, *, n_norm):
    j = pl.program_id(1)

    @pl.when(j == 0)
    def _():
        xn_ref[...] = _rms(x_ref[...], gx_ref[...]).astype(BF16)

    @pl.when(j < n_norm)
    def _():
        acc = _dot(xn_ref[...], w_ref[...])
        for k in range(acc.shape[1] // LANES):
            sl = slice(k * LANES, (k + 1) * LANES)
            o_ref[0, :, sl] = _rms(acc[:, sl], g_ref[...])

    @pl.when(j >= n_norm)
    def _():
        o_ref[0] = _dot(xn_ref[...], w_ref[...])


def _kv_f32(x, gx, w, g, tm, tn=512):
    m = x.shape[0]
    nb = BAND_WIDTH // tn
    return pl.pallas_call(
        functools.partial(_kv_f32_kernel, n_norm=nb),
        grid=(m // tm, 2 * nb),
        in_specs=[pl.BlockSpec((tm, D_MODEL), lambda i, j: (i, 0)),
                  pl.BlockSpec((1, D_MODEL), lambda i, j: (0, 0)),
                  pl.BlockSpec((D_MODEL, tn), lambda i, j: (0, j)),
                  pl.BlockSpec((1, LANES), lambda i, j: (0, 0))],
        out_specs=pl.BlockSpec((1, tm, tn), lambda i, j: (j // nb, i, j % nb)),
        out_shape=jax.ShapeDtypeStruct((2, m, BAND_WIDTH), F32),
        scratch_shapes=[pltpu.VMEM((tm, D_MODEL), BF16)],
        compiler_params=_params(2), name="kv_f32")(x, gx, w, g)


def _expand_kernel(c_ref, w_ref, gkn_ref, krr_ref, k_ref, v_ref, *, v_transposed):
    c = c_ref[...]
    krr = krr_ref[...].astype(BF16)
    for h in range(MLA_HEADS):
        a = _dot(c, w_ref[:, h * 2 * LANES:(h + 1) * 2 * LANES])
        k_ref[:, h * MLA_QK_PAD:h * MLA_QK_PAD + LANES] = _rms(a[:, :LANES], gkn_ref[...]).astype(BF16)
        k_ref[:, h * MLA_QK_PAD + LANES:(h + 1) * MLA_QK_PAD] = krr
        if v_transposed:
            v_ref[0, h * MLA_V:(h + 1) * MLA_V, :] = a[:, LANES:].T.astype(BF16)
        else:
            v_ref[:, h * MLA_V:(h + 1) * MLA_V] = a[:, LANES:].astype(BF16)


def _expand(c, w, gkn, krr, tm, v_transposed):
    m = c.shape[0]
    if v_transposed:
        v_spec = pl.BlockSpec((1, MLA_WIDTH, tm), lambda i: (i, 0, 0))
        v_shape = jax.ShapeDtypeStruct((m // tm, MLA_WIDTH, tm), BF16)
    else:
        v_spec = pl.BlockSpec((tm, MLA_WIDTH), lambda i: (i, 0))
        v_shape = jax.ShapeDtypeStruct((m, MLA_WIDTH), BF16)
    return pl.pallas_call(
        functools.partial(_expand_kernel, v_transposed=v_transposed),
        grid=(m // tm,),
        in_specs=[pl.BlockSpec((tm, MLA_KV_RANK), lambda i: (i, 0)),
                  pl.BlockSpec(w.shape, lambda i: (0, 0)),
                  pl.BlockSpec((1, LANES), lambda i: (0, 0)),
                  pl.BlockSpec((tm, LANES), lambda i: (i, 0))],
        out_specs=[pl.BlockSpec((tm, MLA_HEADS * MLA_QK_PAD), lambda i: (i, 0)), v_spec],
        out_shape=[jax.ShapeDtypeStruct((m, MLA_HEADS * MLA_QK_PAD), BF16), v_shape],
        compiler_params=_params(1), name="expand")(c, w, gkn, krr)


def _band_bias_kernel(b_ref, o_ref, ot_ref):
    n_missing = BAND_PIECES - 1 - pl.program_id(0)
    t = pltpu.roll(jnp.broadcast_to(b_ref[0], (BAND_TQ, 2 * BAND_PAST)), 0, 1,
                   stride=1, stride_axis=0)[:, :BAND_WIN] * LOG2E
    col = lax.broadcasted_iota(jnp.int32, (BAND_TQ, BAND_WIN), 1)
    qc = lax.broadcasted_iota(jnp.int32, (BAND_TQ, BAND_WIN), 0) // CHUNK
    kc = col // CHUNK
    t = jnp.where(kc >= qc, jnp.where(kc <= qc + BAND_PAST // CHUNK, t, NEG), NEG)
    t = jnp.where(col >= n_missing * BAND_TQ, t, NEG)
    o_ref[0, 0] = t
    ot_ref[0, 0] = t.T


def _band_bias(rel_bias):
    far = jnp.broadcast_to(rel_bias[:, REL_SIZE - 1:], (BAND_HEADS, BAND_PAST - REL_MAX))
    near = jnp.broadcast_to(rel_bias[:, :1], (BAND_HEADS, BAND_WIN - BAND_PAST - CHUNK))
    wrap = jnp.broadcast_to(rel_bias[:, REL_SIZE - 1:], (BAND_HEADS, 2 * BAND_PAST - BAND_WIN))
    row0 = jnp.concatenate([far, rel_bias[:, ::-1], near, wrap], axis=1)
    assert row0.shape == (BAND_HEADS, 2 * BAND_PAST)
    return pl.pallas_call(
        _band_bias_kernel,
        grid=(BAND_PIECES, BAND_HEADS),
        in_specs=[pl.BlockSpec((1, 1, 2 * BAND_PAST), lambda v, h: (h, 0, 0))],
        out_specs=[pl.BlockSpec((1, 1, BAND_TQ, BAND_WIN), lambda v, h: (v, h, 0, 0)),
                   pl.BlockSpec((1, 1, BAND_WIN, BAND_TQ), lambda v, h: (v, h, 0, 0))],
        out_shape=[jax.ShapeDtypeStruct((BAND_PIECES, BAND_HEADS, BAND_TQ, BAND_WIN), F32),
                   jax.ShapeDtypeStruct((BAND_PIECES, BAND_HEADS, BAND_WIN, BAND_TQ), F32)],
        compiler_params=_params(2), name="band_bias")(row0[:, None, :])


def _softmax_pv(s, v):
    m = functools.reduce(jnp.maximum, [jnp.max(sp, axis=-1, keepdims=True) for sp in s])
    e = [jnp.exp2(sp - m) for sp in s]
    l = functools.reduce(jnp.add, [jnp.sum(ep, axis=-1, keepdims=True) for ep in e])
    o = functools.reduce(jnp.add, [_dot(ep.astype(BF16), vp) for ep, vp in zip(e, v)])
    return (o / l).astype(BF16)


def _band_prompt_kernel(q_ref, ka_ref, kb_ref, kc_ref, va_ref, vb_ref, vc_ref, bias_ref, o_ref,
                        sa_ref, sb_ref):
    def head(h):
        return slice(h * BAND_HEAD_DIM, (h + 1) * BAND_HEAD_DIM)

    def scores(h):
        k = jnp.concatenate([r[0, :, head(h)] for r in (ka_ref, kb_ref, kc_ref)], axis=0)
        return _dot_nt(k, q_ref[0, :, head(h)]) + bias_ref[0, h]

    bufs = (sa_ref, sb_ref)
    bufs[0][...] = scores(0)
    for h in range(BAND_HEADS):
        if h + 1 < BAND_HEADS:
            bufs[(h + 1) % 2][...] = scores(h + 1)
        s = bufs[h % 2][...]
        vt = jnp.concatenate([r[head(h), :] for r in (va_ref, vb_ref, vc_ref)], axis=1)
        e = jnp.exp2(s - jnp.max(s, axis=0, keepdims=True))
        o = _dot(vt, e.astype(BF16)) / jnp.sum(e, axis=0, keepdims=True)
        o_ref[0, :, head(h)] = o.T.astype(BF16)


def _band_prompt(qkv, vt, bias_t):
    b, s, _ = qkv.shape
    nq = s // BAND_TQ
    blk = (1, BAND_TQ, BAND_WIDTH)

    def k_spec(back):
        return pl.BlockSpec(blk, lambda bi, i: (bi, jnp.maximum(i - back, 0), 1))

    def vt_spec(back):
        return pl.BlockSpec((BAND_WIDTH, BAND_TQ), lambda bi, i: (0, bi * nq + jnp.maximum(i - back, 0)))

    return pl.pallas_call(
        _band_prompt_kernel,
        grid=(b, nq),
        in_specs=[pl.BlockSpec(blk, lambda bi, i: (bi, i, 0))]
                 + [k_spec(BAND_PIECES - 1 - p) for p in range(BAND_PIECES)]
                 + [vt_spec(BAND_PIECES - 1 - p) for p in range(BAND_PIECES)]
                 + [pl.BlockSpec((1,) + bias_t.shape[1:],
                                 lambda bi, i: (jnp.minimum(i, BAND_PIECES - 1), 0, 0, 0))],
        out_specs=pl.BlockSpec(blk, lambda bi, i: (bi, i, 0)),
        out_shape=jax.ShapeDtypeStruct((b, s, BAND_WIDTH), BF16),
        scratch_shapes=[pltpu.VMEM((BAND_WIN, BAND_TQ), F32), pltpu.VMEM((BAND_WIN, BAND_TQ), F32)],
        compiler_params=_params(2), name="band_prompt")(qkv, qkv, qkv, qkv, vt, vt, vt, bias_t)


def _band_sample_kernel(q_ref, kn_ref, vn_ref, kc_ref, vc_ref, bias_ref, o_ref, *, n_past):
    t = q_ref.shape[1]
    for h in range(BAND_HEADS):
        hs = slice(h * BAND_HEAD_DIM, (h + 1) * BAND_HEAD_DIM)
        q = q_ref[0, :, hs]
        k_old = kc_ref[0, pl.ds(h, n_past, stride=BAND_HEADS), :].astype(BF16)
        v_old = vc_ref[0, pl.ds(h, n_past, stride=BAND_HEADS), :].astype(BF16)
        s = [_dot_nt(q, k_old) + bias_ref[0, h, :t, :n_past],
             _dot_nt(q, kn_ref[0, :, hs]) + bias_ref[0, h, :t, n_past:n_past + t]]
        o_ref[0, :, hs] = _softmax_pv(s, [v_old, vn_ref[0, :, hs]])


def _band_sample(qkv, k_cache, v_cache, bias):
    b, t, _ = qkv.shape
    n_past = k_cache.shape[1] // BAND_HEADS
    new = (1, t, BAND_WIDTH)
    old = (1, n_past * BAND_HEADS, BAND_HEAD_DIM)
    return pl.pallas_call(
        functools.partial(_band_sample_kernel, n_past=n_past),
        grid=(b,),
        in_specs=[pl.BlockSpec(new, lambda bi: (bi, 0, 0)),
                  pl.BlockSpec(new, lambda bi: (bi, 0, 1)),
                  pl.BlockSpec(new, lambda bi: (bi, 0, 2)),
                  pl.BlockSpec(old, lambda bi: (bi, 0, 0)),
                  pl.BlockSpec(old, lambda bi: (bi, 0, 0)),
                  pl.BlockSpec((1,) + bias.shape[1:], lambda bi: (BAND_PIECES - 1, 0, 0, 0))],
        out_specs=pl.BlockSpec(new, lambda bi: (bi, 0, 0)),
        out_shape=jax.ShapeDtypeStruct((b, t, BAND_WIDTH), BF16),
        compiler_params=_params(1), name="band_sample")(qkv, qkv, qkv, k_cache, v_cache, bias)


def _mla_prompt_kernel(q_ref, k_ref, vt_ref, o_ref, mask_ref, *bufs, tq, nq):
    def scores(i, j):
        return _dot_nt(k_ref[0, j * tq:(j + 1) * tq, :], q_ref[0, i * tq:(i + 1) * tq, :])

    kc = lax.broadcasted_iota(jnp.int32, (tq, tq), 0) // CHUNK
    qc = lax.broadcasted_iota(jnp.int32, (tq, tq), 1) // CHUNK
    mask_ref[...] = jnp.where(kc <= qc, 0.0, NEG)

    pairs = [(i, j) for i in range(nq) for j in range(i + 1)]
    ahead = len(bufs) - 1
    for n in range(ahead):
        bufs[n][...] = scores(*pairs[n])
    for n, (i, j) in enumerate(pairs):
        if n + ahead < len(pairs):
            bufs[(n + ahead) % len(bufs)][...] = scores(*pairs[n + ahead])
        s = bufs[n % len(bufs)][...]
        if j == i:
            s = s + mask_ref[...]
        if j == 0:
            m = jnp.max(s, axis=0, keepdims=True)
            p = jnp.exp2(s - m)
            l = jnp.sum(p, axis=0, keepdims=True)
            acc = _dot(vt_ref[0, j], p.astype(BF16))
        else:
            m_new = jnp.maximum(m, jnp.max(s, axis=0, keepdims=True))
            alpha = jnp.exp2(m - m_new)
            p = jnp.exp2(s - m_new)
            l = alpha * l + jnp.sum(p, axis=0, keepdims=True)
            acc = alpha * acc + _dot(vt_ref[0, j], p.astype(BF16))
            m = m_new
        if j == i:
            o_ref[0, i * tq:(i + 1) * tq, :] = (acc / l).T.astype(BF16)


def _mla_prompt(q, k, vt, tq):
    b, s, _ = q.shape
    nq = s // tq
    return pl.pallas_call(
        functools.partial(_mla_prompt_kernel, tq=tq, nq=nq),
        grid=(b, MLA_HEADS),
        in_specs=[pl.BlockSpec((1, s, MLA_QK_PAD), lambda bi, h: (bi, 0, h)),
                  pl.BlockSpec((1, s, MLA_QK_PAD), lambda bi, h: (bi, 0, h)),
                  pl.BlockSpec((1, nq, MLA_V, tq), lambda bi, h: (bi, 0, h, 0))],
        out_specs=pl.BlockSpec((1, s, MLA_V), lambda bi, h: (bi, 0, h)),
        out_shape=jax.ShapeDtypeStruct((b, s, MLA_WIDTH), BF16),
        scratch_shapes=[pltpu.VMEM((tq, tq), F32)] * (1 + MLA_SCORE_BUFFERS),
        compiler_params=_params(2), name="mla_prompt")(q, k, vt)


def _mla_sample_kernel(q_ref, k_ref, v_ref, o_ref):
    s = _dot_nt(q_ref[0], k_ref[0])
    e = jnp.exp2(s - jnp.max(s, axis=-1, keepdims=True))
    o = _dot(e.astype(BF16), v_ref[0])
    o_ref[0] = (o / jnp.sum(e, axis=-1, keepdims=True)).astype(BF16)


def _mla_sample(q, k, v):
    b, t, _ = q.shape
    l = k.shape[1]
    return pl.pallas_call(
        _mla_sample_kernel,
        grid=(b, MLA_HEADS),
        in_specs=[pl.BlockSpec((1, t, MLA_QK_PAD), lambda bi, h: (bi, 0, h)),
                  pl.BlockSpec((1, l, MLA_QK_PAD), lambda bi, h: (bi, 0, h)),
                  pl.BlockSpec((1, l, MLA_V), lambda bi, h: (bi, 0, h))],
        out_specs=pl.BlockSpec((1, t, MLA_V), lambda bi, h: (bi, 0, h)),
        out_shape=jax.ShapeDtypeStruct((b, t, MLA_WIDTH), BF16),
        compiler_params=_params(2), name="mla_sample")(q, k, v)


def _mix_kernel(xn_ref, oa_ref, ob_ref, wga_ref, wgb_ref, wpa_ref, wpb_ref, o_ref):
    xn = xn_ref[...]
    a = jax.nn.sigmoid(_dot(xn, wga_ref[...])) * _dot(oa_ref[...], wpa_ref[...])
    b = jax.nn.sigmoid(_dot(xn, wgb_ref[...])) * _dot(ob_ref[...], wpb_ref[...])
    o_ref[...] = (a + b).astype(BF16)


def _mix(xn, oa, ob, wga, wgb, wpa, wpb, tm, tn=512):
    m = xn.shape[0]
    return pl.pallas_call(
        _mix_kernel,
        grid=(m // tm, D_MODEL // tn),
        in_specs=[pl.BlockSpec((tm, D_MODEL), lambda i, j: (i, 0)),
                  pl.BlockSpec((tm, BAND_WIDTH), lambda i, j: (i, 0)),
                  pl.BlockSpec((tm, MLA_WIDTH), lambda i, j: (i, 0)),
                  pl.BlockSpec((D_MODEL, tn), lambda i, j: (0, j)),
                  pl.BlockSpec((D_MODEL, tn), lambda i, j: (0, j)),
                  pl.BlockSpec((BAND_WIDTH, tn), lambda i, j: (0, j)),
                  pl.BlockSpec((MLA_WIDTH, tn), lambda i, j: (0, j))],
        out_specs=pl.BlockSpec((tm, tn), lambda i, j: (i, j)),
        out_shape=jax.ShapeDtypeStruct((m, D_MODEL), BF16),
        compiler_params=_params(2), name="mix")(xn, oa, ob, wga, wgb, wpa, wpb)


def _outproj_kernel(x_ref, mix_ref, w_ref, h_ref):
    h_ref[...] = x_ref[...] + _dot(mix_ref[...], w_ref[...])


def _outproj(x, mix, w, tm):
    m = x.shape[0]
    rows = pl.BlockSpec((tm, D_MODEL), lambda i: (i, 0))
    return pl.pallas_call(
        _outproj_kernel,
        grid=(m // tm,),
        in_specs=[rows, rows, pl.BlockSpec((D_MODEL, D_MODEL), lambda i: (0, 0))],
        out_specs=rows,
        out_shape=jax.ShapeDtypeStruct((m, D_MODEL), F32),
        compiler_params=_params(1), name="outproj")(x, mix, w)


def _ffn_kernel(h_ref, g_ref, wu_ref, wd_ref, o_ref, hn_ref):
    @pl.when(pl.program_id(1) == 0)
    def _():
        h = h_ref[...]
        hn_ref[...] = _rms(h, g_ref[...]).astype(BF16)
        o_ref[...] = h

    u = jnp.maximum(_dot(hn_ref[...], wu_ref[...]), 0.0)
    o_ref[...] += _dot((u * u).astype(BF16), wd_ref[...])


def _ffn(h, g, wu, wd, tm, tf=512):
    m = h.shape[0]
    return pl.pallas_call(
        _ffn_kernel,
        grid=(m // tm, D_FF // tf),
        in_specs=[pl.BlockSpec((tm, D_MODEL), lambda i, f: (i, 0)),
                  pl.BlockSpec((1, D_MODEL), lambda i, f: (0, 0)),
                  pl.BlockSpec((D_MODEL, tf), lambda i, f: (0, f)),
                  pl.BlockSpec((tf, D_MODEL), lambda i, f: (f, 0))],
        out_specs=pl.BlockSpec((tm, D_MODEL), lambda i, f: (i, 0)),
        out_shape=jax.ShapeDtypeStruct((m, D_MODEL), F32),
        scratch_shapes=[pltpu.VMEM((tm, D_MODEL), BF16)],
        compiler_params=_params(2), name="ffn")(h, g, wu, wd)


def _rope_table(pos, rows):
    half = MLA_ROPE // 2
    freqs = ROPE_THETA ** (-(jnp.arange(half, dtype=F32) / half))
    ang = pos.astype(F32)[:, None] * freqs[None, :]
    cos = jnp.cos(ang)
    sin = jnp.sin(ang)
    table = jnp.concatenate([cos, cos, -sin, sin], axis=1)
    return jnp.tile(table, (rows // table.shape[0], 1))


def _with_partner(w):
    half = MLA_ROPE // 2
    return jnp.concatenate([w, w[..., half:], w[..., :half]], axis=-1)


def _layer_weights(l, norm_mix_g, w_in, g_aq, g_ak, g_kv, g_kr, g_qn, g_qr, g_kn,
                   w_kv_b, w_pa, w_pb, w_out, norm_ffn_g, w_up, w_down):
    w = w_in[l]
    c = 0
    parts = []
    for width in (3 * BAND_WIDTH, MLA_HEADS * MLA_QK, MLA_KV_RANK, MLA_ROPE, D_MODEL, D_MODEL):
        parts.append(w[:, c:c + width])
        c += width
    w_a, w_bq, w_ckv, w_kr, w_ga, w_gb = parts
    w_bq = w_bq.reshape(D_MODEL, MLA_HEADS, MLA_QK)
    w_q = jnp.concatenate([w_bq[..., :MLA_NOPE], _with_partner(w_bq[..., MLA_NOPE:])], axis=-1)
    w_q = w_q.reshape(D_MODEL, MLA_HEADS * MLA_QK_PAD)
    row = lambda g: g[l][None, :].astype(F32)
    return dict(
        gx=row(norm_mix_g),
        w_proj=jnp.concatenate([w_a, w_q, w_ckv], axis=1).astype(BF16),
        w_kr=_with_partner(w_kr).astype(BF16),
        w_kv=w_a[:, BAND_WIDTH:].astype(BF16),
        g_a=jnp.concatenate([jnp.tile(g_aq[l] * BAND_QSCALE, BAND_HEADS), jnp.tile(g_ak[l], BAND_HEADS),
                             jnp.ones((BAND_WIDTH,), F32)])[None, :],
        g_ak=row(g_ak),
        g_qn=row(g_qn),
        g_qr=_with_partner(g_qr[l])[None, :],
        g_kv=row(g_kv),
        g_kr=_with_partner(g_kr[l])[None, :],
        w_kv_b=w_kv_b[l].astype(BF16),
        g_kn=row(g_kn),
        w_ga=w_ga.astype(BF16), w_gb=w_gb.astype(BF16),
        w_pa=w_pa[l].astype(BF16), w_pb=w_pb[l].astype(BF16),
        w_out=w_out[l].astype(BF16),
        g_ffn=row(norm_ffn_g),
        w_up=w_up[l].astype(BF16), w_down=w_down[l].astype(BF16),
    )


def _merge_ffn(x2, xn, oa, ob, p, tm):
    mix = _mix(xn, oa, ob, p["w_ga"], p["w_gb"], p["w_pa"], p["w_pb"], tm)
    h = _outproj(x2, mix, p["w_out"], min(tm, 512))
    return _ffn(h, p["g_ffn"], p["w_up"], p["w_down"], tm)


def _cache_shape(kv, lead):
    return tuple(kv[n].reshape(lead + (BAND_HEADS, BAND_HEAD_DIM)) for n in range(2))


def kernel(x_prompt, x_sample, cache_a_k, cache_a_v, cache_mla_ckv, cache_mla_krope, norm_mix_g, w_in, g_aq, g_ak, rel_bias, g_kv, g_kr, g_qn, g_qr, g_kn, w_kv_b, w_pa, w_pb, w_out, norm_ffn_g, w_up, w_down):
    b, s, _ = x_prompt.shape
    bs, t, _ = x_sample.shape
    past = cache_mla_ckv.shape[2]
    n_band = cache_a_k.shape[2]
    depth = w_in.shape[0]
    keep = min(BAND_PAST, s)
    tm_p = 1024
    tq_mla = 512
    tm_s = bs * t
    assert s % tm_p == 0 and s % BAND_TQ == 0 and s % tq_mla == 0 and t == CHUNK and past % CHUNK == 0
    assert (b * keep) % tm_s == 0
    cs_p = _rope_table(jnp.arange(s), max(s, tm_p))
    cs_s = _rope_table(past + jnp.arange(t), tm_s)

    yp = x_prompt.reshape(b * s, D_MODEL)
    ys = x_sample.reshape(bs * t, D_MODEL)
    outs = [[] for _ in range(8)]
    for l in range(depth):
        p = _layer_weights(l, norm_mix_g, w_in, g_aq, g_ak, g_kv, g_kr, g_qn, g_qr, g_kn,
                           w_kv_b, w_pa, w_pb, w_out, norm_ffn_g, w_up, w_down)
        bias, bias_t = _band_bias(rel_bias[l])

        xn, qkv, vt, q_mla, ckv, ckv_bf, krr, kr = _proj(yp, p, cs_p, tm_p)
        x_keep = yp.reshape(b, s, D_MODEL)[:, s - keep:].reshape(b * keep, D_MODEL)
        ak, av = _cache_shape(_kv_f32(x_keep, p["gx"], p["w_kv"], p["g_ak"], tm_s), (b, keep))
        oa = _band_prompt(qkv.reshape(b, s, 3 * BAND_WIDTH), vt, bias_t)
        k_mla, vt_mla = _expand(ckv_bf, p["w_kv_b"], p["g_kn"], krr, tq_mla, True)
        ob = _mla_prompt(q_mla.reshape(b, s, -1), k_mla.reshape(b, s, -1),
                         vt_mla.reshape(b, s // tq_mla, MLA_WIDTH, tq_mla), tq_mla)
        for o, val in zip(outs[:4], (ak, av, ckv.reshape(b, s, MLA_KV_RANK), kr.reshape(b, s, MLA_ROPE))):
            o.append(val)
        yp = _merge_ffn(yp, xn, oa.reshape(b * s, BAND_WIDTH), ob.reshape(b * s, MLA_WIDTH), p, tm_p)

        xn, qkv, _, q_mla, ckv, ckv_bf, krr, kr = _proj(ys, p, cs_s, tm_s)
        ak, av = _cache_shape(_kv_f32(ys, p["gx"], p["w_kv"], p["g_ak"], tm_s), (bs, t))
        oa = _band_sample(qkv.reshape(bs, t, 3 * BAND_WIDTH),
                          cache_a_k[l].reshape(bs, n_band * BAND_HEADS, BAND_HEAD_DIM),
                          cache_a_v[l].reshape(bs, n_band * BAND_HEADS, BAND_HEAD_DIM), bias)
        ckv_all = jnp.concatenate([cache_mla_ckv[l].astype(BF16), ckv_bf.reshape(bs, t, MLA_KV_RANK)], axis=1)
        kr_old = cache_mla_krope[l].astype(F32)
        krr_all = jnp.concatenate([jnp.concatenate([kr_old, kr_old], axis=-1), krr.reshape(bs, t, LANES)], axis=1)
        n_all = past + t
        k_mla, v_mla = _expand(ckv_all.reshape(bs * n_all, MLA_KV_RANK), p["w_kv_b"], p["g_kn"],
                               krr_all.reshape(bs * n_all, LANES), n_all // 4, False)
        ob = _mla_sample(q_mla.reshape(bs, t, -1), k_mla.reshape(bs, n_all, -1), v_mla.reshape(bs, n_all, -1))
        for o, val in zip(outs[4:], (ak, av, ckv.reshape(bs, t, MLA_KV_RANK), kr.reshape(bs, t, MLA_ROPE))):
            o.append(val)
        ys = _merge_ffn(ys, xn, oa.reshape(bs * t, BAND_WIDTH), ob.reshape(bs * t, MLA_WIDTH), p, tm_s)

    return (yp.reshape(b, s, D_MODEL), ys.reshape(bs, t, D_MODEL)) + tuple(jnp.stack(o) for o in outs)
```

```python
import functools

import jax
import jax.numpy as jnp
from jax import lax
from jax.experimental import pallas as pl
from jax.experimental.pallas import tpu as pltpu

F32 = jnp.float32
BF16 = jnp.bfloat16

D_MODEL = 2048
CHUNK = 64
BAND_PAST = 8 * CHUNK
BAND_HEADS = 8
BAND_HEAD_DIM = 128
BAND_WIDTH = BAND_HEADS * BAND_HEAD_DIM
REL_MAX = 256
REL_SIZE = (CHUNK - 1) + REL_MAX + 1
BAND_SCALE = BAND_HEAD_DIM ** -0.5
MLA_HEADS = 8
MLA_NOPE = 128
MLA_ROPE = 64
MLA_QK = MLA_NOPE + MLA_ROPE
MLA_V = 128
MLA_WIDTH = MLA_HEADS * MLA_V
MLA_KV_RANK = 512
MLA_SCALE = MLA_QK ** -0.5
ROPE_THETA = 10000.0
D_FF = 4 * D_MODEL
EPS = 1e-6
NEG = -1e30
LOG2E = 1.4426950408889634
MLA_QSCALE = MLA_SCALE * LOG2E
BAND_QSCALE = BAND_SCALE * LOG2E

LANES = 128
MLA_QK_PAD = 2 * LANES
BAND_TQ = 256
BAND_WIN = BAND_TQ + BAND_PAST
BAND_PIECES = BAND_WIN // BAND_TQ
VMEM_LIMIT = 56 * 1024 * 1024
PROJ_TN = 512
SUM_ROWS = 16
BAND_SCORE_BUFFERS = 3
MLA_SCORE_BUFFERS = 3


def _params(n_axes, vmem=VMEM_LIMIT):
    return pltpu.CompilerParams(dimension_semantics=("arbitrary",) * n_axes,
                                vmem_limit_bytes=vmem)


def _rms(a, g):
    return a * lax.rsqrt(jnp.mean(a * a, axis=-1, keepdims=True) + EPS) * g


def _dot(a, b):
    return jnp.dot(a, b, preferred_element_type=F32)


def _dot_nt(a, b):
    return lax.dot_general(a, b, (((1,), (1,)), ((), ())), preferred_element_type=F32)


def _rope_pair(ar, g, cs):
    lane = lax.broadcasted_iota(jnp.int32, (1, LANES), 1)
    first = (lane < MLA_ROPE).astype(F32)
    ss = jnp.sum(ar * ar * first, axis=-1, keepdims=True) / MLA_ROPE
    return ar * lax.rsqrt(ss + EPS) * g * cs


_NB_BAND_NORM = 2 * BAND_WIDTH // PROJ_TN
_NB_BAND = 3 * BAND_WIDTH // PROJ_TN
_NB_MLA_Q = MLA_HEADS * MLA_QK_PAD // PROJ_TN
_NB_PROJ = _NB_BAND + _NB_MLA_Q + 1
_CKV_BLOCK = (3 * BAND_WIDTH + MLA_HEADS * MLA_QK) // PROJ_TN
assert MLA_KV_RANK == PROJ_TN and _CKV_BLOCK * PROJ_TN == 3 * BAND_WIDTH + MLA_HEADS * MLA_QK


def _proj_kernel(x_ref, gx_ref, w_ref, wq_ref, wkr_ref, gh_ref, gqn_ref, gqr_ref, gkv_ref, gkr_ref, cs_ref,
                 xn_out_ref, qkv_ref, vt_ref, q_ref, ckv_ref, ckvb_ref, krr_ref, kr_ref, xn_ref):
    j = pl.program_id(1)

    @pl.when(j == 0)
    def _():
        xn = _rms(x_ref[...], gx_ref[...]).astype(BF16)
        xn_ref[...] = xn
        xn_out_ref[...] = xn

    def block_product(weights=w_ref):
        return _dot(xn_ref[...], weights[...])

    @pl.when(j < _NB_BAND_NORM)
    def _():
        acc = block_product()
        for k in range(PROJ_TN // LANES):
            sl = slice(k * LANES, (k + 1) * LANES)
            qkv_ref[:, sl] = _rms(acc[:, sl], gh_ref[:, sl]).astype(BF16)

    @pl.when(jnp.logical_and(j >= _NB_BAND_NORM, j < _NB_BAND))
    def _():
        acc = block_product()
        qkv_ref[...] = acc.astype(BF16)
        vt_ref[...] = acc.T.astype(BF16)

    @pl.when(jnp.logical_and(j >= _NB_BAND, j < _NB_BAND + _NB_MLA_Q))
    def _():
        acc = block_product(wq_ref)
        cs = cs_ref[...]
        for k in range(PROJ_TN // MLA_QK_PAD):
            c0 = k * MLA_QK_PAD
            qn = _rms(acc[:, c0:c0 + LANES], gqn_ref[...]) * MLA_QSCALE
            qr = _rope_pair(acc[:, c0 + LANES:c0 + 2 * LANES], gqr_ref[...], cs) * MLA_QSCALE
            q_ref[:, c0:c0 + LANES] = qn.astype(BF16)
            q_ref[:, c0 + LANES:c0 + 2 * LANES] = qr.astype(BF16)

    @pl.when(j == _NB_BAND + _NB_MLA_Q)
    def _():
        cn = _rms(block_product(), gkv_ref[...])
        ckv_ref[...] = cn
        ckvb_ref[...] = cn.astype(BF16)
        t = _rope_pair(_dot(xn_ref[...], wkr_ref[...]), gkr_ref[...], cs_ref[...])
        krr = t + pltpu.roll(t, MLA_ROPE, 1)
        krr_ref[...] = krr
        kr_ref[...] = krr[:, :MLA_ROPE]


def _proj(x, p, cs, tm):
    m = x.shape[0]
    tn = PROJ_TN
    n_cs = cs.shape[0] // tm
    const = lambda i, j: (0, 0)
    rows = lambda i, j: (i, 0)
    return pl.pallas_call(
        _proj_kernel,
        grid=(m // tm, _NB_PROJ),
        in_specs=[pl.BlockSpec((tm, D_MODEL), rows),
                  pl.BlockSpec((1, D_MODEL), const),
                  pl.BlockSpec((D_MODEL, tn), lambda i, j: (0, jnp.where(
                      j < _NB_BAND, j, jnp.where(j == _NB_PROJ - 1, _CKV_BLOCK, _NB_BAND - 1)))),
                  pl.BlockSpec((D_MODEL, tn), lambda i, j: (0, jnp.clip(j - _NB_BAND, 0, _NB_MLA_Q - 1))),
                  pl.BlockSpec((D_MODEL, LANES), const),
                  pl.BlockSpec((1, tn), lambda i, j: (0, jnp.minimum(j, _NB_BAND - 1))),
                  pl.BlockSpec((1, LANES), const),
                  pl.BlockSpec((1, LANES), const),
                  pl.BlockSpec((1, MLA_KV_RANK), const),
                  pl.BlockSpec((1, LANES), const),
                  pl.BlockSpec((tm, LANES), lambda i, j: (i % n_cs, 0))],
        out_specs=[pl.BlockSpec((tm, D_MODEL), rows),
                   pl.BlockSpec((tm, tn), lambda i, j: (i, jnp.minimum(j, _NB_BAND - 1))),
                   pl.BlockSpec((tn, tm), lambda i, j: (jnp.clip(j - _NB_BAND_NORM, 0, _NB_BAND - _NB_BAND_NORM - 1), i)),
                   pl.BlockSpec((tm, tn), lambda i, j: (i, jnp.clip(j - _NB_BAND, 0, _NB_MLA_Q - 1))),
                   pl.BlockSpec((tm, MLA_KV_RANK), rows),
                   pl.BlockSpec((tm, MLA_KV_RANK), rows),
                   pl.BlockSpec((tm, LANES), rows),
                   pl.BlockSpec((tm, MLA_ROPE), rows)],
        out_shape=[jax.ShapeDtypeStruct((m, D_MODEL), BF16),
                   jax.ShapeDtypeStruct((m, 3 * BAND_WIDTH), BF16),
                   jax.ShapeDtypeStruct((BAND_WIDTH, m), BF16),
                   jax.ShapeDtypeStruct((m, MLA_HEADS * MLA_QK_PAD), BF16),
                   jax.ShapeDtypeStruct((m, MLA_KV_RANK), F32),
                   jax.ShapeDtypeStruct((m, MLA_KV_RANK), BF16),
                   jax.ShapeDtypeStruct((m, LANES), F32),
                   jax.ShapeDtypeStruct((m, MLA_ROPE), F32)],
        scratch_shapes=[pltpu.VMEM((tm, D_MODEL), BF16)],
        compiler_params=_params(2, 62 * 1024 * 1024), name="proj")(
            x, p["gx"], p["w_in"], p["w_q"], p["w_kr"], p["g_a"], p["g_qn"], p["g_qr"], p["g_kv"], p["g_kr"], cs)


def _kv_f32_kernel(x_ref, gx_ref, w_ref, g_ref, o_ref, xn_ref):
    j = pl.program_id(1)
    tm = x_ref.shape[0]

    @pl.when(j == 0)
    def _():
        xn_ref[...] = _rms(x_ref[...], gx_ref[...]).astype(BF16)

    def put(h, y):
        o_ref[0, pl.ds(h, tm, stride=BAND_HEADS), :] = y

    @pl.when(j == 0)
    def _():
        acc = _dot(xn_ref[...], w_ref[...])
        for h in range(BAND_HEADS):
            put(h, _rms(acc[:, h * BAND_HEAD_DIM:(h + 1) * BAND_HEAD_DIM], g_ref[...]))

    @pl.when(j == 1)
    def _():
        acc = _dot(xn_ref[...], w_ref[...])
        for h in range(BAND_HEADS):
            put(h, acc[:, h * BAND_HEAD_DIM:(h + 1) * BAND_HEAD_DIM])


def _kv_f32(x, gx, w_in, g, tm, n_rows, row_block):
    return pl.pallas_call(
        _kv_f32_kernel,
        grid=(n_rows // tm, 2),
        in_specs=[pl.BlockSpec((tm, D_MODEL), lambda i, j: (row_block(i), 0)),
                  pl.BlockSpec((1, D_MODEL), lambda i, j: (0, 0)),
                  pl.BlockSpec((D_MODEL, BAND_WIDTH), lambda i, j: (0, j + 1)),
                  pl.BlockSpec((1, LANES), lambda i, j: (0, 0))],
        out_specs=pl.BlockSpec((1, tm * BAND_HEADS, BAND_HEAD_DIM), lambda i, j: (j, i, 0)),
        out_shape=jax.ShapeDtypeStruct((2, n_rows * BAND_HEADS, BAND_HEAD_DIM), F32),
        scratch_shapes=[pltpu.VMEM((tm, D_MODEL), BF16)],
        compiler_params=_params(2), name="kv_f32")(x, gx, w_in, g)


def _expand_kernel(c_ref, w_ref, gkn_ref, krr_ref, k_ref, v_ref, *, v_transposed):
    c = c_ref[...].astype(BF16)
    krr = krr_ref[...].astype(BF16)
    if krr.shape[1] == MLA_ROPE:
        krr = jnp.concatenate([krr, krr], axis=1)
    for h in range(MLA_HEADS):
        a = _dot(c, w_ref[:, h * 2 * LANES:(h + 1) * 2 * LANES])
        k_ref[:, h * MLA_QK_PAD:h * MLA_QK_PAD + LANES] = _rms(a[:, :LANES], gkn_ref[...]).astype(BF16)
        k_ref[:, h * MLA_QK_PAD + LANES:(h + 1) * MLA_QK_PAD] = krr
        if v_transposed:
            v_ref[0, h * MLA_V:(h + 1) * MLA_V, :] = a[:, LANES:].T.astype(BF16)
        else:
            v_ref[:, h * MLA_V:(h + 1) * MLA_V] = a[:, LANES:].astype(BF16)


def _expand(c, w, gkn, krr, tm, v_transposed):
    m = c.shape[0]
    if v_transposed:
        v_spec = pl.BlockSpec((1, MLA_WIDTH, tm), lambda i: (i, 0, 0))
        v_shape = jax.ShapeDtypeStruct((m // tm, MLA_WIDTH, tm), BF16)
    else:
        v_spec = pl.BlockSpec((tm, MLA_WIDTH), lambda i: (i, 0))
        v_shape = jax.ShapeDtypeStruct((m, MLA_WIDTH), BF16)
    return pl.pallas_call(
        functools.partial(_expand_kernel, v_transposed=v_transposed),
        grid=(m // tm,),
        in_specs=[pl.BlockSpec((tm, MLA_KV_RANK), lambda i: (i, 0)),
                  pl.BlockSpec(w.shape, lambda i: (0, 0)),
                  pl.BlockSpec((1, LANES), lambda i: (0, 0)),
                  pl.BlockSpec((tm, krr.shape[1]), lambda i: (i, 0))],
        out_specs=[pl.BlockSpec((tm, MLA_HEADS * MLA_QK_PAD), lambda i: (i, 0)), v_spec],
        out_shape=[jax.ShapeDtypeStruct((m, MLA_HEADS * MLA_QK_PAD), BF16), v_shape],
        compiler_params=_params(1), name="expand")(c, w, gkn, krr)


def _band_bias_kernel(b_ref, o_ref, ot_ref):
    n_missing = BAND_PIECES - 1 - pl.program_id(0)
    t = pltpu.roll(jnp.broadcast_to(b_ref[0], (BAND_TQ, 2 * BAND_PAST)), 0, 1,
                   stride=1, stride_axis=0)[:, :BAND_WIN] * LOG2E
    col = lax.broadcasted_iota(jnp.int32, (BAND_TQ, BAND_WIN), 1)
    qc = lax.broadcasted_iota(jnp.int32, (BAND_TQ, BAND_WIN), 0) // CHUNK
    kc = col // CHUNK
    t = jnp.where(kc >= qc, jnp.where(kc <= qc + BAND_PAST // CHUNK, t, NEG), NEG)
    t = jnp.where(col >= n_missing * BAND_TQ, t, NEG)
    o_ref[0, 0] = t
    ot_ref[0, 0] = t.T


def _band_bias(rel_bias):
    far = jnp.broadcast_to(rel_bias[:, REL_SIZE - 1:], (BAND_HEADS, BAND_PAST - REL_MAX))
    near = jnp.broadcast_to(rel_bias[:, :1], (BAND_HEADS, BAND_WIN - BAND_PAST - CHUNK))
    wrap = jnp.broadcast_to(rel_bias[:, REL_SIZE - 1:], (BAND_HEADS, 2 * BAND_PAST - BAND_WIN))
    row0 = jnp.concatenate([far, rel_bias[:, ::-1], near, wrap], axis=1)
    assert row0.shape == (BAND_HEADS, 2 * BAND_PAST)
    return pl.pallas_call(
        _band_bias_kernel,
        grid=(BAND_PIECES, BAND_HEADS),
        in_specs=[pl.BlockSpec((1, 1, 2 * BAND_PAST), lambda v, h: (h, 0, 0))],
        out_specs=[pl.BlockSpec((1, 1, BAND_TQ, BAND_WIN), lambda v, h: (v, h, 0, 0)),
                   pl.BlockSpec((1, 1, BAND_WIN, BAND_TQ), lambda v, h: (v, h, 0, 0))],
        out_shape=[jax.ShapeDtypeStruct((BAND_PIECES, BAND_HEADS, BAND_TQ, BAND_WIN), F32),
                   jax.ShapeDtypeStruct((BAND_PIECES, BAND_HEADS, BAND_WIN, BAND_TQ), F32)],
        compiler_params=_params(2), name="band_bias")(row0[:, None, :])


def _softmax_pv(s, v):
    m = functools.reduce(jnp.maximum, [jnp.max(sp, axis=-1, keepdims=True) for sp in s])
    e = [jnp.exp2(sp - m) for sp in s]
    l = functools.reduce(jnp.add, [jnp.sum(ep, axis=-1, keepdims=True) for ep in e])
    o = functools.reduce(jnp.add, [_dot(ep.astype(BF16), vp) for ep, vp in zip(e, v)])
    return (o / l).astype(BF16)


def _band_prompt_kernel(q_ref, ka_ref, kb_ref, kc_ref, va_ref, vb_ref, vc_ref, bias_ref, o_ref, *bufs):
    def head(h):
        return slice(h * BAND_HEAD_DIM, (h + 1) * BAND_HEAD_DIM)

    def scores(h):
        k = jnp.concatenate([r[0, :, head(h)] for r in (ka_ref, kb_ref, kc_ref)], axis=0)
        return _dot_nt(k, q_ref[0, :, head(h)]) + bias_ref[0, h]

    ahead = len(bufs) - 1
    for h in range(ahead):
        bufs[h][...] = scores(h)
    for h in range(BAND_HEADS):
        if h + ahead < BAND_HEADS:
            bufs[(h + ahead) % len(bufs)][...] = scores(h + ahead)
        s = bufs[h % len(bufs)][...]
        vt = jnp.concatenate([r[head(h), :] for r in (va_ref, vb_ref, vc_ref)], axis=1)
        vt = jnp.concatenate([vt, jnp.ones((SUM_ROWS, BAND_WIN), BF16)], axis=0)
        e = jnp.exp2(s - jnp.max(s, axis=0, keepdims=True))
        o = _dot(vt, e.astype(BF16))
        o = o[:BAND_HEAD_DIM] / o[BAND_HEAD_DIM:BAND_HEAD_DIM + 1]
        o_ref[0, :, head(h)] = o.T.astype(BF16)


def _band_prompt(qkv, vt, bias_t):
    b, s, _ = qkv.shape
    nq = s // BAND_TQ
    blk = (1, BAND_TQ, BAND_WIDTH)

    def k_spec(back):
        return pl.BlockSpec(blk, lambda bi, i: (bi, jnp.maximum(i - back, 0), 1))

    def vt_spec(back):
        return pl.BlockSpec((BAND_WIDTH, BAND_TQ), lambda bi, i: (0, bi * nq + jnp.maximum(i - back, 0)))

    return pl.pallas_call(
        _band_prompt_kernel,
        grid=(b, nq),
        in_specs=[pl.BlockSpec(blk, lambda bi, i: (bi, i, 0))]
                 + [k_spec(BAND_PIECES - 1 - p) for p in range(BAND_PIECES)]
                 + [vt_spec(BAND_PIECES - 1 - p) for p in range(BAND_PIECES)]
                 + [pl.BlockSpec((1,) + bias_t.shape[1:],
                                 lambda bi, i: (jnp.minimum(i, BAND_PIECES - 1), 0, 0, 0))],
        out_specs=pl.BlockSpec(blk, lambda bi, i: (bi, i, 0)),
        out_shape=jax.ShapeDtypeStruct((b, s, BAND_WIDTH), BF16),
        scratch_shapes=[pltpu.VMEM((BAND_WIN, BAND_TQ), F32)] * BAND_SCORE_BUFFERS,
        compiler_params=_params(2), name="band_prompt")(qkv, qkv, qkv, qkv, vt, vt, vt, bias_t)


def _band_sample_kernel(q_ref, kn_ref, vn_ref, kc_ref, vc_ref, bias_ref, o_ref, *, n_past):
    t = q_ref.shape[1]
    for h in range(BAND_HEADS):
        hs = slice(h * BAND_HEAD_DIM, (h + 1) * BAND_HEAD_DIM)
        q = q_ref[0, :, hs]
        k_old = kc_ref[0, pl.ds(h, n_past, stride=BAND_HEADS), :].astype(BF16)
        v_old = vc_ref[0, pl.ds(h, n_past, stride=BAND_HEADS), :].astype(BF16)
        s = [_dot_nt(q, k_old) + bias_ref[0, h, :t, :n_past],
             _dot_nt(q, kn_ref[0, :, hs]) + bias_ref[0, h, :t, n_past:n_past + t]]
        o_ref[0, :, hs] = _softmax_pv(s, [v_old, vn_ref[0, :, hs]])


def _band_sample(qkv, k_cache, v_cache, bias):
    b, t, _ = qkv.shape
    n_past = k_cache.shape[1] // BAND_HEADS
    new = (1, t, BAND_WIDTH)
    old = (1, n_past * BAND_HEADS, BAND_HEAD_DIM)
    return pl.pallas_call(
        functools.partial(_band_sample_kernel, n_past=n_past),
        grid=(b,),
        in_specs=[pl.BlockSpec(new, lambda bi: (bi, 0, 0)),
                  pl.BlockSpec(new, lambda bi: (bi, 0, 1)),
                  pl.BlockSpec(new, lambda bi: (bi, 0, 2)),
                  pl.BlockSpec(old, lambda bi: (bi, 0, 0)),
                  pl.BlockSpec(old, lambda bi: (bi, 0, 0)),
                  pl.BlockSpec((1,) + bias.shape[1:], lambda bi: (BAND_PIECES - 1, 0, 0, 0))],
        out_specs=pl.BlockSpec(new, lambda bi: (bi, 0, 0)),
        out_shape=jax.ShapeDtypeStruct((b, t, BAND_WIDTH), BF16),
        compiler_params=_params(1), name="band_sample")(qkv, qkv, qkv, k_cache, v_cache, bias)


def _mla_prompt_kernel(q_ref, k_ref, vt_ref, o_ref, mask_ref, *bufs, tq, nq):
    def scores(i, j):
        return _dot_nt(k_ref[0, j * tq:(j + 1) * tq, :], q_ref[0, i * tq:(i + 1) * tq, :])

    kc = lax.broadcasted_iota(jnp.int32, (tq, tq), 0) // CHUNK
    qc = lax.broadcasted_iota(jnp.int32, (tq, tq), 1) // CHUNK
    mask_ref[...] = jnp.where(kc <= qc, 0.0, NEG)

    pairs = [(i, j) for i in range(nq) for j in range(i + 1)]
    ahead = len(bufs) - 1
    for n in range(ahead):
        bufs[n][...] = scores(*pairs[n])
    for n, (i, j) in enumerate(pairs):
        if n + ahead < len(pairs):
            bufs[(n + ahead) % len(bufs)][...] = scores(*pairs[n + ahead])
        s = bufs[n % len(bufs)][...]
        if j == i:
            s = s + mask_ref[...]
        vt = jnp.concatenate([vt_ref[0, j], jnp.ones((SUM_ROWS, tq), BF16)], axis=0)
        if j == 0:
            m = jnp.max(s, axis=0, keepdims=True)
            acc = _dot(vt, jnp.exp2(s - m).astype(BF16))
        else:
            m_new = jnp.maximum(m, jnp.max(s, axis=0, keepdims=True))
            acc = jnp.exp2(m - m_new) * acc + _dot(vt, jnp.exp2(s - m_new).astype(BF16))
            m = m_new
        if j == i:
            o_ref[0, i * tq:(i + 1) * tq, :] = (acc[:MLA_V] / acc[MLA_V:MLA_V + 1]).T.astype(BF16)


def _mla_prompt(q, k, vt, tq):
    b, s, _ = q.shape
    nq = s // tq
    return pl.pallas_call(
        functools.partial(_mla_prompt_kernel, tq=tq, nq=nq),
        grid=(b, MLA_HEADS),
        in_specs=[pl.BlockSpec((1, s, MLA_QK_PAD), lambda bi, h: (bi, 0, h)),
                  pl.BlockSpec((1, s, MLA_QK_PAD), lambda bi, h: (bi, 0, h)),
                  pl.BlockSpec((1, nq, MLA_V, tq), lambda bi, h: (bi, 0, h, 0))],
        out_specs=pl.BlockSpec((1, s, MLA_V), lambda bi, h: (bi, 0, h)),
        out_shape=jax.ShapeDtypeStruct((b, s, MLA_WIDTH), BF16),
        scratch_shapes=[pltpu.VMEM((tq, tq), F32)] * (1 + MLA_SCORE_BUFFERS),
        compiler_params=_params(2), name="mla_prompt")(q, k, vt)


def _mla_sample_kernel(q_ref, ko_ref, vo_ref, kn_ref, vn_ref, o_ref):
    q = q_ref[0]
    s = [_dot_nt(q, ko_ref[0]), _dot_nt(q, kn_ref[0])]
    o_ref[0] = _softmax_pv(s, [vo_ref[0], vn_ref[0]])


def _mla_sample(q, k_old, v_old, k_new, v_new):
    b, t, _ = q.shape

    def kv_specs(k):
        n = k.shape[1]
        return [pl.BlockSpec((1, n, MLA_QK_PAD), lambda bi, h: (bi, 0, h)),
                pl.BlockSpec((1, n, MLA_V), lambda bi, h: (bi, 0, h))]

    return pl.pallas_call(
        _mla_sample_kernel,
        grid=(b, MLA_HEADS),
        in_specs=[pl.BlockSpec((1, t, MLA_QK_PAD), lambda bi, h: (bi, 0, h))] + kv_specs(k_old) + kv_specs(k_new),
        out_specs=pl.BlockSpec((1, t, MLA_V), lambda bi, h: (bi, 0, h)),
        out_shape=jax.ShapeDtypeStruct((b, t, MLA_WIDTH), BF16),
        compiler_params=_params(2), name="mla_sample")(q, k_old, v_old, k_new, v_new)


def _mix_kernel(xn_ref, oa_ref, ob_ref, wga_ref, wgb_ref, wpa_ref, wpb_ref, o_ref):
    xn = xn_ref[...]
    a = jax.nn.sigmoid(_dot(xn, wga_ref[...])) * _dot(oa_ref[...], wpa_ref[...])
    b = jax.nn.sigmoid(_dot(xn, wgb_ref[...])) * _dot(ob_ref[...], wpb_ref[...])
    o_ref[...] = (a + b).astype(BF16)


def _mix(xn, oa, ob, wga, wgb, wpa, wpb, tm, tn=1024):
    m = xn.shape[0]
    return pl.pallas_call(
        _mix_kernel,
        grid=(m // tm, D_MODEL // tn),
        in_specs=[pl.BlockSpec((tm, D_MODEL), lambda i, j: (i, 0)),
                  pl.BlockSpec((tm, BAND_WIDTH), lambda i, j: (i, 0)),
                  pl.BlockSpec((tm, MLA_WIDTH), lambda i, j: (i, 0)),
                  pl.BlockSpec((D_MODEL, tn), lambda i, j: (0, j)),
                  pl.BlockSpec((D_MODEL, tn), lambda i, j: (0, j)),
                  pl.BlockSpec((BAND_WIDTH, tn), lambda i, j: (0, j)),
                  pl.BlockSpec((MLA_WIDTH, tn), lambda i, j: (0, j))],
        out_specs=pl.BlockSpec((tm, tn), lambda i, j: (i, j)),
        out_shape=jax.ShapeDtypeStruct((m, D_MODEL), BF16),
        compiler_params=_params(2, 62 * 1024 * 1024), name="mix")(xn, oa, ob, wga, wgb, wpa, wpb)


def _outproj_kernel(x_ref, mix_ref, w_ref, h_ref):
    h_ref[...] = x_ref[...] + _dot(mix_ref[...], w_ref[...])


def _outproj(x, mix, w, tm):
    m = x.shape[0]
    rows = pl.BlockSpec((tm, D_MODEL), lambda i: (i, 0))
    return pl.pallas_call(
        _outproj_kernel,
        grid=(m // tm,),
        in_specs=[rows, rows, pl.BlockSpec((D_MODEL, D_MODEL), lambda i: (0, 0))],
        out_specs=rows,
        out_shape=jax.ShapeDtypeStruct((m, D_MODEL), F32),
        compiler_params=_params(1), name="outproj")(x, mix, w)


def _ffn_kernel(h_ref, g_ref, wu_ref, wd_ref, o_ref, hn_ref):
    @pl.when(pl.program_id(1) == 0)
    def _():
        h = h_ref[...]
        hn_ref[...] = _rms(h, g_ref[...]).astype(BF16)
        o_ref[...] = h

    u = jnp.maximum(_dot(hn_ref[...], wu_ref[...]), 0.0)
    o_ref[...] += _dot((u * u).astype(BF16), wd_ref[...])


def _ffn(h, g, wu, wd, tm, tf=1024):
    m = h.shape[0]
    return pl.pallas_call(
        _ffn_kernel,
        grid=(m // tm, D_FF // tf),
        in_specs=[pl.BlockSpec((tm, D_MODEL), lambda i, f: (i, 0)),
                  pl.BlockSpec((1, D_MODEL), lambda i, f: (0, 0)),
                  pl.BlockSpec((D_MODEL, tf), lambda i, f: (0, f)),
                  pl.BlockSpec((tf, D_MODEL), lambda i, f: (f, 0))],
        out_specs=pl.BlockSpec((tm, D_MODEL), lambda i, f: (i, 0)),
        out_shape=jax.ShapeDtypeStruct((m, D_MODEL), F32),
        scratch_shapes=[pltpu.VMEM((tm, D_MODEL), BF16)],
        compiler_params=_params(2, 62 * 1024 * 1024), name="ffn")(h, g, wu, wd)


def _rope_table(pos, rows):
    half = MLA_ROPE // 2
    freqs = ROPE_THETA ** (-(jnp.arange(half, dtype=F32) / half))
    ang = pos.astype(F32)[:, None] * freqs[None, :]
    cos = jnp.cos(ang)
    sin = jnp.sin(ang)
    table = jnp.concatenate([cos, cos, -sin, sin], axis=1)
    return jnp.tile(table, (rows // table.shape[0], 1))


def _with_partner(w):
    half = MLA_ROPE // 2
    return jnp.concatenate([w, w[..., half:], w[..., :half]], axis=-1)


def _layer_weights(l, norm_mix_g, w_in, g_aq, g_ak, g_kv, g_kr, g_qn, g_qr, g_kn,
                   w_kv_b, w_pa, w_pb, w_out, norm_ffn_g, w_up, w_down):
    w = w_in[l].astype(BF16)
    c = 0
    parts = []
    for width in (3 * BAND_WIDTH, MLA_HEADS * MLA_QK, MLA_KV_RANK, MLA_ROPE, D_MODEL, D_MODEL):
        parts.append(w[:, c:c + width])
        c += width
    w_a, w_bq, w_ckv, w_kr, w_ga, w_gb = parts
    w_bq = w_bq.reshape(D_MODEL, MLA_HEADS, MLA_QK)
    w_q = jnp.concatenate([w_bq[..., :MLA_NOPE], _with_partner(w_bq[..., MLA_NOPE:])], axis=-1)
    w_q = w_q.reshape(D_MODEL, MLA_HEADS * MLA_QK_PAD)
    row = lambda g: g[l][None, :].astype(F32)
    return dict(
        gx=row(norm_mix_g),
        w_in=w,
        w_q=w_q,
        w_kr=_with_partner(w_kr),
        g_a=jnp.concatenate([jnp.tile(g_aq[l] * BAND_QSCALE, BAND_HEADS), jnp.tile(g_ak[l], BAND_HEADS),
                             jnp.ones((BAND_WIDTH,), F32)])[None, :],
        g_ak=row(g_ak),
        g_qn=row(g_qn),
        g_qr=_with_partner(g_qr[l])[None, :],
        g_kv=row(g_kv),
        g_kr=_with_partner(g_kr[l])[None, :],
        w_kv_b=w_kv_b[l].astype(BF16),
        g_kn=row(g_kn),
        w_ga=w_ga, w_gb=w_gb,
        w_pa=w_pa[l].astype(BF16), w_pb=w_pb[l].astype(BF16),
        w_out=w_out[l].astype(BF16),
        g_ffn=row(norm_ffn_g),
        w_up=w_up[l].astype(BF16), w_down=w_down[l].astype(BF16),
    )


def _merge_ffn(x2, xn, oa, ob, p, tm):
    mix = _mix(xn, oa, ob, p["w_ga"], p["w_gb"], p["w_pa"], p["w_pb"], tm)
    h = _outproj(x2, mix, p["w_out"], min(tm, 512))
    return _ffn(h, p["g_ffn"], p["w_up"], p["w_down"], tm)


def _cache_shape(kv, lead):
    return tuple(kv[n].reshape(lead + (BAND_HEADS, BAND_HEAD_DIM)) for n in range(2))


def kernel(x_prompt, x_sample, cache_a_k, cache_a_v, cache_mla_ckv, cache_mla_krope, norm_mix_g, w_in, g_aq, g_ak, rel_bias, g_kv, g_kr, g_qn, g_qr, g_kn, w_kv_b, w_pa, w_pb, w_out, norm_ffn_g, w_up, w_down):
    b, s, _ = x_prompt.shape
    bs, t, _ = x_sample.shape
    past = cache_mla_ckv.shape[2]
    n_band = cache_a_k.shape[2]
    depth = w_in.shape[0]
    keep = min(BAND_PAST, s)
    tm_p = 1024
    tq_mla = 512
    tm_s = bs * t
    assert s % tm_p == 0 and s % BAND_TQ == 0 and s % tq_mla == 0 and t == CHUNK and past % CHUNK == 0
    tm_cache = min(bs * past, 512)
    assert (bs * past) % tm_cache == 0
    tm_keep = min(keep, 512)
    assert keep % tm_keep == 0 and s % tm_keep == 0 and (s - keep) % tm_keep == 0
    cs_p = _rope_table(jnp.arange(s), max(s, tm_p))
    cs_s = _rope_table(past + jnp.arange(t), tm_s)

    yp = x_prompt.reshape(b * s, D_MODEL)
    ys = x_sample.reshape(bs * t, D_MODEL)
    outs = [[] for _ in range(8)]
    for l in range(depth):
        p = _layer_weights(l, norm_mix_g, w_in, g_aq, g_ak, g_kv, g_kr, g_qn, g_qr, g_kn,
                           w_kv_b, w_pa, w_pb, w_out, norm_ffn_g, w_up, w_down)
        bias, bias_t = _band_bias(rel_bias[l])

        xn, qkv, vt, q_mla, ckv, ckv_bf, krr, kr = _proj(yp, p, cs_p, tm_p)
        kb = keep // tm_keep
        ak, av = _cache_shape(_kv_f32(
            yp, p["gx"], p["w_in"], p["g_ak"], tm_keep, b * keep,
            lambda i: (i // kb) * (s // tm_keep) + (s - keep) // tm_keep + i % kb), (b, keep))
        oa = _band_prompt(qkv.reshape(b, s, 3 * BAND_WIDTH), vt, bias_t)
        k_mla, vt_mla = _expand(ckv_bf, p["w_kv_b"], p["g_kn"], krr, tq_mla, True)
        ob = _mla_prompt(q_mla.reshape(b, s, -1), k_mla.reshape(b, s, -1),
                         vt_mla.reshape(b, s // tq_mla, MLA_WIDTH, tq_mla), tq_mla)
        for o, val in zip(outs[:4], (ak, av, ckv.reshape(b, s, MLA_KV_RANK), kr.reshape(b, s, MLA_ROPE))):
            o.append(val)
        yp = _merge_ffn(yp, xn, oa.reshape(b * s, BAND_WIDTH), ob.reshape(b * s, MLA_WIDTH), p, tm_p)

        xn, qkv, _, q_mla, ckv, ckv_bf, krr, kr = _proj(ys, p, cs_s, tm_s)
        ak, av = _cache_shape(_kv_f32(ys, p["gx"], p["w_in"], p["g_ak"], tm_s, bs * t, lambda i: i), (bs, t))
        oa = _band_sample(qkv.reshape(bs, t, 3 * BAND_WIDTH),
                          cache_a_k[l].reshape(bs, n_band * BAND_HEADS, BAND_HEAD_DIM),
                          cache_a_v[l].reshape(bs, n_band * BAND_HEADS, BAND_HEAD_DIM), bias)
        k_old, v_old = _expand(cache_mla_ckv[l].reshape(bs * past, MLA_KV_RANK), p["w_kv_b"], p["g_kn"],
                               cache_mla_krope[l].reshape(bs * past, MLA_ROPE), tm_cache, False)
        k_new, v_new = _expand(ckv_bf, p["w_kv_b"], p["g_kn"], krr, tm_s, False)
        ob = _mla_sample(q_mla.reshape(bs, t, -1), k_old.reshape(bs, past, -1), v_old.reshape(bs, past, -1),
                         k_new.reshape(bs, t, -1), v_new.reshape(bs, t, -1))
        for o, val in zip(outs[4:], (ak, av, ckv.reshape(bs, t, MLA_KV_RANK), kr.reshape(bs, t, MLA_ROPE))):
            o.append(val)
        ys = _merge_ffn(ys, xn, oa.reshape(bs * t, BAND_WIDTH), ob.reshape(bs * t, MLA_WIDTH), p, tm_s)

    return (yp.reshape(b, s, D_MODEL), ys.reshape(bs, t, D_MODEL)) + tuple(jnp.stack(o) for o in outs)
```

```python
import functools

import jax
import jax.numpy as jnp
from jax import lax
from jax.experimental import pallas as pl
from jax.experimental.pallas import tpu as pltpu

F32 = jnp.float32
BF16 = jnp.bfloat16

D_MODEL = 2048
CHUNK = 64
BAND_PAST = 8 * CHUNK
BAND_HEADS = 8
BAND_HEAD_DIM = 128
BAND_WIDTH = BAND_HEADS * BAND_HEAD_DIM
REL_MAX = 256
REL_SIZE = (CHUNK - 1) + REL_MAX + 1
BAND_SCALE = BAND_HEAD_DIM ** -0.5
MLA_HEADS = 8
MLA_NOPE = 128
MLA_ROPE = 64
MLA_QK = MLA_NOPE + MLA_ROPE
MLA_V = 128
MLA_WIDTH = MLA_HEADS * MLA_V
MLA_KV_RANK = 512
MLA_SCALE = MLA_QK ** -0.5
ROPE_THETA = 10000.0
D_FF = 4 * D_MODEL
EPS = 1e-6
NEG = -1e30
LOG2E = 1.4426950408889634
MLA_QSCALE = MLA_SCALE * LOG2E
BAND_QSCALE = BAND_SCALE * LOG2E

LANES = 128
MLA_QK_PAD = 2 * LANES
BAND_TQ = 256
BAND_WIN = BAND_TQ + BAND_PAST
BAND_PIECES = BAND_WIN // BAND_TQ
VMEM_LIMIT = 56 * 1024 * 1024
PROJ_TN = 512
SUM_ROWS = 16
BAND_SCORE_BUFFERS = 3
MLA_SCORE_BUFFERS = 3


def _params(n_axes, vmem=VMEM_LIMIT):
    return pltpu.CompilerParams(dimension_semantics=("arbitrary",) * n_axes,
                                vmem_limit_bytes=vmem)


def _rms(a, g):
    return a * lax.rsqrt(jnp.mean(a * a, axis=-1, keepdims=True) + EPS) * g


def _dot(a, b):
    return jnp.dot(a, b, preferred_element_type=F32)


def _dot_nt(a, b):
    return lax.dot_general(a, b, (((1,), (1,)), ((), ())), preferred_element_type=F32)


def _rope_pair(ar, g, cs):
    lane = lax.broadcasted_iota(jnp.int32, (1, LANES), 1)
    first = (lane < MLA_ROPE).astype(F32)
    ss = jnp.sum(ar * ar * first, axis=-1, keepdims=True) / MLA_ROPE
    return ar * lax.rsqrt(ss + EPS) * g * cs


_NB_BAND_NORM = 2 * BAND_WIDTH // PROJ_TN
_NB_BAND = 3 * BAND_WIDTH // PROJ_TN
_NB_MLA_Q = MLA_HEADS * MLA_QK_PAD // PROJ_TN
_NB_PROJ = _NB_BAND + _NB_MLA_Q + 1
_CKV_BLOCK = (3 * BAND_WIDTH + MLA_HEADS * MLA_QK) // PROJ_TN
assert MLA_KV_RANK == PROJ_TN and _CKV_BLOCK * PROJ_TN == 3 * BAND_WIDTH + MLA_HEADS * MLA_QK


def _proj_kernel(x_ref, gx_ref, w_ref, wq_ref, wkr_ref, gh_ref, gqn_ref, gqr_ref, gkv_ref, gkr_ref, cs_ref,
                 xn_out_ref, qkv_ref, vt_ref, q_ref, ckv_ref, ckvb_ref, krr_ref, kr_ref, xn_ref):
    j = pl.program_id(1)

    def block_product(weights=w_ref):
        return _dot(xn_ref[...], weights[...])

    def band_norm(acc):
        for k in range(PROJ_TN // LANES):
            sl = slice(k * LANES, (k + 1) * LANES)
            qkv_ref[:, sl] = _rms(acc[:, sl], gh_ref[:, sl]).astype(BF16)

    @pl.when(j == 0)
    def _():
        xn = _rms(x_ref[...], gx_ref[...]).astype(BF16)
        xn_ref[...] = xn
        xn_out_ref[...] = xn
        band_norm(_dot(xn, w_ref[...]))

    @pl.when(jnp.logical_and(j > 0, j < _NB_BAND_NORM))
    def _():
        band_norm(block_product())

    @pl.when(jnp.logical_and(j >= _NB_BAND_NORM, j < _NB_BAND))
    def _():
        acc = block_product()
        qkv_ref[...] = acc.astype(BF16)
        vt_ref[...] = acc.T.astype(BF16)

    @pl.when(jnp.logical_and(j >= _NB_BAND, j < _NB_BAND + _NB_MLA_Q))
    def _():
        acc = block_product(wq_ref)
        cs = cs_ref[...]
        for k in range(PROJ_TN // MLA_QK_PAD):
            c0 = k * MLA_QK_PAD
            qn = _rms(acc[:, c0:c0 + LANES], gqn_ref[...]) * MLA_QSCALE
            qr = _rope_pair(acc[:, c0 + LANES:c0 + 2 * LANES], gqr_ref[...], cs) * MLA_QSCALE
            q_ref[:, c0:c0 + LANES] = qn.astype(BF16)
            q_ref[:, c0 + LANES:c0 + 2 * LANES] = qr.astype(BF16)

    @pl.when(j == _NB_BAND + _NB_MLA_Q)
    def _():
        cn = _rms(block_product(), gkv_ref[...])
        ckv_ref[...] = cn
        ckvb_ref[...] = cn.astype(BF16)
        t = _rope_pair(_dot(xn_ref[...], wkr_ref[...]), gkr_ref[...], cs_ref[...])
        krr = t + pltpu.roll(t, MLA_ROPE, 1)
        krr_ref[...] = krr
        kr_ref[...] = krr[:, :MLA_ROPE]


def _proj(x, p, cs, tm):
    m = x.shape[0]
    tn = PROJ_TN
    n_cs = cs.shape[0] // tm
    const = lambda i, j: (0, 0)
    rows = lambda i, j: (i, 0)
    return pl.pallas_call(
        _proj_kernel,
        grid=(m // tm, _NB_PROJ),
        in_specs=[pl.BlockSpec((tm, D_MODEL), rows),
                  pl.BlockSpec((1, D_MODEL), const),
                  pl.BlockSpec((D_MODEL, tn), lambda i, j: (0, jnp.where(
                      j < _NB_BAND, j, jnp.where(j == _NB_PROJ - 1, _CKV_BLOCK, _NB_BAND - 1)))),
                  pl.BlockSpec((D_MODEL, tn), lambda i, j: (0, jnp.clip(j - _NB_BAND, 0, _NB_MLA_Q - 1))),
                  pl.BlockSpec((D_MODEL, LANES), const),
                  pl.BlockSpec((1, tn), lambda i, j: (0, jnp.minimum(j, _NB_BAND - 1))),
                  pl.BlockSpec((1, LANES), const),
                  pl.BlockSpec((1, LANES), const),
                  pl.BlockSpec((1, MLA_KV_RANK), const),
                  pl.BlockSpec((1, LANES), const),
                  pl.BlockSpec((tm, LANES), lambda i, j: (i % n_cs, 0))],
        out_specs=[pl.BlockSpec((tm, D_MODEL), rows),
                   pl.BlockSpec((tm, tn), lambda i, j: (i, jnp.minimum(j, _NB_BAND - 1))),
                   pl.BlockSpec((tn, tm), lambda i, j: (jnp.clip(j - _NB_BAND_NORM, 0, _NB_BAND - _NB_BAND_NORM - 1), i)),
                   pl.BlockSpec((tm, tn), lambda i, j: (i, jnp.clip(j - _NB_BAND, 0, _NB_MLA_Q - 1))),
                   pl.BlockSpec((tm, MLA_KV_RANK), rows),
                   pl.BlockSpec((tm, MLA_KV_RANK), rows),
                   pl.BlockSpec((tm, LANES), rows),
                   pl.BlockSpec((tm, MLA_ROPE), rows)],
        out_shape=[jax.ShapeDtypeStruct((m, D_MODEL), BF16),
                   jax.ShapeDtypeStruct((m, 3 * BAND_WIDTH), BF16),
                   jax.ShapeDtypeStruct((BAND_WIDTH, m), BF16),
                   jax.ShapeDtypeStruct((m, MLA_HEADS * MLA_QK_PAD), BF16),
                   jax.ShapeDtypeStruct((m, MLA_KV_RANK), F32),
                   jax.ShapeDtypeStruct((m, MLA_KV_RANK), BF16),
                   jax.ShapeDtypeStruct((m, LANES), F32),
                   jax.ShapeDtypeStruct((m, MLA_ROPE), F32)],
        scratch_shapes=[pltpu.VMEM((tm, D_MODEL), BF16)],
        compiler_params=_params(2, 62 * 1024 * 1024), name="proj")(
            x, p["gx"], p["w_in"], p["w_q"], p["w_kr"], p["g_a"], p["g_qn"], p["g_qr"], p["g_kv"], p["g_kr"], cs)


def _kv_f32_kernel(x_ref, gx_ref, w_ref, g_ref, o_ref, xn_ref):
    j = pl.program_id(1)
    tm = x_ref.shape[0]

    @pl.when(j == 0)
    def _():
        xn_ref[...] = _rms(x_ref[...], gx_ref[...]).astype(BF16)

    def put(h, y):
        o_ref[0, pl.ds(h, tm, stride=BAND_HEADS), :] = y

    @pl.when(j == 0)
    def _():
        acc = _dot(xn_ref[...], w_ref[...])
        for h in range(BAND_HEADS):
            put(h, _rms(acc[:, h * BAND_HEAD_DIM:(h + 1) * BAND_HEAD_DIM], g_ref[...]))

    @pl.when(j == 1)
    def _():
        acc = _dot(xn_ref[...], w_ref[...])
        for h in range(BAND_HEADS):
            put(h, acc[:, h * BAND_HEAD_DIM:(h + 1) * BAND_HEAD_DIM])


def _kv_f32(x, gx, w_in, g, tm, n_rows, row_block):
    return pl.pallas_call(
        _kv_f32_kernel,
        grid=(n_rows // tm, 2),
        in_specs=[pl.BlockSpec((tm, D_MODEL), lambda i, j: (row_block(i), 0)),
                  pl.BlockSpec((1, D_MODEL), lambda i, j: (0, 0)),
                  pl.BlockSpec((D_MODEL, BAND_WIDTH), lambda i, j: (0, j + 1)),
                  pl.BlockSpec((1, LANES), lambda i, j: (0, 0))],
        out_specs=pl.BlockSpec((1, tm * BAND_HEADS, BAND_HEAD_DIM), lambda i, j: (j, i, 0)),
        out_shape=jax.ShapeDtypeStruct((2, n_rows * BAND_HEADS, BAND_HEAD_DIM), F32),
        scratch_shapes=[pltpu.VMEM((tm, D_MODEL), BF16)],
        compiler_params=_params(2), name="kv_f32")(x, gx, w_in, g)


def _expand_kernel(c_ref, w_ref, gkn_ref, krr_ref, k_ref, v_ref, *, v_transposed):
    c = c_ref[...].astype(BF16)
    krr = krr_ref[...].astype(BF16)
    if krr.shape[1] == MLA_ROPE:
        krr = jnp.concatenate([krr, krr], axis=1)
    for h in range(MLA_HEADS):
        a = _dot(c, w_ref[:, h * 2 * LANES:(h + 1) * 2 * LANES])
        k_ref[:, h * MLA_QK_PAD:h * MLA_QK_PAD + LANES] = _rms(a[:, :LANES], gkn_ref[...]).astype(BF16)
        k_ref[:, h * MLA_QK_PAD + LANES:(h + 1) * MLA_QK_PAD] = krr
        if v_transposed:
            slab = v_ref.shape[2]
            for n in range(v_ref.shape[0]):
                v_ref[n, h * MLA_V:(h + 1) * MLA_V, :] = a[n * slab:(n + 1) * slab, LANES:].T.astype(BF16)
        else:
            v_ref[:, h * MLA_V:(h + 1) * MLA_V] = a[:, LANES:].astype(BF16)


def _expand(c, w, gkn, krr, tm, v_transposed, slab=None):
    m = c.shape[0]
    if v_transposed:
        v_spec = pl.BlockSpec((tm // slab, MLA_WIDTH, slab), lambda i: (i, 0, 0))
        v_shape = jax.ShapeDtypeStruct((m // slab, MLA_WIDTH, slab), BF16)
    else:
        v_spec = pl.BlockSpec((tm, MLA_WIDTH), lambda i: (i, 0))
        v_shape = jax.ShapeDtypeStruct((m, MLA_WIDTH), BF16)
    return pl.pallas_call(
        functools.partial(_expand_kernel, v_transposed=v_transposed),
        grid=(m // tm,),
        in_specs=[pl.BlockSpec((tm, MLA_KV_RANK), lambda i: (i, 0)),
                  pl.BlockSpec(w.shape, lambda i: (0, 0)),
                  pl.BlockSpec((1, LANES), lambda i: (0, 0)),
                  pl.BlockSpec((tm, krr.shape[1]), lambda i: (i, 0))],
        out_specs=[pl.BlockSpec((tm, MLA_HEADS * MLA_QK_PAD), lambda i: (i, 0)), v_spec],
        out_shape=[jax.ShapeDtypeStruct((m, MLA_HEADS * MLA_QK_PAD), BF16), v_shape],
        compiler_params=_params(1), name="expand")(c, w, gkn, krr)


def _band_bias_kernel(b_ref, o_ref, ot_ref):
    t = pltpu.roll(jnp.broadcast_to(b_ref[0], (BAND_TQ, 2 * BAND_PAST)), 0, 1,
                   stride=1, stride_axis=0)[:, :BAND_WIN] * LOG2E
    qc = lax.broadcasted_iota(jnp.int32, (BAND_TQ, BAND_WIN), 0) // CHUNK
    kc = lax.broadcasted_iota(jnp.int32, (BAND_TQ, BAND_WIN), 1) // CHUNK
    t = jnp.where(kc >= qc, jnp.where(kc <= qc + BAND_PAST // CHUNK, t, NEG), NEG)
    o_ref[0] = t
    tt = t.T
    key = lax.broadcasted_iota(jnp.int32, (BAND_WIN, BAND_TQ), 0)
    for v in range(BAND_PIECES):
        n_missing = BAND_PIECES - 1 - v
        ot_ref[v, 0] = jnp.where(key >= n_missing * BAND_TQ, tt, NEG)


def _band_bias(rel_bias):
    far = jnp.broadcast_to(rel_bias[:, REL_SIZE - 1:], (BAND_HEADS, BAND_PAST - REL_MAX))
    near = jnp.broadcast_to(rel_bias[:, :1], (BAND_HEADS, BAND_WIN - BAND_PAST - CHUNK))
    wrap = jnp.broadcast_to(rel_bias[:, REL_SIZE - 1:], (BAND_HEADS, 2 * BAND_PAST - BAND_WIN))
    row0 = jnp.concatenate([far, rel_bias[:, ::-1], near, wrap], axis=1)
    assert row0.shape == (BAND_HEADS, 2 * BAND_PAST)
    return pl.pallas_call(
        _band_bias_kernel,
        grid=(BAND_HEADS,),
        in_specs=[pl.BlockSpec((1, 1, 2 * BAND_PAST), lambda h: (h, 0, 0))],
        out_specs=[pl.BlockSpec((1, BAND_TQ, BAND_WIN), lambda h: (h, 0, 0)),
                   pl.BlockSpec((BAND_PIECES, 1, BAND_WIN, BAND_TQ), lambda h: (0, h, 0, 0))],
        out_shape=[jax.ShapeDtypeStruct((BAND_HEADS, BAND_TQ, BAND_WIN), F32),
                   jax.ShapeDtypeStruct((BAND_PIECES, BAND_HEADS, BAND_WIN, BAND_TQ), F32)],
        compiler_params=_params(1), name="band_bias")(row0[:, None, :])


def _softmax_pv(s, v):
    m = functools.reduce(jnp.maximum, [jnp.max(sp, axis=-1, keepdims=True) for sp in s])
    e = [jnp.exp2(sp - m) for sp in s]
    l = functools.reduce(jnp.add, [jnp.sum(ep, axis=-1, keepdims=True) for ep in e])
    o = functools.reduce(jnp.add, [_dot(ep.astype(BF16), vp) for ep, vp in zip(e, v)])
    return (o / l).astype(BF16)


def _band_prompt_kernel(q_ref, ka_ref, kb_ref, kc_ref, va_ref, vb_ref, vc_ref, bias_ref, o_ref, *bufs):
    def head(h):
        return slice(h * BAND_HEAD_DIM, (h + 1) * BAND_HEAD_DIM)

    def scores(h):
        k = jnp.concatenate([r[0, :, head(h)] for r in (ka_ref, kb_ref, kc_ref)], axis=0)
        return _dot_nt(k, q_ref[0, :, head(h)]) + bias_ref[0, h]

    ahead = len(bufs) - 1
    for h in range(ahead):
        bufs[h][...] = scores(h)
    for h in range(BAND_HEADS):
        if h + ahead < BAND_HEADS:
            bufs[(h + ahead) % len(bufs)][...] = scores(h + ahead)
        s = bufs[h % len(bufs)][...]
        vt = jnp.concatenate([r[head(h), :] for r in (va_ref, vb_ref, vc_ref)], axis=1)
        vt = jnp.concatenate([vt, jnp.ones((SUM_ROWS, BAND_WIN), BF16)], axis=0)
        e = jnp.exp2(s - jnp.max(s, axis=0, keepdims=True))
        o = _dot(vt, e.astype(BF16))
        o = o[:BAND_HEAD_DIM] / o[BAND_HEAD_DIM:BAND_HEAD_DIM + 1]
        o_ref[0, :, head(h)] = o.T.astype(BF16)


def _band_prompt(qkv, vt, bias_t):
    b, s, _ = qkv.shape
    nq = s // BAND_TQ
    blk = (1, BAND_TQ, BAND_WIDTH)

    def k_spec(back):
        return pl.BlockSpec(blk, lambda bi, i: (bi, jnp.maximum(i - back, 0), 1))

    def vt_spec(back):
        return pl.BlockSpec((BAND_WIDTH, BAND_TQ), lambda bi, i: (0, bi * nq + jnp.maximum(i - back, 0)))

    return pl.pallas_call(
        _band_prompt_kernel,
        grid=(b, nq),
        in_specs=[pl.BlockSpec(blk, lambda bi, i: (bi, i, 0))]
                 + [k_spec(BAND_PIECES - 1 - p) for p in range(BAND_PIECES)]
                 + [vt_spec(BAND_PIECES - 1 - p) for p in range(BAND_PIECES)]
                 + [pl.BlockSpec((1,) + bias_t.shape[1:],
                                 lambda bi, i: (jnp.minimum(i, BAND_PIECES - 1), 0, 0, 0))],
        out_specs=pl.BlockSpec(blk, lambda bi, i: (bi, i, 0)),
        out_shape=jax.ShapeDtypeStruct((b, s, BAND_WIDTH), BF16),
        scratch_shapes=[pltpu.VMEM((BAND_WIN, BAND_TQ), F32)] * BAND_SCORE_BUFFERS,
        compiler_params=_params(2), name="band_prompt")(qkv, qkv, qkv, qkv, vt, vt, vt, bias_t)


def _band_sample_kernel(q_ref, kn_ref, vn_ref, kc_ref, vc_ref, bias_ref, o_ref, *, n_past):
    t = q_ref.shape[1]
    for h in range(BAND_HEADS):
        hs = slice(h * BAND_HEAD_DIM, (h + 1) * BAND_HEAD_DIM)
        q = q_ref[0, :, hs]
        k_old = kc_ref[0, pl.ds(h, n_past, stride=BAND_HEADS), :].astype(BF16)
        v_old = vc_ref[0, pl.ds(h, n_past, stride=BAND_HEADS), :].astype(BF16)
        s = [_dot_nt(q, k_old) + bias_ref[h, :t, :n_past],
             _dot_nt(q, kn_ref[0, :, hs]) + bias_ref[h, :t, n_past:n_past + t]]
        o_ref[0, :, hs] = _softmax_pv(s, [v_old, vn_ref[0, :, hs]])


def _band_sample(qkv, k_cache, v_cache, bias):
    b, t, _ = qkv.shape
    n_past = k_cache.shape[1] // BAND_HEADS
    new = (1, t, BAND_WIDTH)
    old = (1, n_past * BAND_HEADS, BAND_HEAD_DIM)
    return pl.pallas_call(
        functools.partial(_band_sample_kernel, n_past=n_past),
        grid=(b,),
        in_specs=[pl.BlockSpec(new, lambda bi: (bi, 0, 0)),
                  pl.BlockSpec(new, lambda bi: (bi, 0, 1)),
                  pl.BlockSpec(new, lambda bi: (bi, 0, 2)),
                  pl.BlockSpec(old, lambda bi: (bi, 0, 0)),
                  pl.BlockSpec(old, lambda bi: (bi, 0, 0)),
                  pl.BlockSpec(bias.shape, lambda bi: (0, 0, 0))],
        out_specs=pl.BlockSpec(new, lambda bi: (bi, 0, 0)),
        out_shape=jax.ShapeDtypeStruct((b, t, BAND_WIDTH), BF16),
        compiler_params=_params(1), name="band_sample")(qkv, qkv, qkv, k_cache, v_cache, bias)


def _mla_prompt_kernel(q_ref, k_ref, vt_ref, o_ref, mask_ref, *bufs, tq, nq):
    def scores(i):
        return _dot_nt(k_ref[0, :(i + 1) * tq, :], q_ref[0, i * tq:(i + 1) * tq, :])

    kc = lax.broadcasted_iota(jnp.int32, (tq, tq), 0) // CHUNK
    qc = lax.broadcasted_iota(jnp.int32, (tq, tq), 1) // CHUNK
    mask_ref[...] = jnp.where(kc <= qc, 0.0, NEG)

    ones = jnp.ones((SUM_ROWS, tq), BF16)
    ahead = len(bufs) - 1
    for i in range(min(ahead, nq)):
        bufs[i][:(i + 1) * tq] = scores(i)
    for i in range(nq):
        if i + ahead < nq:
            bufs[(i + ahead) % len(bufs)][:(i + ahead + 1) * tq] = scores(i + ahead)
        buf = bufs[i % len(bufs)]
        blocks = [buf[j * tq:(j + 1) * tq] for j in range(i)] + [buf[i * tq:(i + 1) * tq] + mask_ref[...]]
        m = functools.reduce(jnp.maximum, [jnp.max(sj, axis=0, keepdims=True) for sj in blocks])
        acc = functools.reduce(jnp.add, [
            _dot(jnp.concatenate([vt_ref[0, j], ones], axis=0), jnp.exp2(sj - m).astype(BF16))
            for j, sj in enumerate(blocks)])
        o_ref[0, i * tq:(i + 1) * tq, :] = (acc[:MLA_V] / acc[MLA_V:MLA_V + 1]).T.astype(BF16)


def _mla_prompt(q, k, vt, tq):
    b, s, _ = q.shape
    nq = s // tq
    return pl.pallas_call(
        functools.partial(_mla_prompt_kernel, tq=tq, nq=nq),
        grid=(b, MLA_HEADS),
        in_specs=[pl.BlockSpec((1, s, MLA_QK_PAD), lambda bi, h: (bi, 0, h)),
                  pl.BlockSpec((1, s, MLA_QK_PAD), lambda bi, h: (bi, 0, h)),
                  pl.BlockSpec((1, nq, MLA_V, tq), lambda bi, h: (bi, 0, h, 0))],
        out_specs=pl.BlockSpec((1, s, MLA_V), lambda bi, h: (bi, 0, h)),
        out_shape=jax.ShapeDtypeStruct((b, s, MLA_WIDTH), BF16),
        scratch_shapes=[pltpu.VMEM((tq, tq), F32)] + [pltpu.VMEM((s, tq), F32)] * MLA_SCORE_BUFFERS,
        compiler_params=_params(2), name="mla_prompt")(q, k, vt)


def _mla_sample_kernel(q_ref, ko_ref, vo_ref, kn_ref, vn_ref, o_ref):
    for h in range(MLA_HEADS):
        qk = slice(h * MLA_QK_PAD, (h + 1) * MLA_QK_PAD)
        hv = slice(h * MLA_V, (h + 1) * MLA_V)
        q = q_ref[0, :, qk]
        s = [_dot_nt(q, ko_ref[0, :, qk]), _dot_nt(q, kn_ref[0, :, qk])]
        o_ref[0, :, hv] = _softmax_pv(s, [vo_ref[0, :, hv], vn_ref[0, :, hv]])


def _mla_sample(q, k_old, v_old, k_new, v_new):
    b, t, _ = q.shape

    def whole(a):
        return pl.BlockSpec((1,) + a.shape[1:], lambda bi: (bi, 0, 0))

    return pl.pallas_call(
        _mla_sample_kernel,
        grid=(b,),
        in_specs=[whole(q), whole(k_old), whole(v_old), whole(k_new), whole(v_new)],
        out_specs=pl.BlockSpec((1, t, MLA_WIDTH), lambda bi: (bi, 0, 0)),
        out_shape=jax.ShapeDtypeStruct((b, t, MLA_WIDTH), BF16),
        compiler_params=_params(1), name="mla_sample")(q, k_old, v_old, k_new, v_new)


def _mix_kernel(xn_ref, oa_ref, ob_ref, wga_ref, wgb_ref, wpa_ref, wpb_ref, o_ref):
    xn = xn_ref[...]
    a = jax.nn.sigmoid(_dot(xn, wga_ref[...])) * _dot(oa_ref[...], wpa_ref[...])
    b = jax.nn.sigmoid(_dot(xn, wgb_ref[...])) * _dot(ob_ref[...], wpb_ref[...])
    o_ref[...] = (a + b).astype(BF16)


def _mix(xn, oa, ob, wga, wgb, wpa, wpb, tm, tn=1024):
    m = xn.shape[0]
    return pl.pallas_call(
        _mix_kernel,
        grid=(m // tm, D_MODEL // tn),
        in_specs=[pl.BlockSpec((tm, D_MODEL), lambda i, j: (i, 0)),
                  pl.BlockSpec((tm, BAND_WIDTH), lambda i, j: (i, 0)),
                  pl.BlockSpec((tm, MLA_WIDTH), lambda i, j: (i, 0)),
                  pl.BlockSpec((D_MODEL, tn), lambda i, j: (0, j)),
                  pl.BlockSpec((D_MODEL, tn), lambda i, j: (0, j)),
                  pl.BlockSpec((BAND_WIDTH, tn), lambda i, j: (0, j)),
                  pl.BlockSpec((MLA_WIDTH, tn), lambda i, j: (0, j))],
        out_specs=pl.BlockSpec((tm, tn), lambda i, j: (i, j)),
        out_shape=jax.ShapeDtypeStruct((m, D_MODEL), BF16),
        compiler_params=_params(2, 62 * 1024 * 1024), name="mix")(xn, oa, ob, wga, wgb, wpa, wpb)


def _outproj_kernel(x_ref, mix_ref, w_ref, h_ref):
    h_ref[...] = x_ref[...] + _dot(mix_ref[...], w_ref[...])


def _outproj(x, mix, w, tm):
    m = x.shape[0]
    rows = pl.BlockSpec((tm, D_MODEL), lambda i: (i, 0))
    return pl.pallas_call(
        _outproj_kernel,
        grid=(m // tm,),
        in_specs=[rows, rows, pl.BlockSpec((D_MODEL, D_MODEL), lambda i: (0, 0))],
        out_specs=rows,
        out_shape=jax.ShapeDtypeStruct((m, D_MODEL), F32),
        compiler_params=_params(1), name="outproj")(x, mix, w)


def _ffn_kernel(h_ref, g_ref, wu_ref, wd_ref, o_ref, hn_ref):
    def update(hn):
        u = jnp.maximum(_dot(hn, wu_ref[...]), 0.0)
        return _dot((u * u).astype(BF16), wd_ref[...])

    @pl.when(pl.program_id(1) == 0)
    def _():
        h = h_ref[...]
        hn = _rms(h, g_ref[...]).astype(BF16)
        hn_ref[...] = hn
        o_ref[...] = h + update(hn)

    @pl.when(pl.program_id(1) > 0)
    def _():
        o_ref[...] += update(hn_ref[...])


def _ffn(h, g, wu, wd, tm, tf=1024):
    m = h.shape[0]
    return pl.pallas_call(
        _ffn_kernel,
        grid=(m // tm, D_FF // tf),
        in_specs=[pl.BlockSpec((tm, D_MODEL), lambda i, f: (i, 0)),
                  pl.BlockSpec((1, D_MODEL), lambda i, f: (0, 0)),
                  pl.BlockSpec((D_MODEL, tf), lambda i, f: (0, f)),
                  pl.BlockSpec((tf, D_MODEL), lambda i, f: (f, 0))],
        out_specs=pl.BlockSpec((tm, D_MODEL), lambda i, f: (i, 0)),
        out_shape=jax.ShapeDtypeStruct((m, D_MODEL), F32),
        scratch_shapes=[pltpu.VMEM((tm, D_MODEL), BF16)],
        compiler_params=_params(2, 62 * 1024 * 1024), name="ffn")(h, g, wu, wd)


def _rope_table(pos, rows):
    half = MLA_ROPE // 2
    freqs = ROPE_THETA ** (-(jnp.arange(half, dtype=F32) / half))
    ang = pos.astype(F32)[:, None] * freqs[None, :]
    cos = jnp.cos(ang)
    sin = jnp.sin(ang)
    table = jnp.concatenate([cos, cos, -sin, sin], axis=1)
    return jnp.tile(table, (rows // table.shape[0], 1))


def _with_partner(w):
    half = MLA_ROPE // 2
    return jnp.concatenate([w, w[..., half:], w[..., :half]], axis=-1)


def _layer_weights(l, norm_mix_g, w_in, g_aq, g_ak, g_kv, g_kr, g_qn, g_qr, g_kn,
                   w_kv_b, w_pa, w_pb, w_out, norm_ffn_g, w_up, w_down):
    w = w_in[l].astype(BF16)
    c = 0
    parts = []
    for width in (3 * BAND_WIDTH, MLA_HEADS * MLA_QK, MLA_KV_RANK, MLA_ROPE, D_MODEL, D_MODEL):
        parts.append(w[:, c:c + width])
        c += width
    w_a, w_bq, w_ckv, w_kr, w_ga, w_gb = parts
    w_bq = w_bq.reshape(D_MODEL, MLA_HEADS, MLA_QK)
    w_q = jnp.concatenate([w_bq[..., :MLA_NOPE], _with_partner(w_bq[..., MLA_NOPE:])], axis=-1)
    w_q = w_q.reshape(D_MODEL, MLA_HEADS * MLA_QK_PAD)
    row = lambda g: g[l][None, :].astype(F32)
    return dict(
        gx=row(norm_mix_g),
        w_in=w,
        w_q=w_q,
        w_kr=_with_partner(w_kr),
        g_a=jnp.concatenate([jnp.tile(g_aq[l] * BAND_QSCALE, BAND_HEADS), jnp.tile(g_ak[l], BAND_HEADS),
                             jnp.ones((BAND_WIDTH,), F32)])[None, :],
        g_ak=row(g_ak),
        g_qn=row(g_qn),
        g_qr=_with_partner(g_qr[l])[None, :],
        g_kv=row(g_kv),
        g_kr=_with_partner(g_kr[l])[None, :],
        w_kv_b=w_kv_b[l].astype(BF16),
        g_kn=row(g_kn),
        w_ga=w_ga, w_gb=w_gb,
        w_pa=w_pa[l].astype(BF16), w_pb=w_pb[l].astype(BF16),
        w_out=w_out[l].astype(BF16),
        g_ffn=row(norm_ffn_g),
        w_up=w_up[l].astype(BF16), w_down=w_down[l].astype(BF16),
    )


def _merge_ffn(x2, xn, oa, ob, p, tm):
    mix = _mix(xn, oa, ob, p["w_ga"], p["w_gb"], p["w_pa"], p["w_pb"], tm)
    h = _outproj(x2, mix, p["w_out"], min(tm, 512))
    return _ffn(h, p["g_ffn"], p["w_up"], p["w_down"], tm)


def _cache_shape(kv, lead):
    return tuple(kv[n].reshape(lead + (BAND_HEADS, BAND_HEAD_DIM)) for n in range(2))


def kernel(x_prompt, x_sample, cache_a_k, cache_a_v, cache_mla_ckv, cache_mla_krope, norm_mix_g, w_in, g_aq, g_ak, rel_bias, g_kv, g_kr, g_qn, g_qr, g_kn, w_kv_b, w_pa, w_pb, w_out, norm_ffn_g, w_up, w_down):
    b, s, _ = x_prompt.shape
    bs, t, _ = x_sample.shape
    past = cache_mla_ckv.shape[2]
    n_band = cache_a_k.shape[2]
    depth = w_in.shape[0]
    keep = min(BAND_PAST, s)
    tm_p = 1024
    tq_mla = 256
    tm_s = bs * t
    assert s % tm_p == 0 and s % BAND_TQ == 0 and s % tq_mla == 0 and t == CHUNK and past % CHUNK == 0
    tm_cache = min(bs * past, 2048)
    assert (bs * past) % tm_cache == 0
    tm_keep = min(keep, 512)
    assert keep % tm_keep == 0 and s % tm_keep == 0 and (s - keep) % tm_keep == 0
    cs_p = _rope_table(jnp.arange(s), max(s, tm_p))
    cs_s = _rope_table(past + jnp.arange(t), tm_s)

    yp = x_prompt.reshape(b * s, D_MODEL)
    ys = x_sample.reshape(bs * t, D_MODEL)
    outs = [[] for _ in range(8)]
    for l in range(depth):
        p = _layer_weights(l, norm_mix_g, w_in, g_aq, g_ak, g_kv, g_kr, g_qn, g_qr, g_kn,
                           w_kv_b, w_pa, w_pb, w_out, norm_ffn_g, w_up, w_down)
        bias, bias_t = _band_bias(rel_bias[l])

        xn, qkv, vt, q_mla, ckv, ckv_bf, krr, kr = _proj(yp, p, cs_p, tm_p)
        kb = keep // tm_keep
        ak, av = _cache_shape(_kv_f32(
            yp, p["gx"], p["w_in"], p["g_ak"], tm_keep, b * keep,
            lambda i: (i // kb) * (s // tm_keep) + (s - keep) // tm_keep + i % kb), (b, keep))
        oa = _band_prompt(qkv.reshape(b, s, 3 * BAND_WIDTH), vt, bias_t)
        k_mla, vt_mla = _expand(ckv_bf, p["w_kv_b"], p["g_kn"], krr, tm_p, True, tq_mla)
        ob = _mla_prompt(q_mla.reshape(b, s, -1), k_mla.reshape(b, s, -1),
                         vt_mla.reshape(b, s // tq_mla, MLA_WIDTH, tq_mla), tq_mla)
        for o, val in zip(outs[:4], (ak, av, ckv.reshape(b, s, MLA_KV_RANK), kr.reshape(b, s, MLA_ROPE))):
            o.append(val)
        yp = _merge_ffn(yp, xn, oa.reshape(b * s, BAND_WIDTH), ob.reshape(b * s, MLA_WIDTH), p, tm_p)

        xn, qkv, _, q_mla, ckv, ckv_bf, krr, kr = _proj(ys, p, cs_s, tm_s)
        ak, av = _cache_shape(_kv_f32(ys, p["gx"], p["w_in"], p["g_ak"], tm_s, bs * t, lambda i: i), (bs, t))
        oa = _band_sample(qkv.reshape(bs, t, 3 * BAND_WIDTH),
                          cache_a_k[l].reshape(bs, n_band * BAND_HEADS, BAND_HEAD_DIM),
                          cache_a_v[l].reshape(bs, n_band * BAND_HEADS, BAND_HEAD_DIM), bias)
        k_old, v_old = _expand(cache_mla_ckv[l].reshape(bs * past, MLA_KV_RANK), p["w_kv_b"], p["g_kn"],
                               cache_mla_krope[l].reshape(bs * past, MLA_ROPE), tm_cache, False)
        k_new, v_new = _expand(ckv_bf, p["w_kv_b"], p["g_kn"], krr, tm_s, False)
        ob = _mla_sample(q_mla.reshape(bs, t, -1), k_old.reshape(bs, past, -1), v_old.reshape(bs, past, -1),
                         k_new.reshape(bs, t, -1), v_new.reshape(bs, t, -1))
        for o, val in zip(outs[4:], (ak, av, ckv.reshape(bs, t, MLA_KV_RANK), kr.reshape(bs, t, MLA_ROPE))):
            o.append(val)
        ys = _merge_ffn(ys, xn, oa.reshape(bs * t, BAND_WIDTH), ob.reshape(bs * t, MLA_WIDTH), p, tm_s)

    return (yp.reshape(b, s, D_MODEL), ys.reshape(bs, t, D_MODEL)) + tuple(jnp.stack(o) for o in outs)
```

```python
import functools

import jax
import jax.numpy as jnp
from jax import lax
from jax.experimental import pallas as pl
from jax.experimental.pallas import tpu as pltpu

F32 = jnp.float32
BF16 = jnp.bfloat16

D_MODEL = 2048
CHUNK = 64
BAND_PAST = 8 * CHUNK
BAND_HEADS = 8
BAND_HEAD_DIM = 128
BAND_WIDTH = BAND_HEADS * BAND_HEAD_DIM
REL_MAX = 256
REL_SIZE = (CHUNK - 1) + REL_MAX + 1
BAND_SCALE = BAND_HEAD_DIM ** -0.5
MLA_HEADS = 8
MLA_NOPE = 128
MLA_ROPE = 64
MLA_QK = MLA_NOPE + MLA_ROPE
MLA_V = 128
MLA_WIDTH = MLA_HEADS * MLA_V
MLA_KV_RANK = 512
MLA_SCALE = MLA_QK ** -0.5
ROPE_THETA = 10000.0
D_FF = 4 * D_MODEL
EPS = 1e-6
NEG = -1e30
LOG2E = 1.4426950408889634
MLA_QSCALE = MLA_SCALE * LOG2E
BAND_QSCALE = BAND_SCALE * LOG2E

LANES = 128
MLA_QK_PAD = 2 * LANES
BAND_TQ = 256
BAND_WIN = BAND_TQ + BAND_PAST
BAND_PIECES = BAND_WIN // BAND_TQ
VMEM_LIMIT = 56 * 1024 * 1024
PROJ_TN = 512
PROJ_ROW_CHUNKS = 4
SUM_ROWS = 16
BAND_SCORE_BUFFERS = 4
MLA_SCORE_BUFFERS = 3


def _params(n_axes, vmem=VMEM_LIMIT):
    return pltpu.CompilerParams(dimension_semantics=("arbitrary",) * n_axes,
                                vmem_limit_bytes=vmem)


def _rms(a, g):
    return a * lax.rsqrt(jnp.mean(a * a, axis=-1, keepdims=True) + EPS) * g


def _dot(a, b):
    return jnp.dot(a, b, preferred_element_type=F32)


def _dot_nt(a, b):
    return lax.dot_general(a, b, (((1,), (1,)), ((), ())), preferred_element_type=F32)


def _rope_pair(ar, g, cs):
    lane = lax.broadcasted_iota(jnp.int32, (1, LANES), 1)
    first = (lane < MLA_ROPE).astype(F32)
    ss = jnp.sum(ar * ar * first, axis=-1, keepdims=True) / MLA_ROPE
    return ar * lax.rsqrt(ss + EPS) * g * cs


_NB_BAND_NORM = 2 * BAND_WIDTH // PROJ_TN
_NB_BAND = 3 * BAND_WIDTH // PROJ_TN
_NB_MLA_Q = MLA_HEADS * MLA_QK_PAD // PROJ_TN
_NB_PROJ = _NB_BAND + _NB_MLA_Q + 1
_CKV_BLOCK = (3 * BAND_WIDTH + MLA_HEADS * MLA_QK) // PROJ_TN
assert MLA_KV_RANK == PROJ_TN and _CKV_BLOCK * PROJ_TN == 3 * BAND_WIDTH + MLA_HEADS * MLA_QK


def _proj_kernel(x_ref, gx_ref, w_ref, wq_ref, wkr_ref, gh_ref, gqn_ref, gqr_ref, gkv_ref, gkr_ref, cs_ref,
                 xn_out_ref, qkv_ref, vt_ref, q_ref, ckv_ref, ckvb_ref, krr_ref, kr_ref, xn_ref):
    j = pl.program_id(1)
    tm = x_ref.shape[0]
    row_chunks = [slice(c * tm // PROJ_ROW_CHUNKS, (c + 1) * tm // PROJ_ROW_CHUNKS) for c in range(PROJ_ROW_CHUNKS)]

    def product(r, weights=w_ref):
        return _dot(xn_ref[r, :], weights[...])

    def band_norm(r, acc):
        for k in range(PROJ_TN // LANES):
            sl = slice(k * LANES, (k + 1) * LANES)
            qkv_ref[r, sl] = _rms(acc[:, sl], gh_ref[:, sl]).astype(BF16)

    @pl.when(j == 0)
    def _():
        for r in row_chunks:
            xn = _rms(x_ref[r, :], gx_ref[...]).astype(BF16)
            xn_ref[r, :] = xn
            xn_out_ref[r, :] = xn
            band_norm(r, _dot(xn, w_ref[...]))

    @pl.when(jnp.logical_and(j > 0, j < _NB_BAND_NORM))
    def _():
        for r in row_chunks:
            band_norm(r, product(r))

    @pl.when(jnp.logical_and(j >= _NB_BAND_NORM, j < _NB_BAND))
    def _():
        for r in row_chunks:
            acc = product(r)
            qkv_ref[r, :] = acc.astype(BF16)
            vt_ref[:, r] = acc.T.astype(BF16)

    @pl.when(jnp.logical_and(j >= _NB_BAND, j < _NB_BAND + _NB_MLA_Q))
    def _():
        for r in row_chunks:
            acc = product(r, wq_ref)
            cs = cs_ref[r, :]
            for k in range(PROJ_TN // MLA_QK_PAD):
                c0 = k * MLA_QK_PAD
                qn = _rms(acc[:, c0:c0 + LANES], gqn_ref[...]) * MLA_QSCALE
                qr = _rope_pair(acc[:, c0 + LANES:c0 + 2 * LANES], gqr_ref[...], cs) * MLA_QSCALE
                q_ref[r, c0:c0 + LANES] = qn.astype(BF16)
                q_ref[r, c0 + LANES:c0 + 2 * LANES] = qr.astype(BF16)

    @pl.when(j == _NB_BAND + _NB_MLA_Q)
    def _():
        for r in row_chunks:
            cn = _rms(product(r), gkv_ref[...])
            ckv_ref[r, :] = cn
            ckvb_ref[r, :] = cn.astype(BF16)
            t = _rope_pair(product(r, wkr_ref), gkr_ref[...], cs_ref[r, :])
            krr = t + pltpu.roll(t, MLA_ROPE, 1)
            krr_ref[r, :] = krr
            kr_ref[r, :] = krr[:, :MLA_ROPE]


def _proj(x, p, cs, tm):
    m = x.shape[0]
    tn = PROJ_TN
    n_cs = cs.shape[0] // tm
    const = lambda i, j: (0, 0)
    rows = lambda i, j: (i, 0)
    return pl.pallas_call(
        _proj_kernel,
        grid=(m // tm, _NB_PROJ),
        in_specs=[pl.BlockSpec((tm, D_MODEL), rows),
                  pl.BlockSpec((1, D_MODEL), const),
                  pl.BlockSpec((D_MODEL, tn), lambda i, j: (0, jnp.where(
                      j < _NB_BAND, j, jnp.where(j == _NB_PROJ - 1, _CKV_BLOCK, _NB_BAND - 1)))),
                  pl.BlockSpec((D_MODEL, tn), lambda i, j: (0, jnp.clip(j - _NB_BAND, 0, _NB_MLA_Q - 1))),
                  pl.BlockSpec((D_MODEL, LANES), const),
                  pl.BlockSpec((1, tn), lambda i, j: (0, jnp.minimum(j, _NB_BAND - 1))),
                  pl.BlockSpec((1, LANES), const),
                  pl.BlockSpec((1, LANES), const),
                  pl.BlockSpec((1, MLA_KV_RANK), const),
                  pl.BlockSpec((1, LANES), const),
                  pl.BlockSpec((tm, LANES), lambda i, j: (i % n_cs, 0))],
        out_specs=[pl.BlockSpec((tm, D_MODEL), rows),
                   pl.BlockSpec((tm, tn), lambda i, j: (i, jnp.minimum(j, _NB_BAND - 1))),
                   pl.BlockSpec((tn, tm), lambda i, j: (jnp.clip(j - _NB_BAND_NORM, 0, _NB_BAND - _NB_BAND_NORM - 1), i)),
                   pl.BlockSpec((tm, tn), lambda i, j: (i, jnp.clip(j - _NB_BAND, 0, _NB_MLA_Q - 1))),
                   pl.BlockSpec((tm, MLA_KV_RANK), rows),
                   pl.BlockSpec((tm, MLA_KV_RANK), rows),
                   pl.BlockSpec((tm, LANES), rows),
                   pl.BlockSpec((tm, MLA_ROPE), rows)],
        out_shape=[jax.ShapeDtypeStruct((m, D_MODEL), BF16),
                   jax.ShapeDtypeStruct((m, 3 * BAND_WIDTH), BF16),
                   jax.ShapeDtypeStruct((BAND_WIDTH, m), BF16),
                   jax.ShapeDtypeStruct((m, MLA_HEADS * MLA_QK_PAD), BF16),
                   jax.ShapeDtypeStruct((m, MLA_KV_RANK), F32),
                   jax.ShapeDtypeStruct((m, MLA_KV_RANK), BF16),
                   jax.ShapeDtypeStruct((m, LANES), F32),
                   jax.ShapeDtypeStruct((m, MLA_ROPE), F32)],
        scratch_shapes=[pltpu.VMEM((tm, D_MODEL), BF16)],
        compiler_params=_params(2, 62 * 1024 * 1024), name="proj")(
            x, p["gx"], p["w_in"], p["w_q"], p["w_kr"], p["g_a"], p["g_qn"], p["g_qr"], p["g_kv"], p["g_kr"], cs)


def _kv_f32_kernel(x_ref, gx_ref, w_ref, g_ref, k_ref, v_ref, xn_ref):
    j = pl.program_id(1)
    tm = x_ref.shape[0]

    def heads(o_ref, acc, norm):
        for h in range(BAND_HEADS):
            y = acc[:, h * BAND_HEAD_DIM:(h + 1) * BAND_HEAD_DIM]
            o_ref[pl.ds(h, tm, stride=BAND_HEADS), :] = _rms(y, g_ref[...]) if norm else y

    @pl.when(j == 0)
    def _():
        xn = _rms(x_ref[...], gx_ref[...]).astype(BF16)
        xn_ref[...] = xn
        heads(k_ref, _dot(xn, w_ref[...]), True)

    @pl.when(j == 1)
    def _():
        heads(v_ref, _dot(xn_ref[...], w_ref[...]), False)


def _kv_f32(x, gx, w_in, g, tm, n_rows, row_block):
    out = pl.BlockSpec((tm * BAND_HEADS, BAND_HEAD_DIM), lambda i, j: (i, 0))
    shape = jax.ShapeDtypeStruct((n_rows * BAND_HEADS, BAND_HEAD_DIM), F32)
    return pl.pallas_call(
        _kv_f32_kernel,
        grid=(n_rows // tm, 2),
        in_specs=[pl.BlockSpec((tm, D_MODEL), lambda i, j: (row_block(i), 0)),
                  pl.BlockSpec((1, D_MODEL), lambda i, j: (0, 0)),
                  pl.BlockSpec((D_MODEL, BAND_WIDTH), lambda i, j: (0, j + 1)),
                  pl.BlockSpec((1, LANES), lambda i, j: (0, 0))],
        out_specs=[out, out],
        out_shape=[shape, shape],
        scratch_shapes=[pltpu.VMEM((tm, D_MODEL), BF16)],
        compiler_params=_params(2), name="kv_f32")(x, gx, w_in, g)


def _expand_kernel(c_ref, w_ref, gkn_ref, krr_ref, k_ref, v_ref, *, v_transposed):
    c = c_ref[...].astype(BF16)
    krr = krr_ref[...].astype(BF16)
    if krr.shape[1] == MLA_ROPE:
        krr = jnp.concatenate([krr, krr], axis=1)
    for h in range(MLA_HEADS):
        a = _dot(c, w_ref[:, h * 2 * LANES:(h + 1) * 2 * LANES])
        k_ref[:, h * MLA_QK_PAD:h * MLA_QK_PAD + LANES] = _rms(a[:, :LANES], gkn_ref[...]).astype(BF16)
        k_ref[:, h * MLA_QK_PAD + LANES:(h + 1) * MLA_QK_PAD] = krr
        if v_transposed:
            slab = v_ref.shape[2]
            for n in range(v_ref.shape[0]):
                v_ref[n, h * MLA_V:(h + 1) * MLA_V, :] = a[n * slab:(n + 1) * slab, LANES:].T.astype(BF16)
        else:
            v_ref[:, h * MLA_V:(h + 1) * MLA_V] = a[:, LANES:].astype(BF16)


def _expand(c, w, gkn, krr, tm, v_transposed, slab=None):
    m = c.shape[0]
    if v_transposed:
        v_spec = pl.BlockSpec((tm // slab, MLA_WIDTH, slab), lambda i: (i, 0, 0))
        v_shape = jax.ShapeDtypeStruct((m // slab, MLA_WIDTH, slab), BF16)
    else:
        v_spec = pl.BlockSpec((tm, MLA_WIDTH), lambda i: (i, 0))
        v_shape = jax.ShapeDtypeStruct((m, MLA_WIDTH), BF16)
    return pl.pallas_call(
        functools.partial(_expand_kernel, v_transposed=v_transposed),
        grid=(m // tm,),
        in_specs=[pl.BlockSpec((tm, MLA_KV_RANK), lambda i: (i, 0)),
                  pl.BlockSpec(w.shape, lambda i: (0, 0)),
                  pl.BlockSpec((1, LANES), lambda i: (0, 0)),
                  pl.BlockSpec((tm, krr.shape[1]), lambda i: (i, 0))],
        out_specs=[pl.BlockSpec((tm, MLA_HEADS * MLA_QK_PAD), lambda i: (i, 0)), v_spec],
        out_shape=[jax.ShapeDtypeStruct((m, MLA_HEADS * MLA_QK_PAD), BF16), v_shape],
        compiler_params=_params(1), name="expand")(c, w, gkn, krr)


def _band_bias_kernel(b_ref, o_ref, ot_ref):
    t = pltpu.roll(jnp.broadcast_to(b_ref[0], (BAND_TQ, 2 * BAND_PAST)), 0, 1,
                   stride=1, stride_axis=0)[:, :BAND_WIN] * LOG2E
    qc = lax.broadcasted_iota(jnp.int32, (BAND_TQ, BAND_WIN), 0) // CHUNK
    kc = lax.broadcasted_iota(jnp.int32, (BAND_TQ, BAND_WIN), 1) // CHUNK
    t = jnp.where(kc >= qc, jnp.where(kc <= qc + BAND_PAST // CHUNK, t, NEG), NEG)
    o_ref[0] = t
    tt = t.T
    key = lax.broadcasted_iota(jnp.int32, (BAND_WIN, BAND_TQ), 0)
    for v in range(BAND_PIECES):
        n_missing = BAND_PIECES - 1 - v
        ot_ref[v, 0] = jnp.where(key >= n_missing * BAND_TQ, tt, NEG)


def _band_bias(rel_bias):
    far = jnp.broadcast_to(rel_bias[:, REL_SIZE - 1:], (BAND_HEADS, BAND_PAST - REL_MAX))
    near = jnp.broadcast_to(rel_bias[:, :1], (BAND_HEADS, BAND_WIN - BAND_PAST - CHUNK))
    wrap = jnp.broadcast_to(rel_bias[:, REL_SIZE - 1:], (BAND_HEADS, 2 * BAND_PAST - BAND_WIN))
    row0 = jnp.concatenate([far, rel_bias[:, ::-1], near, wrap], axis=1)
    assert row0.shape == (BAND_HEADS, 2 * BAND_PAST)
    return pl.pallas_call(
        _band_bias_kernel,
        grid=(BAND_HEADS,),
        in_specs=[pl.BlockSpec((1, 1, 2 * BAND_PAST), lambda h: (h, 0, 0))],
        out_specs=[pl.BlockSpec((1, BAND_TQ, BAND_WIN), lambda h: (h, 0, 0)),
                   pl.BlockSpec((BAND_PIECES, 1, BAND_WIN, BAND_TQ), lambda h: (0, h, 0, 0))],
        out_shape=[jax.ShapeDtypeStruct((BAND_HEADS, BAND_TQ, BAND_WIN), F32),
                   jax.ShapeDtypeStruct((BAND_PIECES, BAND_HEADS, BAND_WIN, BAND_TQ), F32)],
        compiler_params=_params(1), name="band_bias")(row0[:, None, :])


def _softmax_pv(s, v):
    m = functools.reduce(jnp.maximum, [jnp.max(sp, axis=-1, keepdims=True) for sp in s])
    e = [jnp.exp2(sp - m) for sp in s]
    l = functools.reduce(jnp.add, [jnp.sum(ep, axis=-1, keepdims=True) for ep in e])
    o = functools.reduce(jnp.add, [_dot(ep.astype(BF16), vp) for ep, vp in zip(e, v)])
    return (o / l).astype(BF16)


def _band_prompt_kernel(q_ref, ka_ref, kb_ref, kc_ref, va_ref, vb_ref, vc_ref, bias_ref, o_ref, *bufs):
    def head(h):
        return slice(h * BAND_HEAD_DIM, (h + 1) * BAND_HEAD_DIM)

    def scores(h):
        k = jnp.concatenate([r[0, :, head(h)] for r in (ka_ref, kb_ref, kc_ref)], axis=0)
        return _dot_nt(k, q_ref[0, :, head(h)]) + bias_ref[0, h]

    ahead = len(bufs) - 1
    for h in range(ahead):
        bufs[h][...] = scores(h)
    for h in range(BAND_HEADS):
        if h + ahead < BAND_HEADS:
            bufs[(h + ahead) % len(bufs)][...] = scores(h + ahead)
        s = bufs[h % len(bufs)][...]
        vt = jnp.concatenate([r[head(h), :] for r in (va_ref, vb_ref, vc_ref)], axis=1)
        vt = jnp.concatenate([vt, jnp.ones((SUM_ROWS, BAND_WIN), BF16)], axis=0)
        e = jnp.exp2(s - jnp.max(s, axis=0, keepdims=True))
        o = _dot(vt, e.astype(BF16))
        o = o[:BAND_HEAD_DIM] / o[BAND_HEAD_DIM:BAND_HEAD_DIM + 1]
        o_ref[0, :, head(h)] = o.T.astype(BF16)


def _band_prompt(qkv, vt, bias_t):
    b, s, _ = qkv.shape
    nq = s // BAND_TQ
    blk = (1, BAND_TQ, BAND_WIDTH)

    def k_spec(back):
        return pl.BlockSpec(blk, lambda bi, i: (bi, jnp.maximum(i - back, 0), 1))

    def vt_spec(back):
        return pl.BlockSpec((BAND_WIDTH, BAND_TQ), lambda bi, i: (0, bi * nq + jnp.maximum(i - back, 0)))

    return pl.pallas_call(
        _band_prompt_kernel,
        grid=(b, nq),
        in_specs=[pl.BlockSpec(blk, lambda bi, i: (bi, i, 0))]
                 + [k_spec(BAND_PIECES - 1 - p) for p in range(BAND_PIECES)]
                 + [vt_spec(BAND_PIECES - 1 - p) for p in range(BAND_PIECES)]
                 + [pl.BlockSpec((1,) + bias_t.shape[1:],
                                 lambda bi, i: (jnp.minimum(i, BAND_PIECES - 1), 0, 0, 0))],
        out_specs=pl.BlockSpec(blk, lambda bi, i: (bi, i, 0)),
        out_shape=jax.ShapeDtypeStruct((b, s, BAND_WIDTH), BF16),
        scratch_shapes=[pltpu.VMEM((BAND_WIN, BAND_TQ), F32)] * BAND_SCORE_BUFFERS,
        compiler_params=_params(2), name="band_prompt")(qkv, qkv, qkv, qkv, vt, vt, vt, bias_t)


def _band_sample_kernel(q_ref, kn_ref, vn_ref, kc_ref, vc_ref, bias_ref, o_ref, *, n_past):
    t = q_ref.shape[1]
    for h in range(BAND_HEADS):
        hs = slice(h * BAND_HEAD_DIM, (h + 1) * BAND_HEAD_DIM)
        q = q_ref[0, :, hs]
        k_old = kc_ref[0, pl.ds(h, n_past, stride=BAND_HEADS), :].astype(BF16)
        v_old = vc_ref[0, pl.ds(h, n_past, stride=BAND_HEADS), :].astype(BF16)
        s = [_dot_nt(q, k_old) + bias_ref[h, :t, :n_past],
             _dot_nt(q, kn_ref[0, :, hs]) + bias_ref[h, :t, n_past:n_past + t]]
        o_ref[0, :, hs] = _softmax_pv(s, [v_old, vn_ref[0, :, hs]])


def _band_sample(qkv, k_cache, v_cache, bias):
    b, t, _ = qkv.shape
    n_past = k_cache.shape[1] // BAND_HEADS
    new = (1, t, BAND_WIDTH)
    old = (1, n_past * BAND_HEADS, BAND_HEAD_DIM)
    return pl.pallas_call(
        functools.partial(_band_sample_kernel, n_past=n_past),
        grid=(b,),
        in_specs=[pl.BlockSpec(new, lambda bi: (bi, 0, 0)),
                  pl.BlockSpec(new, lambda bi: (bi, 0, 1)),
                  pl.BlockSpec(new, lambda bi: (bi, 0, 2)),
                  pl.BlockSpec(old, lambda bi: (bi, 0, 0)),
                  pl.BlockSpec(old, lambda bi: (bi, 0, 0)),
                  pl.BlockSpec(bias.shape, lambda bi: (0, 0, 0))],
        out_specs=pl.BlockSpec(new, lambda bi: (bi, 0, 0)),
        out_shape=jax.ShapeDtypeStruct((b, t, BAND_WIDTH), BF16),
        compiler_params=_params(1), name="band_sample")(qkv, qkv, qkv, k_cache, v_cache, bias)


def _mla_prompt_kernel(q_ref, k_ref, vt_ref, o_ref, mask_ref, *bufs, tq, nq):
    def scores(i):
        return _dot_nt(k_ref[0, :(i + 1) * tq, :], q_ref[0, i * tq:(i + 1) * tq, :])

    kc = lax.broadcasted_iota(jnp.int32, (tq, tq), 0) // CHUNK
    qc = lax.broadcasted_iota(jnp.int32, (tq, tq), 1) // CHUNK
    mask_ref[...] = jnp.where(kc <= qc, 0.0, NEG)

    ones = jnp.ones((SUM_ROWS, tq), BF16)
    ahead = len(bufs) - 1
    for i in range(min(ahead, nq)):
        bufs[i][:(i + 1) * tq] = scores(i)
    for i in range(nq):
        if i + ahead < nq:
            bufs[(i + ahead) % len(bufs)][:(i + ahead + 1) * tq] = scores(i + ahead)
        buf = bufs[i % len(bufs)]
        blocks = [buf[j * tq:(j + 1) * tq] for j in range(i)] + [buf[i * tq:(i + 1) * tq] + mask_ref[...]]
        m = functools.reduce(jnp.maximum, [jnp.max(sj, axis=0, keepdims=True) for sj in blocks])
        acc = functools.reduce(jnp.add, [
            _dot(jnp.concatenate([vt_ref[0, j], ones], axis=0), jnp.exp2(sj - m).astype(BF16))
            for j, sj in enumerate(blocks)])
        o_ref[0, i * tq:(i + 1) * tq, :] = (acc[:MLA_V] / acc[MLA_V:MLA_V + 1]).T.astype(BF16)


def _mla_prompt(q, k, vt, tq):
    b, s, _ = q.shape
    nq = s // tq
    return pl.pallas_call(
        functools.partial(_mla_prompt_kernel, tq=tq, nq=nq),
        grid=(b, MLA_HEADS),
        in_specs=[pl.BlockSpec((1, s, MLA_QK_PAD), lambda bi, h: (bi, 0, h)),
                  pl.BlockSpec((1, s, MLA_QK_PAD), lambda bi, h: (bi, 0, h)),
                  pl.BlockSpec((1, nq, MLA_V, tq), lambda bi, h: (bi, 0, h, 0))],
        out_specs=pl.BlockSpec((1, s, MLA_V), lambda bi, h: (bi, 0, h)),
        out_shape=jax.ShapeDtypeStruct((b, s, MLA_WIDTH), BF16),
        scratch_shapes=[pltpu.VMEM((tq, tq), F32)] + [pltpu.VMEM((s, tq), F32)] * MLA_SCORE_BUFFERS,
        compiler_params=_params(2), name="mla_prompt")(q, k, vt)


def _mla_sample_kernel(q_ref, ko_ref, vo_ref, kn_ref, vn_ref, o_ref):
    for h in range(MLA_HEADS):
        qk = slice(h * MLA_QK_PAD, (h + 1) * MLA_QK_PAD)
        hv = slice(h * MLA_V, (h + 1) * MLA_V)
        q = q_ref[0, :, qk]
        s = [_dot_nt(q, ko_ref[0, :, qk]), _dot_nt(q, kn_ref[0, :, qk])]
        o_ref[0, :, hv] = _softmax_pv(s, [vo_ref[0, :, hv], vn_ref[0, :, hv]])


def _mla_sample(q, k_old, v_old, k_new, v_new):
    b, t, _ = q.shape

    def whole(a):
        return pl.BlockSpec((1,) + a.shape[1:], lambda bi: (bi, 0, 0))

    return pl.pallas_call(
        _mla_sample_kernel,
        grid=(b,),
        in_specs=[whole(q), whole(k_old), whole(v_old), whole(k_new), whole(v_new)],
        out_specs=pl.BlockSpec((1, t, MLA_WIDTH), lambda bi: (bi, 0, 0)),
        out_shape=jax.ShapeDtypeStruct((b, t, MLA_WIDTH), BF16),
        compiler_params=_params(1), name="mla_sample")(q, k_old, v_old, k_new, v_new)


def _mix_kernel(xn_ref, oa_ref, ob_ref, wga_ref, wgb_ref, wpa_ref, wpb_ref, o_ref):
    xn = xn_ref[...]
    a = jax.nn.sigmoid(_dot(xn, wga_ref[...])) * _dot(oa_ref[...], wpa_ref[...])
    b = jax.nn.sigmoid(_dot(xn, wgb_ref[...])) * _dot(ob_ref[...], wpb_ref[...])
    o_ref[...] = (a + b).astype(BF16)


def _mix(xn, oa, ob, wga, wgb, wpa, wpb, tm, tn=1024):
    m = xn.shape[0]
    return pl.pallas_call(
        _mix_kernel,
        grid=(m // tm, D_MODEL // tn),
        in_specs=[pl.BlockSpec((tm, D_MODEL), lambda i, j: (i, 0)),
                  pl.BlockSpec((tm, BAND_WIDTH), lambda i, j: (i, 0)),
                  pl.BlockSpec((tm, MLA_WIDTH), lambda i, j: (i, 0)),
                  pl.BlockSpec((D_MODEL, tn), lambda i, j: (0, j)),
                  pl.BlockSpec((D_MODEL, tn), lambda i, j: (0, j)),
                  pl.BlockSpec((BAND_WIDTH, tn), lambda i, j: (0, j)),
                  pl.BlockSpec((MLA_WIDTH, tn), lambda i, j: (0, j))],
        out_specs=pl.BlockSpec((tm, tn), lambda i, j: (i, j)),
        out_shape=jax.ShapeDtypeStruct((m, D_MODEL), BF16),
        compiler_params=_params(2, 62 * 1024 * 1024), name="mix")(xn, oa, ob, wga, wgb, wpa, wpb)


def _outproj_kernel(x_ref, mix_ref, w_ref, h_ref):
    h_ref[...] = x_ref[...] + _dot(mix_ref[...], w_ref[...])


def _outproj(x, mix, w, tm):
    m = x.shape[0]
    rows = pl.BlockSpec((tm, D_MODEL), lambda i: (i, 0))
    return pl.pallas_call(
        _outproj_kernel,
        grid=(m // tm,),
        in_specs=[rows, rows, pl.BlockSpec((D_MODEL, D_MODEL), lambda i: (0, 0))],
        out_specs=rows,
        out_shape=jax.ShapeDtypeStruct((m, D_MODEL), F32),
        compiler_params=_params(1), name="outproj")(x, mix, w)


def _ffn_kernel(h_ref, g_ref, wu_ref, wd_ref, o_ref, hn_ref):
    def update(hn):
        u = jnp.maximum(_dot(hn, wu_ref[...]), 0.0)
        return _dot((u * u).astype(BF16), wd_ref[...])

    @pl.when(pl.program_id(1) == 0)
    def _():
        h = h_ref[...]
        hn = _rms(h, g_ref[...]).astype(BF16)
        hn_ref[...] = hn
        o_ref[...] = h + update(hn)

    @pl.when(pl.program_id(1) > 0)
    def _():
        o_ref[...] += update(hn_ref[...])


def _ffn(h, g, wu, wd, tm, tf=1024):
    m = h.shape[0]
    return pl.pallas_call(
        _ffn_kernel,
        grid=(m // tm, D_FF // tf),
        in_specs=[pl.BlockSpec((tm, D_MODEL), lambda i, f: (i, 0)),
                  pl.BlockSpec((1, D_MODEL), lambda i, f: (0, 0)),
                  pl.BlockSpec((D_MODEL, tf), lambda i, f: (0, f)),
                  pl.BlockSpec((tf, D_MODEL), lambda i, f: (f, 0))],
        out_specs=pl.BlockSpec((tm, D_MODEL), lambda i, f: (i, 0)),
        out_shape=jax.ShapeDtypeStruct((m, D_MODEL), F32),
        scratch_shapes=[pltpu.VMEM((tm, D_MODEL), BF16)],
        compiler_params=_params(2, 62 * 1024 * 1024), name="ffn")(h, g, wu, wd)


def _rope_table(pos, rows):
    half = MLA_ROPE // 2
    freqs = ROPE_THETA ** (-(jnp.arange(half, dtype=F32) / half))
    ang = pos.astype(F32)[:, None] * freqs[None, :]
    cos = jnp.cos(ang)
    sin = jnp.sin(ang)
    table = jnp.concatenate([cos, cos, -sin, sin], axis=1)
    return jnp.tile(table, (rows // table.shape[0], 1))


def _with_partner(w):
    half = MLA_ROPE // 2
    return jnp.concatenate([w, w[..., half:], w[..., :half]], axis=-1)


def _layer_weights(l, norm_mix_g, w_in, g_aq, g_ak, g_kv, g_kr, g_qn, g_qr, g_kn,
                   w_kv_b, w_pa, w_pb, w_out, norm_ffn_g, w_up, w_down):
    w = w_in[l].astype(BF16)
    c = 0
    parts = []
    for width in (3 * BAND_WIDTH, MLA_HEADS * MLA_QK, MLA_KV_RANK, MLA_ROPE, D_MODEL, D_MODEL):
        parts.append(w[:, c:c + width])
        c += width
    w_a, w_bq, w_ckv, w_kr, w_ga, w_gb = parts
    w_q = jnp.concatenate(
        [piece for h in range(MLA_HEADS)
         for piece in (w_bq[:, h * MLA_QK:h * MLA_QK + MLA_NOPE],
                       _with_partner(w_bq[:, h * MLA_QK + MLA_NOPE:(h + 1) * MLA_QK]))], axis=1)
    row = lambda g: g[l][None, :].astype(F32)
    return dict(
        gx=row(norm_mix_g),
        w_in=w,
        w_q=w_q,
        w_kr=_with_partner(w_kr),
        g_a=jnp.concatenate([jnp.tile(g_aq[l] * BAND_QSCALE, BAND_HEADS), jnp.tile(g_ak[l], BAND_HEADS),
                             jnp.ones((BAND_WIDTH,), F32)])[None, :],
        g_ak=row(g_ak),
        g_qn=row(g_qn),
        g_qr=_with_partner(g_qr[l])[None, :],
        g_kv=row(g_kv),
        g_kr=_with_partner(g_kr[l])[None, :],
        w_kv_b=w_kv_b[l].astype(BF16),
        g_kn=row(g_kn),
        w_ga=w_ga, w_gb=w_gb,
        w_pa=w_pa[l].astype(BF16), w_pb=w_pb[l].astype(BF16),
        w_out=w_out[l].astype(BF16),
        g_ffn=row(norm_ffn_g),
        w_up=w_up[l].astype(BF16), w_down=w_down[l].astype(BF16),
    )


def _merge_ffn(x2, xn, oa, ob, p, tm):
    mix = _mix(xn, oa, ob, p["w_ga"], p["w_gb"], p["w_pa"], p["w_pb"], tm)
    h = _outproj(x2, mix, p["w_out"], min(tm, 512))
    return _ffn(h, p["g_ffn"], p["w_up"], p["w_down"], tm)


def _cache_shape(kv, lead):
    return tuple(a.reshape(lead + (BAND_HEADS, BAND_HEAD_DIM)) for a in kv)


def kernel(x_prompt, x_sample, cache_a_k, cache_a_v, cache_mla_ckv, cache_mla_krope, norm_mix_g, w_in, g_aq, g_ak, rel_bias, g_kv, g_kr, g_qn, g_qr, g_kn, w_kv_b, w_pa, w_pb, w_out, norm_ffn_g, w_up, w_down):
    b, s, _ = x_prompt.shape
    bs, t, _ = x_sample.shape
    past = cache_mla_ckv.shape[2]
    n_band = cache_a_k.shape[2]
    depth = w_in.shape[0]
    keep = min(BAND_PAST, s)
    tm_p = 1024
    tq_mla = 256
    tm_s = bs * t
    assert s % tm_p == 0 and s % BAND_TQ == 0 and s % tq_mla == 0 and t == CHUNK and past % CHUNK == 0
    tm_cache = min(bs * past, 2048)
    assert (bs * past) % tm_cache == 0
    tm_keep = min(keep, 512)
    assert keep % tm_keep == 0 and s % tm_keep == 0 and (s - keep) % tm_keep == 0
    cs_p = _rope_table(jnp.arange(s), max(s, tm_p))
    cs_s = _rope_table(past + jnp.arange(t), tm_s)

    yp = x_prompt.reshape(b * s, D_MODEL)
    ys = x_sample.reshape(bs * t, D_MODEL)
    outs = [[] for _ in range(8)]
    for l in range(depth):
        p = _layer_weights(l, norm_mix_g, w_in, g_aq, g_ak, g_kv, g_kr, g_qn, g_qr, g_kn,
                           w_kv_b, w_pa, w_pb, w_out, norm_ffn_g, w_up, w_down)
        bias, bias_t = _band_bias(rel_bias[l])

        xn, qkv, vt, q_mla, ckv, ckv_bf, krr, kr = _proj(yp, p, cs_p, tm_p)
        kb = keep // tm_keep
        ak, av = _cache_shape(_kv_f32(
            yp, p["gx"], p["w_in"], p["g_ak"], tm_keep, b * keep,
            lambda i: (i // kb) * (s // tm_keep) + (s - keep) // tm_keep + i % kb), (b, keep))
        oa = _band_prompt(qkv.reshape(b, s, 3 * BAND_WIDTH), vt, bias_t)
        k_mla, vt_mla = _expand(ckv_bf, p["w_kv_b"], p["g_kn"], krr, tm_p, True, tq_mla)
        ob = _mla_prompt(q_mla.reshape(b, s, -1), k_mla.reshape(b, s, -1),
                         vt_mla.reshape(b, s // tq_mla, MLA_WIDTH, tq_mla), tq_mla)
        for o, val in zip(outs[:4], (ak, av, ckv.reshape(b, s, MLA_KV_RANK), kr.reshape(b, s, MLA_ROPE))):
            o.append(val)
        yp = _merge_ffn(yp, xn, oa.reshape(b * s, BAND_WIDTH), ob.reshape(b * s, MLA_WIDTH), p, tm_p)

        xn, qkv, _, q_mla, ckv, ckv_bf, krr, kr = _proj(ys, p, cs_s, tm_s)
        ak, av = _cache_shape(_kv_f32(ys, p["gx"], p["w_in"], p["g_ak"], tm_s, bs * t, lambda i: i), (bs, t))
        oa = _band_sample(qkv.reshape(bs, t, 3 * BAND_WIDTH),
                          cache_a_k[l].reshape(bs, n_band * BAND_HEADS, BAND_HEAD_DIM),
                          cache_a_v[l].reshape(bs, n_band * BAND_HEADS, BAND_HEAD_DIM), bias)
        k_old, v_old = _expand(cache_mla_ckv[l].reshape(bs * past, MLA_KV_RANK), p["w_kv_b"], p["g_kn"],
                               cache_mla_krope[l].reshape(bs * past, MLA_ROPE), tm_cache, False)
        k_new, v_new = _expand(ckv_bf, p["w_kv_b"], p["g_kn"], krr, tm_s, False)
        ob = _mla_sample(q_mla.reshape(bs, t, -1), k_old.reshape(bs, past, -1), v_old.reshape(bs, past, -1),
                         k_new.reshape(bs, t, -1), v_new.reshape(bs, t, -1))
        for o, val in zip(outs[4:], (ak, av, ckv.reshape(bs, t, MLA_KV_RANK), kr.reshape(bs, t, MLA_ROPE))):
            o.append(val)
        ys = _merge_ffn(ys, xn, oa.reshape(bs * t, BAND_WIDTH), ob.reshape(bs * t, MLA_WIDTH), p, tm_s)

    return (yp.reshape(b, s, D_MODEL), ys.reshape(bs, t, D_MODEL)) + tuple(jnp.stack(o) for o in outs)
```

```python
import functools

import jax
import jax.numpy as jnp
from jax import lax
from jax.experimental import pallas as pl
from jax.experimental.pallas import tpu as pltpu

F32 = jnp.float32
BF16 = jnp.bfloat16

D_MODEL = 2048
CHUNK = 64
BAND_PAST = 8 * CHUNK
BAND_HEADS = 8
BAND_HEAD_DIM = 128
BAND_WIDTH = BAND_HEADS * BAND_HEAD_DIM
REL_MAX = 256
REL_SIZE = (CHUNK - 1) + REL_MAX + 1
BAND_SCALE = BAND_HEAD_DIM ** -0.5
MLA_HEADS = 8
MLA_NOPE = 128
MLA_ROPE = 64
MLA_QK = MLA_NOPE + MLA_ROPE
MLA_V = 128
MLA_WIDTH = MLA_HEADS * MLA_V
MLA_KV_RANK = 512
MLA_SCALE = MLA_QK ** -0.5
ROPE_THETA = 10000.0
D_FF = 4 * D_MODEL
EPS = 1e-6
NEG = -1e30
LOG2E = 1.4426950408889634
MLA_QSCALE = MLA_SCALE * LOG2E
BAND_QSCALE = BAND_SCALE * LOG2E

LANES = 128
MLA_QK_PAD = 2 * LANES
BAND_TQ = 256
BAND_WIN = BAND_TQ + BAND_PAST
BAND_PIECES = BAND_WIN // BAND_TQ
VMEM_LIMIT = 56 * 1024 * 1024
PROJ_TN = 512
PROJ_ROW_CHUNKS = 4
SUM_ROWS = 16
BAND_QB = 4
BAND_N_BIAS = min(BAND_QB, BAND_PIECES)
BAND_SCORE_BUFFERS = 4
MLA_SCORE_BUFFERS = 3


def _params(n_axes, vmem=VMEM_LIMIT):
    return pltpu.CompilerParams(dimension_semantics=("arbitrary",) * n_axes,
                                vmem_limit_bytes=vmem)


def _rms(a, g):
    return a * lax.rsqrt(jnp.mean(a * a, axis=-1, keepdims=True) + EPS) * g


def _dot(a, b):
    return jnp.dot(a, b, preferred_element_type=F32)


def _dot_nt(a, b):
    return lax.dot_general(a, b, (((1,), (1,)), ((), ())), preferred_element_type=F32)


def _rope_pair(ar, g, cs):
    lane = lax.broadcasted_iota(jnp.int32, (1, LANES), 1)
    first = (lane < MLA_ROPE).astype(F32)
    ss = jnp.sum(ar * ar * first, axis=-1, keepdims=True) / MLA_ROPE
    return ar * lax.rsqrt(ss + EPS) * g * cs


_NB_BAND_NORM = 2 * BAND_WIDTH // PROJ_TN
_NB_BAND = 3 * BAND_WIDTH // PROJ_TN
_NB_MLA_Q = MLA_HEADS * MLA_QK_PAD // PROJ_TN
_NB_PROJ = _NB_BAND + _NB_MLA_Q + 1
_CKV_BLOCK = (3 * BAND_WIDTH + MLA_HEADS * MLA_QK) // PROJ_TN
assert MLA_KV_RANK == PROJ_TN and _CKV_BLOCK * PROJ_TN == 3 * BAND_WIDTH + MLA_HEADS * MLA_QK


def _proj_kernel(x_ref, gx_ref, w_ref, wq_ref, wkr_ref, gh_ref, gqn_ref, gqr_ref, gkv_ref, gkr_ref, cs_ref,
                 xn_out_ref, qkv_ref, vt_ref, q_ref, ckv_ref, ckvb_ref, krr_ref, kr_ref, xn_ref):
    j = pl.program_id(1)
    tm = x_ref.shape[0]
    row_chunks = [slice(c * tm // PROJ_ROW_CHUNKS, (c + 1) * tm // PROJ_ROW_CHUNKS) for c in range(PROJ_ROW_CHUNKS)]

    def product(r, weights=w_ref):
        return _dot(xn_ref[r, :], weights[...])

    def band_norm(r, acc):
        for k in range(PROJ_TN // LANES):
            sl = slice(k * LANES, (k + 1) * LANES)
            qkv_ref[r, sl] = _rms(acc[:, sl], gh_ref[:, sl]).astype(BF16)

    @pl.when(j == 0)
    def _():
        for r in row_chunks:
            xn = _rms(x_ref[r, :], gx_ref[...]).astype(BF16)
            xn_ref[r, :] = xn
            xn_out_ref[r, :] = xn
            band_norm(r, _dot(xn, w_ref[...]))

    @pl.when(jnp.logical_and(j > 0, j < _NB_BAND_NORM))
    def _():
        for r in row_chunks:
            band_norm(r, product(r))

    @pl.when(jnp.logical_and(j >= _NB_BAND_NORM, j < _NB_BAND))
    def _():
        for r in row_chunks:
            acc = product(r)
            qkv_ref[r, :] = acc.astype(BF16)
            vt_ref[:, r] = acc.T.astype(BF16)

    @pl.when(jnp.logical_and(j >= _NB_BAND, j < _NB_BAND + _NB_MLA_Q))
    def _():
        for r in row_chunks:
            acc = product(r, wq_ref)
            cs = cs_ref[r, :]
            for k in range(PROJ_TN // MLA_QK_PAD):
                c0 = k * MLA_QK_PAD
                qn = _rms(acc[:, c0:c0 + LANES], gqn_ref[...]) * MLA_QSCALE
                qr = _rope_pair(acc[:, c0 + LANES:c0 + 2 * LANES], gqr_ref[...], cs) * MLA_QSCALE
                q_ref[r, c0:c0 + LANES] = qn.astype(BF16)
                q_ref[r, c0 + LANES:c0 + 2 * LANES] = qr.astype(BF16)

    @pl.when(j == _NB_BAND + _NB_MLA_Q)
    def _():
        for r in row_chunks:
            cn = _rms(product(r), gkv_ref[...])
            ckv_ref[r, :] = cn
            ckvb_ref[r, :] = cn.astype(BF16)
            t = _rope_pair(product(r, wkr_ref), gkr_ref[...], cs_ref[r, :])
            krr = t + pltpu.roll(t, MLA_ROPE, 1)
            krr_ref[r, :] = krr
            kr_ref[r, :] = krr[:, :MLA_ROPE]


def _proj(x, p, cs, tm):
    m = x.shape[0]
    tn = PROJ_TN
    n_cs = cs.shape[0] // tm
    const = lambda i, j: (0, 0)
    rows = lambda i, j: (i, 0)
    return pl.pallas_call(
        _proj_kernel,
        grid=(m // tm, _NB_PROJ),
        in_specs=[pl.BlockSpec((tm, D_MODEL), rows),
                  pl.BlockSpec((1, D_MODEL), const),
                  pl.BlockSpec((D_MODEL, tn), lambda i, j: (0, jnp.where(
                      j < _NB_BAND, j, jnp.where(j == _NB_PROJ - 1, _CKV_BLOCK, _NB_BAND - 1)))),
                  pl.BlockSpec((D_MODEL, tn), lambda i, j: (0, jnp.clip(j - _NB_BAND, 0, _NB_MLA_Q - 1))),
                  pl.BlockSpec((D_MODEL, LANES), const),
                  pl.BlockSpec((1, tn), lambda i, j: (0, jnp.minimum(j, _NB_BAND - 1))),
                  pl.BlockSpec((1, LANES), const),
                  pl.BlockSpec((1, LANES), const),
                  pl.BlockSpec((1, MLA_KV_RANK), const),
                  pl.BlockSpec((1, LANES), const),
                  pl.BlockSpec((tm, LANES), lambda i, j: (i % n_cs, 0))],
        out_specs=[pl.BlockSpec((tm, D_MODEL), rows),
                   pl.BlockSpec((tm, tn), lambda i, j: (i, jnp.minimum(j, _NB_BAND - 1))),
                   pl.BlockSpec((tn, tm), lambda i, j: (jnp.clip(j - _NB_BAND_NORM, 0, _NB_BAND - _NB_BAND_NORM - 1), i)),
                   pl.BlockSpec((tm, tn), lambda i, j: (i, jnp.clip(j - _NB_BAND, 0, _NB_MLA_Q - 1))),
                   pl.BlockSpec((tm, MLA_KV_RANK), rows),
                   pl.BlockSpec((tm, MLA_KV_RANK), rows),
                   pl.BlockSpec((tm, LANES), rows),
                   pl.BlockSpec((tm, MLA_ROPE), rows)],
        out_shape=[jax.ShapeDtypeStruct((m, D_MODEL), BF16),
                   jax.ShapeDtypeStruct((m, 3 * BAND_WIDTH), BF16),
                   jax.ShapeDtypeStruct((BAND_WIDTH, m), BF16),
                   jax.ShapeDtypeStruct((m, MLA_HEADS * MLA_QK_PAD), BF16),
                   jax.ShapeDtypeStruct((m, MLA_KV_RANK), F32),
                   jax.ShapeDtypeStruct((m, MLA_KV_RANK), BF16),
                   jax.ShapeDtypeStruct((m, LANES), F32),
                   jax.ShapeDtypeStruct((m, MLA_ROPE), F32)],
        scratch_shapes=[pltpu.VMEM((tm, D_MODEL), BF16)],
        compiler_params=_params(2, 62 * 1024 * 1024), name="proj")(
            x, p["gx"], p["w_in"], p["w_q"], p["w_kr"], p["g_a"], p["g_qn"], p["g_qr"], p["g_kv"], p["g_kr"], cs)


def _kv_f32_kernel(x_ref, gx_ref, w_ref, g_ref, k_ref, v_ref, xn_ref):
    j = pl.program_id(1)
    tm = x_ref.shape[0]

    def heads(o_ref, acc, norm):
        for h in range(BAND_HEADS):
            y = acc[:, h * BAND_HEAD_DIM:(h + 1) * BAND_HEAD_DIM]
            o_ref[pl.ds(h, tm, stride=BAND_HEADS), :] = _rms(y, g_ref[...]) if norm else y

    @pl.when(j == 0)
    def _():
        xn = _rms(x_ref[...], gx_ref[...]).astype(BF16)
        xn_ref[...] = xn
        heads(k_ref, _dot(xn, w_ref[...]), True)

    @pl.when(j == 1)
    def _():
        heads(v_ref, _dot(xn_ref[...], w_ref[...]), False)


def _kv_f32(x, gx, w_in, g, tm, n_rows, row_block):
    out = pl.BlockSpec((tm * BAND_HEADS, BAND_HEAD_DIM), lambda i, j: (i, 0))
    shape = jax.ShapeDtypeStruct((n_rows * BAND_HEADS, BAND_HEAD_DIM), F32)
    return pl.pallas_call(
        _kv_f32_kernel,
        grid=(n_rows // tm, 2),
        in_specs=[pl.BlockSpec((tm, D_MODEL), lambda i, j: (row_block(i), 0)),
                  pl.BlockSpec((1, D_MODEL), lambda i, j: (0, 0)),
                  pl.BlockSpec((D_MODEL, BAND_WIDTH), lambda i, j: (0, j + 1)),
                  pl.BlockSpec((1, LANES), lambda i, j: (0, 0))],
        out_specs=[out, out],
        out_shape=[shape, shape],
        scratch_shapes=[pltpu.VMEM((tm, D_MODEL), BF16)],
        compiler_params=_params(2), name="kv_f32")(x, gx, w_in, g)


def _expand_kernel(c_ref, w_ref, gkn_ref, krr_ref, k_ref, v_ref, *, v_transposed):
    c = c_ref[...].astype(BF16)
    krr = krr_ref[...].astype(BF16)
    if krr.shape[1] == MLA_ROPE:
        krr = jnp.concatenate([krr, krr], axis=1)
    for h in range(MLA_HEADS):
        a = _dot(c, w_ref[:, h * 2 * LANES:(h + 1) * 2 * LANES])
        k_ref[:, h * MLA_QK_PAD:h * MLA_QK_PAD + LANES] = _rms(a[:, :LANES], gkn_ref[...]).astype(BF16)
        k_ref[:, h * MLA_QK_PAD + LANES:(h + 1) * MLA_QK_PAD] = krr
        if v_transposed:
            slab = v_ref.shape[2]
            for n in range(v_ref.shape[0]):
                v_ref[n, h * MLA_V:(h + 1) * MLA_V, :] = a[n * slab:(n + 1) * slab, LANES:].T.astype(BF16)
        else:
            v_ref[:, h * MLA_V:(h + 1) * MLA_V] = a[:, LANES:].astype(BF16)


def _expand(c, w, gkn, krr, tm, v_transposed, slab=None):
    m = c.shape[0]
    if v_transposed:
        v_spec = pl.BlockSpec((tm // slab, MLA_WIDTH, slab), lambda i: (i, 0, 0))
        v_shape = jax.ShapeDtypeStruct((m // slab, MLA_WIDTH, slab), BF16)
    else:
        v_spec = pl.BlockSpec((tm, MLA_WIDTH), lambda i: (i, 0))
        v_shape = jax.ShapeDtypeStruct((m, MLA_WIDTH), BF16)
    return pl.pallas_call(
        functools.partial(_expand_kernel, v_transposed=v_transposed),
        grid=(m // tm,),
        in_specs=[pl.BlockSpec((tm, MLA_KV_RANK), lambda i: (i, 0)),
                  pl.BlockSpec(w.shape, lambda i: (0, 0)),
                  pl.BlockSpec((1, LANES), lambda i: (0, 0)),
                  pl.BlockSpec((tm, krr.shape[1]), lambda i: (i, 0))],
        out_specs=[pl.BlockSpec((tm, MLA_HEADS * MLA_QK_PAD), lambda i: (i, 0)), v_spec],
        out_shape=[jax.ShapeDtypeStruct((m, MLA_HEADS * MLA_QK_PAD), BF16), v_shape],
        compiler_params=_params(1), name="expand")(c, w, gkn, krr)


def _band_bias_kernel(b_ref, o_ref, ot_ref):
    t = pltpu.roll(jnp.broadcast_to(b_ref[0], (BAND_TQ, 2 * BAND_PAST)), 0, 1,
                   stride=1, stride_axis=0)[:, :BAND_WIN] * LOG2E
    qc = lax.broadcasted_iota(jnp.int32, (BAND_TQ, BAND_WIN), 0) // CHUNK
    kc = lax.broadcasted_iota(jnp.int32, (BAND_TQ, BAND_WIN), 1) // CHUNK
    t = jnp.where(kc >= qc, jnp.where(kc <= qc + BAND_PAST // CHUNK, t, NEG), NEG)
    o_ref[0] = t
    tt = t.T
    key = lax.broadcasted_iota(jnp.int32, (BAND_WIN, BAND_TQ), 0)
    for v in range(BAND_PIECES):
        n_missing = BAND_PIECES - 1 - v
        ot_ref[v, 0] = jnp.where(key >= n_missing * BAND_TQ, tt, NEG)


def _band_bias(rel_bias):
    far = jnp.broadcast_to(rel_bias[:, REL_SIZE - 1:], (BAND_HEADS, BAND_PAST - REL_MAX))
    near = jnp.broadcast_to(rel_bias[:, :1], (BAND_HEADS, BAND_WIN - BAND_PAST - CHUNK))
    wrap = jnp.broadcast_to(rel_bias[:, REL_SIZE - 1:], (BAND_HEADS, 2 * BAND_PAST - BAND_WIN))
    row0 = jnp.concatenate([far, rel_bias[:, ::-1], near, wrap], axis=1)
    assert row0.shape == (BAND_HEADS, 2 * BAND_PAST)
    return pl.pallas_call(
        _band_bias_kernel,
        grid=(BAND_HEADS,),
        in_specs=[pl.BlockSpec((1, 1, 2 * BAND_PAST), lambda h: (h, 0, 0))],
        out_specs=[pl.BlockSpec((1, BAND_TQ, BAND_WIN), lambda h: (h, 0, 0)),
                   pl.BlockSpec((BAND_PIECES, 1, BAND_WIN, BAND_TQ), lambda h: (0, h, 0, 0))],
        out_shape=[jax.ShapeDtypeStruct((BAND_HEADS, BAND_TQ, BAND_WIN), F32),
                   jax.ShapeDtypeStruct((BAND_PIECES, BAND_HEADS, BAND_WIN, BAND_TQ), F32)],
        compiler_params=_params(1), name="band_bias")(row0[:, None, :])


def _softmax_pv(s, v):
    m = functools.reduce(jnp.maximum, [jnp.max(sp, axis=-1, keepdims=True) for sp in s])
    e = [jnp.exp2(sp - m) for sp in s]
    l = functools.reduce(jnp.add, [jnp.sum(ep, axis=-1, keepdims=True) for ep in e])
    o = functools.reduce(jnp.add, [_dot(ep.astype(BF16), vp) for ep, vp in zip(e, v)])
    return (o / l).astype(BF16)


def _band_prompt_kernel(q_ref, *refs):
    n_piece = BAND_QB + BAND_PIECES - 1
    k_refs, vt_refs = refs[:n_piece], refs[n_piece:2 * n_piece]
    bias_refs = refs[2 * n_piece:2 * n_piece + BAND_N_BIAS]
    o_ref = refs[2 * n_piece + BAND_N_BIAS]
    bufs = refs[2 * n_piece + BAND_N_BIAS + 1:]

    def head(h):
        return slice(h * BAND_HEAD_DIM, (h + 1) * BAND_HEAD_DIM)

    def rows(n):
        return slice(n * BAND_TQ, (n + 1) * BAND_TQ)

    def scores(n, h):
        k = jnp.concatenate([r[0, :, head(h)] for r in k_refs[n:n + BAND_PIECES]], axis=0)
        return _dot_nt(k, q_ref[0, rows(n), head(h)]) + bias_refs[min(n, BAND_N_BIAS - 1)][0, h]

    units = [(n, h) for n in range(BAND_QB) for h in range(BAND_HEADS)]
    ahead = len(bufs) - 1
    for u in range(ahead):
        bufs[u][...] = scores(*units[u])
    for u, (n, h) in enumerate(units):
        if u + ahead < len(units):
            bufs[(u + ahead) % len(bufs)][...] = scores(*units[u + ahead])
        s = bufs[u % len(bufs)][...]
        vt = jnp.concatenate([r[head(h), :] for r in vt_refs[n:n + BAND_PIECES]], axis=1)
        vt = jnp.concatenate([vt, jnp.ones((SUM_ROWS, BAND_WIN), BF16)], axis=0)
        e = jnp.exp2(s - jnp.max(s, axis=0, keepdims=True))
        o = _dot(vt, e.astype(BF16))
        o = o[:BAND_HEAD_DIM] / o[BAND_HEAD_DIM:BAND_HEAD_DIM + 1]
        o_ref[0, rows(n), head(h)] = o.T.astype(BF16)


def _band_prompt(qkv, vt, bias_t):
    b, s, _ = qkv.shape
    nq = s // BAND_TQ
    assert nq % BAND_QB == 0
    n_piece = BAND_QB + BAND_PIECES - 1
    piece = (1, BAND_TQ, BAND_WIDTH)
    group = (1, BAND_QB * BAND_TQ, BAND_WIDTH)

    def first_block(i, p):
        return jnp.maximum(i * BAND_QB - (BAND_PIECES - 1) + p, 0)

    def k_spec(p):
        return pl.BlockSpec(piece, lambda bi, i: (bi, first_block(i, p), 1))

    def vt_spec(p):
        return pl.BlockSpec((BAND_WIDTH, BAND_TQ), lambda bi, i: (0, bi * nq + first_block(i, p)))

    def bias_spec(n):
        return pl.BlockSpec((1,) + bias_t.shape[1:],
                            lambda bi, i: (jnp.minimum(i * BAND_QB + n, BAND_PIECES - 1), 0, 0, 0))

    return pl.pallas_call(
        _band_prompt_kernel,
        grid=(b, nq // BAND_QB),
        in_specs=[pl.BlockSpec(group, lambda bi, i: (bi, i, 0))]
                 + [k_spec(p) for p in range(n_piece)]
                 + [vt_spec(p) for p in range(n_piece)]
                 + [bias_spec(n) for n in range(BAND_N_BIAS)],
        out_specs=pl.BlockSpec(group, lambda bi, i: (bi, i, 0)),
        out_shape=jax.ShapeDtypeStruct((b, s, BAND_WIDTH), BF16),
        scratch_shapes=[pltpu.VMEM((BAND_WIN, BAND_TQ), F32)] * BAND_SCORE_BUFFERS,
        compiler_params=_params(2, 62 * 1024 * 1024), name="band_prompt")(
            qkv, *([qkv] * n_piece), *([vt] * n_piece), *([bias_t] * BAND_N_BIAS))


def _band_sample_kernel(q_ref, kn_ref, vn_ref, kc_ref, vc_ref, bias_ref, o_ref, *, n_past):
    t = q_ref.shape[1]
    for h in range(BAND_HEADS):
        hs = slice(h * BAND_HEAD_DIM, (h + 1) * BAND_HEAD_DIM)
        q = q_ref[0, :, hs]
        k_old = kc_ref[0, pl.ds(h, n_past, stride=BAND_HEADS), :].astype(BF16)
        v_old = vc_ref[0, pl.ds(h, n_past, stride=BAND_HEADS), :].astype(BF16)
        s = [_dot_nt(q, k_old) + bias_ref[h, :t, :n_past],
             _dot_nt(q, kn_ref[0, :, hs]) + bias_ref[h, :t, n_past:n_past + t]]
        o_ref[0, :, hs] = _softmax_pv(s, [v_old, vn_ref[0, :, hs]])


def _band_sample(qkv, k_cache, v_cache, bias):
    b, t, _ = qkv.shape
    n_past = k_cache.shape[1] // BAND_HEADS
    new = (1, t, BAND_WIDTH)
    old = (1, n_past * BAND_HEADS, BAND_HEAD_DIM)
    return pl.pallas_call(
        functools.partial(_band_sample_kernel, n_past=n_past),
        grid=(b,),
        in_specs=[pl.BlockSpec(new, lambda bi: (bi, 0, 0)),
                  pl.BlockSpec(new, lambda bi: (bi, 0, 1)),
                  pl.BlockSpec(new, lambda bi: (bi, 0, 2)),
                  pl.BlockSpec(old, lambda bi: (bi, 0, 0)),
                  pl.BlockSpec(old, lambda bi: (bi, 0, 0)),
                  pl.BlockSpec(bias.shape, lambda bi: (0, 0, 0))],
        out_specs=pl.BlockSpec(new, lambda bi: (bi, 0, 0)),
        out_shape=jax.ShapeDtypeStruct((b, t, BAND_WIDTH), BF16),
        compiler_params=_params(1), name="band_sample")(qkv, qkv, qkv, k_cache, v_cache, bias)


def _mla_prompt_kernel(q_ref, k_ref, vt_ref, o_ref, mask_ref, *bufs, tq, nq):
    def scores(i):
        return _dot_nt(k_ref[0, :(i + 1) * tq, :], q_ref[0, i * tq:(i + 1) * tq, :])

    kc = lax.broadcasted_iota(jnp.int32, (tq, tq), 0) // CHUNK
    qc = lax.broadcasted_iota(jnp.int32, (tq, tq), 1) // CHUNK
    mask_ref[...] = jnp.where(kc <= qc, 0.0, NEG)

    ones = jnp.ones((SUM_ROWS, tq), BF16)
    ahead = len(bufs) - 1
    for i in range(min(ahead, nq)):
        bufs[i][:(i + 1) * tq] = scores(i)
    for i in range(nq):
        if i + ahead < nq:
            bufs[(i + ahead) % len(bufs)][:(i + ahead + 1) * tq] = scores(i + ahead)
        buf = bufs[i % len(bufs)]
        blocks = [buf[j * tq:(j + 1) * tq] for j in range(i)] + [buf[i * tq:(i + 1) * tq] + mask_ref[...]]
        m = functools.reduce(jnp.maximum, [jnp.max(sj, axis=0, keepdims=True) for sj in blocks])
        acc = functools.reduce(jnp.add, [
            _dot(jnp.concatenate([vt_ref[0, j], ones], axis=0), jnp.exp2(sj - m).astype(BF16))
            for j, sj in enumerate(blocks)])
        o_ref[0, i * tq:(i + 1) * tq, :] = (acc[:MLA_V] / acc[MLA_V:MLA_V + 1]).T.astype(BF16)


def _mla_prompt(q, k, vt, tq):
    b, s, _ = q.shape
    nq = s // tq
    return pl.pallas_call(
        functools.partial(_mla_prompt_kernel, tq=tq, nq=nq),
        grid=(b, MLA_HEADS),
        in_specs=[pl.BlockSpec((1, s, MLA_QK_PAD), lambda bi, h: (bi, 0, h)),
                  pl.BlockSpec((1, s, MLA_QK_PAD), lambda bi, h: (bi, 0, h)),
                  pl.BlockSpec((1, nq, MLA_V, tq), lambda bi, h: (bi, 0, h, 0))],
        out_specs=pl.BlockSpec((1, s, MLA_V), lambda bi, h: (bi, 0, h)),
        out_shape=jax.ShapeDtypeStruct((b, s, MLA_WIDTH), BF16),
        scratch_shapes=[pltpu.VMEM((tq, tq), F32)] + [pltpu.VMEM((s, tq), F32)] * MLA_SCORE_BUFFERS,
        compiler_params=_params(2), name="mla_prompt")(q, k, vt)


def _mla_sample_kernel(q_ref, ko_ref, vo_ref, kn_ref, vn_ref, o_ref):
    for h in range(MLA_HEADS):
        qk = slice(h * MLA_QK_PAD, (h + 1) * MLA_QK_PAD)
        hv = slice(h * MLA_V, (h + 1) * MLA_V)
        q = q_ref[0, :, qk]
        s = [_dot_nt(q, ko_ref[0, :, qk]), _dot_nt(q, kn_ref[0, :, qk])]
        o_ref[0, :, hv] = _softmax_pv(s, [vo_ref[0, :, hv], vn_ref[0, :, hv]])


def _mla_sample(q, k_old, v_old, k_new, v_new):
    b, t, _ = q.shape

    def whole(a):
        return pl.BlockSpec((1,) + a.shape[1:], lambda bi: (bi, 0, 0))

    return pl.pallas_call(
        _mla_sample_kernel,
        grid=(b,),
        in_specs=[whole(q), whole(k_old), whole(v_old), whole(k_new), whole(v_new)],
        out_specs=pl.BlockSpec((1, t, MLA_WIDTH), lambda bi: (bi, 0, 0)),
        out_shape=jax.ShapeDtypeStruct((b, t, MLA_WIDTH), BF16),
        compiler_params=_params(1), name="mla_sample")(q, k_old, v_old, k_new, v_new)


def _mix_kernel(xn_ref, oa_ref, ob_ref, wga_ref, wgb_ref, wpa_ref, wpb_ref, o_ref):
    xn = xn_ref[...]
    a = jax.nn.sigmoid(_dot(xn, wga_ref[...])) * _dot(oa_ref[...], wpa_ref[...])
    b = jax.nn.sigmoid(_dot(xn, wgb_ref[...])) * _dot(ob_ref[...], wpb_ref[...])
    o_ref[...] = (a + b).astype(BF16)


def _mix(xn, oa, ob, wga, wgb, wpa, wpb, tm, tn=1024):
    m = xn.shape[0]
    return pl.pallas_call(
        _mix_kernel,
        grid=(m // tm, D_MODEL // tn),
        in_specs=[pl.BlockSpec((tm, D_MODEL), lambda i, j: (i, 0)),
                  pl.BlockSpec((tm, BAND_WIDTH), lambda i, j: (i, 0)),
                  pl.BlockSpec((tm, MLA_WIDTH), lambda i, j: (i, 0)),
                  pl.BlockSpec((D_MODEL, tn), lambda i, j: (0, j)),
                  pl.BlockSpec((D_MODEL, tn), lambda i, j: (0, j)),
                  pl.BlockSpec((BAND_WIDTH, tn), lambda i, j: (0, j)),
                  pl.BlockSpec((MLA_WIDTH, tn), lambda i, j: (0, j))],
        out_specs=pl.BlockSpec((tm, tn), lambda i, j: (i, j)),
        out_shape=jax.ShapeDtypeStruct((m, D_MODEL), BF16),
        compiler_params=_params(2, 62 * 1024 * 1024), name="mix")(xn, oa, ob, wga, wgb, wpa, wpb)


def _outproj_kernel(x_ref, mix_ref, w_ref, h_ref):
    h_ref[...] = x_ref[...] + _dot(mix_ref[...], w_ref[...])


def _outproj(x, mix, w, tm):
    m = x.shape[0]
    rows = pl.BlockSpec((tm, D_MODEL), lambda i: (i, 0))
    return pl.pallas_call(
        _outproj_kernel,
        grid=(m // tm,),
        in_specs=[rows, rows, pl.BlockSpec((D_MODEL, D_MODEL), lambda i: (0, 0))],
        out_specs=rows,
        out_shape=jax.ShapeDtypeStruct((m, D_MODEL), F32),
        compiler_params=_params(1), name="outproj")(x, mix, w)


def _ffn_kernel(h_ref, g_ref, wu_ref, wd_ref, o_ref, hn_ref):
    def update(hn):
        u = jnp.maximum(_dot(hn, wu_ref[...]), 0.0)
        return _dot((u * u).astype(BF16), wd_ref[...])

    @pl.when(pl.program_id(1) == 0)
    def _():
        h = h_ref[...]
        hn = _rms(h, g_ref[...]).astype(BF16)
        hn_ref[...] = hn
        o_ref[...] = h + update(hn)

    @pl.when(pl.program_id(1) > 0)
    def _():
        o_ref[...] += update(hn_ref[...])


def _ffn(h, g, wu, wd, tm, tf=1024):
    m = h.shape[0]
    return pl.pallas_call(
        _ffn_kernel,
        grid=(m // tm, D_FF // tf),
        in_specs=[pl.BlockSpec((tm, D_MODEL), lambda i, f: (i, 0)),
                  pl.BlockSpec((1, D_MODEL), lambda i, f: (0, 0)),
                  pl.BlockSpec((D_MODEL, tf), lambda i, f: (0, f)),
                  pl.BlockSpec((tf, D_MODEL), lambda i, f: (f, 0))],
        out_specs=pl.BlockSpec((tm, D_MODEL), lambda i, f: (i, 0)),
        out_shape=jax.ShapeDtypeStruct((m, D_MODEL), F32),
        scratch_shapes=[pltpu.VMEM((tm, D_MODEL), BF16)],
        compiler_params=_params(2, 62 * 1024 * 1024), name="ffn")(h, g, wu, wd)


def _rope_table(pos, rows):
    half = MLA_ROPE // 2
    freqs = ROPE_THETA ** (-(jnp.arange(half, dtype=F32) / half))
    ang = pos.astype(F32)[:, None] * freqs[None, :]
    cos = jnp.cos(ang)
    sin = jnp.sin(ang)
    table = jnp.concatenate([cos, cos, -sin, sin], axis=1)
    return jnp.tile(table, (rows // table.shape[0], 1))


def _with_partner(w):
    half = MLA_ROPE // 2
    return jnp.concatenate([w, w[..., half:], w[..., :half]], axis=-1)


def _layer_weights(l, norm_mix_g, w_in, g_aq, g_ak, g_kv, g_kr, g_qn, g_qr, g_kn,
                   w_kv_b, w_pa, w_pb, w_out, norm_ffn_g, w_up, w_down):
    w = w_in[l].astype(BF16)
    c = 0
    parts = []
    for width in (3 * BAND_WIDTH, MLA_HEADS * MLA_QK, MLA_KV_RANK, MLA_ROPE, D_MODEL, D_MODEL):
        parts.append(w[:, c:c + width])
        c += width
    w_a, w_bq, w_ckv, w_kr, w_ga, w_gb = parts
    w_q = jnp.concatenate(
        [piece for h in range(MLA_HEADS)
         for piece in (w_bq[:, h * MLA_QK:h * MLA_QK + MLA_NOPE],
                       _with_partner(w_bq[:, h * MLA_QK + MLA_NOPE:(h + 1) * MLA_QK]))], axis=1)
    row = lambda g: g[l][None, :].astype(F32)
    return dict(
        gx=row(norm_mix_g),
        w_in=w,
        w_q=w_q,
        w_kr=_with_partner(w_kr),
        g_a=jnp.concatenate([jnp.tile(g_aq[l] * BAND_QSCALE, BAND_HEADS), jnp.tile(g_ak[l], BAND_HEADS),
                             jnp.ones((BAND_WIDTH,), F32)])[None, :],
        g_ak=row(g_ak),
        g_qn=row(g_qn),
        g_qr=_with_partner(g_qr[l])[None, :],
        g_kv=row(g_kv),
        g_kr=_with_partner(g_kr[l])[None, :],
        w_kv_b=w_kv_b[l].astype(BF16),
        g_kn=row(g_kn),
        w_ga=w_ga, w_gb=w_gb,
        w_pa=w_pa[l].astype(BF16), w_pb=w_pb[l].astype(BF16),
        w_out=w_out[l].astype(BF16),
        g_ffn=row(norm_ffn_g),
        w_up=w_up[l].astype(BF16), w_down=w_down[l].astype(BF16),
    )


def _merge_ffn(x2, xn, oa, ob, p, tm):
    mix = _mix(xn, oa, ob, p["w_ga"], p["w_gb"], p["w_pa"], p["w_pb"], tm)
    h = _outproj(x2, mix, p["w_out"], min(tm, 512))
    return _ffn(h, p["g_ffn"], p["w_up"], p["w_down"], tm)


def _cache_shape(kv, lead):
    return tuple(a.reshape(lead + (BAND_HEADS, BAND_HEAD_DIM)) for a in kv)


def kernel(x_prompt, x_sample, cache_a_k, cache_a_v, cache_mla_ckv, cache_mla_krope, norm_mix_g, w_in, g_aq, g_ak, rel_bias, g_kv, g_kr, g_qn, g_qr, g_kn, w_kv_b, w_pa, w_pb, w_out, norm_ffn_g, w_up, w_down):
    b, s, _ = x_prompt.shape
    bs, t, _ = x_sample.shape
    past = cache_mla_ckv.shape[2]
    n_band = cache_a_k.shape[2]
    depth = w_in.shape[0]
    keep = min(BAND_PAST, s)
    tm_p = 1024
    tq_mla = 256
    tm_s = bs * t
    assert s % tm_p == 0 and s % BAND_TQ == 0 and s % tq_mla == 0 and t == CHUNK and past % CHUNK == 0
    tm_cache = min(bs * past, 2048)
    assert (bs * past) % tm_cache == 0
    tm_keep = min(keep, 512)
    assert keep % tm_keep == 0 and s % tm_keep == 0 and (s - keep) % tm_keep == 0
    cs_p = _rope_table(jnp.arange(s), max(s, tm_p))
    cs_s = _rope_table(past + jnp.arange(t), tm_s)

    yp = x_prompt.reshape(b * s, D_MODEL)
    ys = x_sample.reshape(bs * t, D_MODEL)
    outs = [[] for _ in range(8)]
    for l in range(depth):
        p = _layer_weights(l, norm_mix_g, w_in, g_aq, g_ak, g_kv, g_kr, g_qn, g_qr, g_kn,
                           w_kv_b, w_pa, w_pb, w_out, norm_ffn_g, w_up, w_down)
        bias, bias_t = _band_bias(rel_bias[l])

        xn, qkv, vt, q_mla, ckv, ckv_bf, krr, kr = _proj(yp, p, cs_p, tm_p)
        kb = keep // tm_keep
        ak, av = _cache_shape(_kv_f32(
            yp, p["gx"], p["w_in"], p["g_ak"], tm_keep, b * keep,
            lambda i: (i // kb) * (s // tm_keep) + (s - keep) // tm_keep + i % kb), (b, keep))
        oa = _band_prompt(qkv.reshape(b, s, 3 * BAND_WIDTH), vt, bias_t)
        k_mla, vt_mla = _expand(ckv_bf, p["w_kv_b"], p["g_kn"], krr, tm_p, True, tq_mla)
        ob = _mla_prompt(q_mla.reshape(b, s, -1), k_mla.reshape(b, s, -1),
                         vt_mla.reshape(b, s // tq_mla, MLA_WIDTH, tq_mla), tq_mla)
        for o, val in zip(outs[:4], (ak, av, ckv.reshape(b, s, MLA_KV_RANK), kr.reshape(b, s, MLA_ROPE))):
            o.append(val)
        yp = _merge_ffn(yp, xn, oa.reshape(b * s, BAND_WIDTH), ob.reshape(b * s, MLA_WIDTH), p, tm_p)

        xn, qkv, _, q_mla, ckv, ckv_bf, krr, kr = _proj(ys, p, cs_s, tm_s)
        ak, av = _cache_shape(_kv_f32(ys, p["gx"], p["w_in"], p["g_ak"], tm_s, bs * t, lambda i: i), (bs, t))
        oa = _band_sample(qkv.reshape(bs, t, 3 * BAND_WIDTH),
                          cache_a_k[l].reshape(bs, n_band * BAND_HEADS, BAND_HEAD_DIM),
                          cache_a_v[l].reshape(bs, n_band * BAND_HEADS, BAND_HEAD_DIM), bias)
        k_old, v_old = _expand(cache_mla_ckv[l].reshape(bs * past, MLA_KV_RANK), p["w_kv_b"], p["g_kn"],
                               cache_mla_krope[l].reshape(bs * past, MLA_ROPE), tm_cache, False)
        k_new, v_new = _expand(ckv_bf, p["w_kv_b"], p["g_kn"], krr, tm_s, False)
        ob = _mla_sample(q_mla.reshape(bs, t, -1), k_old.reshape(bs, past, -1), v_old.reshape(bs, past, -1),
                         k_new.reshape(bs, t, -1), v_new.reshape(bs, t, -1))
        for o, val in zip(outs[4:], (ak, av, ckv.reshape(bs, t, MLA_KV_RANK), kr.reshape(bs, t, MLA_ROPE))):
            o.append(val)
        ys = _merge_ffn(ys, xn, oa.reshape(bs * t, BAND_WIDTH), ob.reshape(bs * t, MLA_WIDTH), p, tm_s)

    return (yp.reshape(b, s, D_MODEL), ys.reshape(bs, t, D_MODEL)) + tuple(jnp.stack(o) for o in outs)
```

```python
import functools

import jax
import jax.numpy as jnp
from jax import lax
from jax.experimental import pallas as pl
from jax.experimental.pallas import tpu as pltpu

F32 = jnp.float32
BF16 = jnp.bfloat16

D_MODEL = 2048
CHUNK = 64
BAND_PAST = 8 * CHUNK
BAND_HEADS = 8
BAND_HEAD_DIM = 128
BAND_WIDTH = BAND_HEADS * BAND_HEAD_DIM
REL_MAX = 256
REL_SIZE = (CHUNK - 1) + REL_MAX + 1
BAND_SCALE = BAND_HEAD_DIM ** -0.5
MLA_HEADS = 8
MLA_NOPE = 128
MLA_ROPE = 64
MLA_QK = MLA_NOPE + MLA_ROPE
MLA_V = 128
MLA_WIDTH = MLA_HEADS * MLA_V
MLA_KV_RANK = 512
MLA_SCALE = MLA_QK ** -0.5
ROPE_THETA = 10000.0
D_FF = 4 * D_MODEL
EPS = 1e-6
NEG = -1e30
LOG2E = 1.4426950408889634
MLA_QSCALE = MLA_SCALE * LOG2E
BAND_QSCALE = BAND_SCALE * LOG2E

LANES = 128
MLA_QK_PAD = 2 * LANES
BAND_TQ = 256
BAND_WIN = BAND_TQ + BAND_PAST
BAND_PIECES = BAND_WIN // BAND_TQ
VMEM_LIMIT = 56 * 1024 * 1024
PROJ_TN = 512
PROJ_ROW_CHUNKS = 4
SUM_ROWS = 16
BAND_QB = 4
BAND_N_BIAS = min(BAND_QB, BAND_PIECES)
BAND_SCORE_BUFFERS = 3
MLA_SCORE_BUFFERS = 3


def _params(n_axes, vmem=VMEM_LIMIT):
    return pltpu.CompilerParams(dimension_semantics=("arbitrary",) * n_axes,
                                vmem_limit_bytes=vmem)


def _rms(a, g):
    return a * lax.rsqrt(jnp.mean(a * a, axis=-1, keepdims=True) + EPS) * g


def _dot(a, b):
    return jnp.dot(a, b, preferred_element_type=F32)


def _dot_nt(a, b):
    return lax.dot_general(a, b, (((1,), (1,)), ((), ())), preferred_element_type=F32)


def _rope_pair(ar, g, cs):
    lane = lax.broadcasted_iota(jnp.int32, (1, LANES), 1)
    first = (lane < MLA_ROPE).astype(F32)
    ss = jnp.sum(ar * ar * first, axis=-1, keepdims=True) / MLA_ROPE
    return ar * lax.rsqrt(ss + EPS) * g * cs


_NB_BAND_NORM = 2 * BAND_WIDTH // PROJ_TN
_NB_BAND = 3 * BAND_WIDTH // PROJ_TN
_NB_MLA_Q = MLA_HEADS * MLA_QK_PAD // PROJ_TN
_NB_PROJ = _NB_BAND + _NB_MLA_Q + 1
_CKV_BLOCK = (3 * BAND_WIDTH + MLA_HEADS * MLA_QK) // PROJ_TN
assert MLA_KV_RANK == PROJ_TN and _CKV_BLOCK * PROJ_TN == 3 * BAND_WIDTH + MLA_HEADS * MLA_QK


def _proj_kernel(x_ref, gx_ref, w_ref, wq_ref, wkr_ref, gh_ref, gqn_ref, gqr_ref, gkv_ref, gkr_ref, cs_ref,
                 xn_out_ref, qkv_ref, vt_ref, q_ref, ckv_ref, ckvb_ref, krr_ref, kr_ref, xn_ref):
    j = pl.program_id(1)
    tm = x_ref.shape[0]
    row_chunks = [slice(c * tm // PROJ_ROW_CHUNKS, (c + 1) * tm // PROJ_ROW_CHUNKS) for c in range(PROJ_ROW_CHUNKS)]

    def product(r, weights=w_ref):
        return _dot(xn_ref[r, :], weights[...])

    def band_norm(r, acc):
        for k in range(PROJ_TN // LANES):
            sl = slice(k * LANES, (k + 1) * LANES)
            qkv_ref[r, sl] = _rms(acc[:, sl], gh_ref[:, sl]).astype(BF16)

    @pl.when(j == 0)
    def _():
        for r in row_chunks:
            xn = _rms(x_ref[r, :], gx_ref[...]).astype(BF16)
            xn_ref[r, :] = xn
            xn_out_ref[r, :] = xn
            band_norm(r, _dot(xn, w_ref[...]))

    @pl.when(jnp.logical_and(j > 0, j < _NB_BAND_NORM))
    def _():
        for r in row_chunks:
            band_norm(r, product(r))

    @pl.when(jnp.logical_and(j >= _NB_BAND_NORM, j < _NB_BAND))
    def _():
        for r in row_chunks:
            acc = product(r)
            qkv_ref[r, :] = acc.astype(BF16)
            vt_ref[:, r] = acc.T.astype(BF16)

    @pl.when(jnp.logical_and(j >= _NB_BAND, j < _NB_BAND + _NB_MLA_Q))
    def _():
        for r in row_chunks:
            acc = product(r, wq_ref)
            cs = cs_ref[r, :]
            for k in range(PROJ_TN // MLA_QK_PAD):
                c0 = k * MLA_QK_PAD
                qn = _rms(acc[:, c0:c0 + LANES], gqn_ref[...]) * MLA_QSCALE
                qr = _rope_pair(acc[:, c0 + LANES:c0 + 2 * LANES], gqr_ref[...], cs) * MLA_QSCALE
                q_ref[r, c0:c0 + LANES] = qn.astype(BF16)
                q_ref[r, c0 + LANES:c0 + 2 * LANES] = qr.astype(BF16)

    @pl.when(j == _NB_BAND + _NB_MLA_Q)
    def _():
        for r in row_chunks:
            cn = _rms(product(r), gkv_ref[...])
            ckv_ref[r, :] = cn
            ckvb_ref[r, :] = cn.astype(BF16)
            t = _rope_pair(product(r, wkr_ref), gkr_ref[...], cs_ref[r, :])
            krr = t + pltpu.roll(t, MLA_ROPE, 1)
            krr_ref[r, :] = krr
            kr_ref[r, :] = krr[:, :MLA_ROPE]


def _proj(x, p, cs, tm):
    m = x.shape[0]
    tn = PROJ_TN
    n_cs = cs.shape[0] // tm
    const = lambda i, j: (0, 0)
    rows = lambda i, j: (i, 0)
    return pl.pallas_call(
        _proj_kernel,
        grid=(m // tm, _NB_PROJ),
        in_specs=[pl.BlockSpec((tm, D_MODEL), rows),
                  pl.BlockSpec((1, D_MODEL), const),
                  pl.BlockSpec((D_MODEL, tn), lambda i, j: (0, jnp.where(
                      j < _NB_BAND, j, jnp.where(j == _NB_PROJ - 1, _CKV_BLOCK, _NB_BAND - 1)))),
                  pl.BlockSpec((D_MODEL, tn), lambda i, j: (0, jnp.clip(j - _NB_BAND, 0, _NB_MLA_Q - 1))),
                  pl.BlockSpec((D_MODEL, LANES), const),
                  pl.BlockSpec((1, tn), lambda i, j: (0, jnp.minimum(j, _NB_BAND - 1))),
                  pl.BlockSpec((1, LANES), const),
                  pl.BlockSpec((1, LANES), const),
                  pl.BlockSpec((1, MLA_KV_RANK), const),
                  pl.BlockSpec((1, LANES), const),
                  pl.BlockSpec((tm, LANES), lambda i, j: (i % n_cs, 0))],
        out_specs=[pl.BlockSpec((tm, D_MODEL), rows),
                   pl.BlockSpec((tm, tn), lambda i, j: (i, jnp.minimum(j, _NB_BAND - 1))),
                   pl.BlockSpec((tn, tm), lambda i, j: (jnp.clip(j - _NB_BAND_NORM, 0, _NB_BAND - _NB_BAND_NORM - 1), i)),
                   pl.BlockSpec((tm, tn), lambda i, j: (i, jnp.clip(j - _NB_BAND, 0, _NB_MLA_Q - 1))),
                   pl.BlockSpec((tm, MLA_KV_RANK), rows),
                   pl.BlockSpec((tm, MLA_KV_RANK), rows),
                   pl.BlockSpec((tm, LANES), rows),
                   pl.BlockSpec((tm, MLA_ROPE), rows)],
        out_shape=[jax.ShapeDtypeStruct((m, D_MODEL), BF16),
                   jax.ShapeDtypeStruct((m, 3 * BAND_WIDTH), BF16),
                   jax.ShapeDtypeStruct((BAND_WIDTH, m), BF16),
                   jax.ShapeDtypeStruct((m, MLA_HEADS * MLA_QK_PAD), BF16),
                   jax.ShapeDtypeStruct((m, MLA_KV_RANK), F32),
                   jax.ShapeDtypeStruct((m, MLA_KV_RANK), BF16),
                   jax.ShapeDtypeStruct((m, LANES), F32),
                   jax.ShapeDtypeStruct((m, MLA_ROPE), F32)],
        scratch_shapes=[pltpu.VMEM((tm, D_MODEL), BF16)],
        compiler_params=_params(2, 62 * 1024 * 1024), name="proj")(
            x, p["gx"], p["w_in"], p["w_q"], p["w_kr"], p["g_a"], p["g_qn"], p["g_qr"], p["g_kv"], p["g_kr"], cs)


def _kv_f32_kernel(x_ref, gx_ref, w_ref, g_ref, k_ref, v_ref, xn_ref):
    j = pl.program_id(1)
    tm = x_ref.shape[0]

    def heads(o_ref, acc, norm):
        for h in range(BAND_HEADS):
            y = acc[:, h * BAND_HEAD_DIM:(h + 1) * BAND_HEAD_DIM]
            o_ref[pl.ds(h, tm, stride=BAND_HEADS), :] = _rms(y, g_ref[...]) if norm else y

    @pl.when(j == 0)
    def _():
        xn = _rms(x_ref[...], gx_ref[...]).astype(BF16)
        xn_ref[...] = xn
        heads(k_ref, _dot(xn, w_ref[...]), True)

    @pl.when(j == 1)
    def _():
        heads(v_ref, _dot(xn_ref[...], w_ref[...]), False)


def _kv_f32(x, gx, w_in, g, tm, n_rows, row_block):
    out = pl.BlockSpec((tm * BAND_HEADS, BAND_HEAD_DIM), lambda i, j: (i, 0))
    shape = jax.ShapeDtypeStruct((n_rows * BAND_HEADS, BAND_HEAD_DIM), F32)
    return pl.pallas_call(
        _kv_f32_kernel,
        grid=(n_rows // tm, 2),
        in_specs=[pl.BlockSpec((tm, D_MODEL), lambda i, j: (row_block(i), 0)),
                  pl.BlockSpec((1, D_MODEL), lambda i, j: (0, 0)),
                  pl.BlockSpec((D_MODEL, BAND_WIDTH), lambda i, j: (0, j + 1)),
                  pl.BlockSpec((1, LANES), lambda i, j: (0, 0))],
        out_specs=[out, out],
        out_shape=[shape, shape],
        scratch_shapes=[pltpu.VMEM((tm, D_MODEL), BF16)],
        compiler_params=_params(2), name="kv_f32")(x, gx, w_in, g)


def _expand_kernel(c_ref, w_ref, gkn_ref, krr_ref, k_ref, v_ref, *, v_transposed):
    c = c_ref[...].astype(BF16)
    krr = krr_ref[...].astype(BF16)
    if krr.shape[1] == MLA_ROPE:
        krr = jnp.concatenate([krr, krr], axis=1)
    for h in range(MLA_HEADS):
        a = _dot(c, w_ref[:, h * 2 * LANES:(h + 1) * 2 * LANES])
        k_ref[:, h * MLA_QK_PAD:h * MLA_QK_PAD + LANES] = _rms(a[:, :LANES], gkn_ref[...]).astype(BF16)
        k_ref[:, h * MLA_QK_PAD + LANES:(h + 1) * MLA_QK_PAD] = krr
        if v_transposed:
            slab = v_ref.shape[2]
            for n in range(v_ref.shape[0]):
                v_ref[n, h * MLA_V:(h + 1) * MLA_V, :] = a[n * slab:(n + 1) * slab, LANES:].T.astype(BF16)
        else:
            v_ref[:, h * MLA_V:(h + 1) * MLA_V] = a[:, LANES:].astype(BF16)


def _expand(c, w, gkn, krr, tm, v_transposed, slab=None):
    m = c.shape[0]
    if v_transposed:
        v_spec = pl.BlockSpec((tm // slab, MLA_WIDTH, slab), lambda i: (i, 0, 0))
        v_shape = jax.ShapeDtypeStruct((m // slab, MLA_WIDTH, slab), BF16)
    else:
        v_spec = pl.BlockSpec((tm, MLA_WIDTH), lambda i: (i, 0))
        v_shape = jax.ShapeDtypeStruct((m, MLA_WIDTH), BF16)
    return pl.pallas_call(
        functools.partial(_expand_kernel, v_transposed=v_transposed),
        grid=(m // tm,),
        in_specs=[pl.BlockSpec((tm, MLA_KV_RANK), lambda i: (i, 0)),
                  pl.BlockSpec(w.shape, lambda i: (0, 0)),
                  pl.BlockSpec((1, LANES), lambda i: (0, 0)),
                  pl.BlockSpec((tm, krr.shape[1]), lambda i: (i, 0))],
        out_specs=[pl.BlockSpec((tm, MLA_HEADS * MLA_QK_PAD), lambda i: (i, 0)), v_spec],
        out_shape=[jax.ShapeDtypeStruct((m, MLA_HEADS * MLA_QK_PAD), BF16), v_shape],
        compiler_params=_params(1), name="expand")(c, w, gkn, krr)


def _band_bias_kernel(b_ref, o_ref, ot_ref):
    t = pltpu.roll(jnp.broadcast_to(b_ref[0], (BAND_TQ, 2 * BAND_PAST)), 0, 1,
                   stride=1, stride_axis=0)[:, :BAND_WIN] * LOG2E
    qc = lax.broadcasted_iota(jnp.int32, (BAND_TQ, BAND_WIN), 0) // CHUNK
    kc = lax.broadcasted_iota(jnp.int32, (BAND_TQ, BAND_WIN), 1) // CHUNK
    t = jnp.where(kc >= qc, jnp.where(kc <= qc + BAND_PAST // CHUNK, t, NEG), NEG)
    o_ref[0] = t
    tt = t.T
    key = lax.broadcasted_iota(jnp.int32, (BAND_WIN, BAND_TQ), 0)
    for v in range(BAND_PIECES):
        n_missing = BAND_PIECES - 1 - v
        ot_ref[v, 0] = jnp.where(key >= n_missing * BAND_TQ, tt, NEG)


def _band_bias(rel_bias):
    far = jnp.broadcast_to(rel_bias[:, REL_SIZE - 1:], (BAND_HEADS, BAND_PAST - REL_MAX))
    near = jnp.broadcast_to(rel_bias[:, :1], (BAND_HEADS, BAND_WIN - BAND_PAST - CHUNK))
    wrap = jnp.broadcast_to(rel_bias[:, REL_SIZE - 1:], (BAND_HEADS, 2 * BAND_PAST - BAND_WIN))
    row0 = jnp.concatenate([far, rel_bias[:, ::-1], near, wrap], axis=1)
    assert row0.shape == (BAND_HEADS, 2 * BAND_PAST)
    return pl.pallas_call(
        _band_bias_kernel,
        grid=(BAND_HEADS,),
        in_specs=[pl.BlockSpec((1, 1, 2 * BAND_PAST), lambda h: (h, 0, 0))],
        out_specs=[pl.BlockSpec((1, BAND_TQ, BAND_WIN), lambda h: (h, 0, 0)),
                   pl.BlockSpec((BAND_PIECES, 1, BAND_WIN, BAND_TQ), lambda h: (0, h, 0, 0))],
        out_shape=[jax.ShapeDtypeStruct((BAND_HEADS, BAND_TQ, BAND_WIN), F32),
                   jax.ShapeDtypeStruct((BAND_PIECES, BAND_HEADS, BAND_WIN, BAND_TQ), F32)],
        compiler_params=_params(1), name="band_bias")(row0[:, None, :])


def _softmax_pv(s, v):
    m = functools.reduce(jnp.maximum, [jnp.max(sp, axis=-1, keepdims=True) for sp in s])
    e = [jnp.exp2(sp - m) for sp in s]
    l = functools.reduce(jnp.add, [jnp.sum(ep, axis=-1, keepdims=True) for ep in e])
    o = functools.reduce(jnp.add, [_dot(ep.astype(BF16), vp) for ep, vp in zip(e, v)])
    return (o / l).astype(BF16)


def _band_prompt_kernel(q_ref, *refs):
    n_piece = BAND_QB + BAND_PIECES - 1
    k_refs, vt_refs = refs[:n_piece], refs[n_piece:2 * n_piece]
    bias_refs = refs[2 * n_piece:2 * n_piece + BAND_N_BIAS]
    o_ref = refs[2 * n_piece + BAND_N_BIAS]
    bufs = refs[2 * n_piece + BAND_N_BIAS + 1:]

    def head(h):
        return slice(h * BAND_HEAD_DIM, (h + 1) * BAND_HEAD_DIM)

    def rows(n):
        return slice(n * BAND_TQ, (n + 1) * BAND_TQ)

    def scores(n, h):
        k = jnp.concatenate([r[0, :, head(h)] for r in k_refs[n:n + BAND_PIECES]], axis=0)
        return _dot_nt(k, q_ref[0, rows(n), head(h)]) + bias_refs[min(n, BAND_N_BIAS - 1)][0, h]

    units = [(n, h) for n in range(BAND_QB) for h in range(BAND_HEADS)]
    ahead = len(bufs) - 1
    for u in range(ahead):
        bufs[u][...] = scores(*units[u])
    for u, (n, h) in enumerate(units):
        if u + ahead < len(units):
            bufs[(u + ahead) % len(bufs)][...] = scores(*units[u + ahead])
        s = bufs[u % len(bufs)][...]
        vt = jnp.concatenate([r[head(h), :] for r in vt_refs[n:n + BAND_PIECES]], axis=1)
        vt = jnp.concatenate([vt, jnp.ones((SUM_ROWS, BAND_WIN), BF16)], axis=0)
        e = jnp.exp2(s - jnp.max(s, axis=0, keepdims=True))
        o = _dot(vt, e.astype(BF16))
        o = o[:BAND_HEAD_DIM] / o[BAND_HEAD_DIM:BAND_HEAD_DIM + 1]
        o_ref[0, rows(n), head(h)] = o.T.astype(BF16)


def _band_prompt(qkv, vt, bias_t):
    b, s, _ = qkv.shape
    nq = s // BAND_TQ
    assert nq % BAND_QB == 0
    n_piece = BAND_QB + BAND_PIECES - 1
    piece = (1, BAND_TQ, BAND_WIDTH)
    group = (1, BAND_QB * BAND_TQ, BAND_WIDTH)

    def first_block(i, p):
        return jnp.maximum(i * BAND_QB - (BAND_PIECES - 1) + p, 0)

    def k_spec(p):
        return pl.BlockSpec(piece, lambda bi, i: (bi, first_block(i, p), 1))

    def vt_spec(p):
        return pl.BlockSpec((BAND_WIDTH, BAND_TQ), lambda bi, i: (0, bi * nq + first_block(i, p)))

    def bias_spec(n):
        return pl.BlockSpec((1,) + bias_t.shape[1:],
                            lambda bi, i: (jnp.minimum(i * BAND_QB + n, BAND_PIECES - 1), 0, 0, 0))

    return pl.pallas_call(
        _band_prompt_kernel,
        grid=(b, nq // BAND_QB),
        in_specs=[pl.BlockSpec(group, lambda bi, i: (bi, i, 0))]
                 + [k_spec(p) for p in range(n_piece)]
                 + [vt_spec(p) for p in range(n_piece)]
                 + [bias_spec(n) for n in range(BAND_N_BIAS)],
        out_specs=pl.BlockSpec(group, lambda bi, i: (bi, i, 0)),
        out_shape=jax.ShapeDtypeStruct((b, s, BAND_WIDTH), BF16),
        scratch_shapes=[pltpu.VMEM((BAND_WIN, BAND_TQ), F32)] * BAND_SCORE_BUFFERS,
        compiler_params=_params(2, 62 * 1024 * 1024), name="band_prompt")(
            qkv, *([qkv] * n_piece), *([vt] * n_piece), *([bias_t] * BAND_N_BIAS))


def _band_sample_kernel(q_ref, kn_ref, vn_ref, kc_ref, vc_ref, bias_ref, o_ref, *, n_past):
    t = q_ref.shape[1]
    for h in range(BAND_HEADS):
        hs = slice(h * BAND_HEAD_DIM, (h + 1) * BAND_HEAD_DIM)
        q = q_ref[0, :, hs]
        k_old = kc_ref[0, pl.ds(h, n_past, stride=BAND_HEADS), :].astype(BF16)
        v_old = vc_ref[0, pl.ds(h, n_past, stride=BAND_HEADS), :].astype(BF16)
        s = [_dot_nt(q, k_old) + bias_ref[h, :t, :n_past],
             _dot_nt(q, kn_ref[0, :, hs]) + bias_ref[h, :t, n_past:n_past + t]]
        o_ref[0, :, hs] = _softmax_pv(s, [v_old, vn_ref[0, :, hs]])


def _band_sample(qkv, k_cache, v_cache, bias):
    b, t, _ = qkv.shape
    n_past = k_cache.shape[1] // BAND_HEADS
    new = (1, t, BAND_WIDTH)
    old = (1, n_past * BAND_HEADS, BAND_HEAD_DIM)
    return pl.pallas_call(
        functools.partial(_band_sample_kernel, n_past=n_past),
        grid=(b,),
        in_specs=[pl.BlockSpec(new, lambda bi: (bi, 0, 0)),
                  pl.BlockSpec(new, lambda bi: (bi, 0, 1)),
                  pl.BlockSpec(new, lambda bi: (bi, 0, 2)),
                  pl.BlockSpec(old, lambda bi: (bi, 0, 0)),
                  pl.BlockSpec(old, lambda bi: (bi, 0, 0)),
                  pl.BlockSpec(bias.shape, lambda bi: (0, 0, 0))],
        out_specs=pl.BlockSpec(new, lambda bi: (bi, 0, 0)),
        out_shape=jax.ShapeDtypeStruct((b, t, BAND_WIDTH), BF16),
        compiler_params=_params(1), name="band_sample")(qkv, qkv, qkv, k_cache, v_cache, bias)


def _mla_prompt_kernel(q_ref, k_ref, vt_ref, o_ref, mask_ref, *bufs, tq, nq):
    def scores(i):
        return _dot_nt(k_ref[0, :(i + 1) * tq, :], q_ref[0, i * tq:(i + 1) * tq, :])

    kc = lax.broadcasted_iota(jnp.int32, (tq, tq), 0) // CHUNK
    qc = lax.broadcasted_iota(jnp.int32, (tq, tq), 1) // CHUNK
    mask_ref[...] = jnp.where(kc <= qc, 0.0, NEG)

    ones = jnp.ones((SUM_ROWS, tq), BF16)
    ahead = len(bufs) - 1
    for i in range(min(ahead, nq)):
        bufs[i][:(i + 1) * tq] = scores(i)
    for i in range(nq):
        if i + ahead < nq:
            bufs[(i + ahead) % len(bufs)][:(i + ahead + 1) * tq] = scores(i + ahead)
        buf = bufs[i % len(bufs)]
        blocks = [buf[j * tq:(j + 1) * tq] for j in range(i)] + [buf[i * tq:(i + 1) * tq] + mask_ref[...]]
        m = functools.reduce(jnp.maximum, [jnp.max(sj, axis=0, keepdims=True) for sj in blocks])
        acc = functools.reduce(jnp.add, [
            _dot(jnp.concatenate([vt_ref[0, j], ones], axis=0), jnp.exp2(sj - m).astype(BF16))
            for j, sj in enumerate(blocks)])
        o_ref[0, i * tq:(i + 1) * tq, :] = (acc[:MLA_V] / acc[MLA_V:MLA_V + 1]).T.astype(BF16)


def _mla_prompt(q, k, vt, tq):
    b, s, _ = q.shape
    nq = s // tq
    return pl.pallas_call(
        functools.partial(_mla_prompt_kernel, tq=tq, nq=nq),
        grid=(b, MLA_HEADS),
        in_specs=[pl.BlockSpec((1, s, MLA_QK_PAD), lambda bi, h: (bi, 0, h)),
                  pl.BlockSpec((1, s, MLA_QK_PAD), lambda bi, h: (bi, 0, h)),
                  pl.BlockSpec((1, nq, MLA_V, tq), lambda bi, h: (bi, 0, h, 0))],
        out_specs=pl.BlockSpec((1, s, MLA_V), lambda bi, h: (bi, 0, h)),
        out_shape=jax.ShapeDtypeStruct((b, s, MLA_WIDTH), BF16),
        scratch_shapes=[pltpu.VMEM((tq, tq), F32)] + [pltpu.VMEM((s, tq), F32)] * MLA_SCORE_BUFFERS,
        compiler_params=_params(2), name="mla_prompt")(q, k, vt)


def _mla_sample_kernel(q_ref, ko_ref, vo_ref, kn_ref, vn_ref, o_ref):
    for h in range(MLA_HEADS):
        qk = slice(h * MLA_QK_PAD, (h + 1) * MLA_QK_PAD)
        hv = slice(h * MLA_V, (h + 1) * MLA_V)
        q = q_ref[0, :, qk]
        s = [_dot_nt(q, ko_ref[0, :, qk]), _dot_nt(q, kn_ref[0, :, qk])]
        o_ref[0, :, hv] = _softmax_pv(s, [vo_ref[0, :, hv], vn_ref[0, :, hv]])


def _mla_sample(q, k_old, v_old, k_new, v_new):
    b, t, _ = q.shape

    def whole(a):
        return pl.BlockSpec((1,) + a.shape[1:], lambda bi: (bi, 0, 0))

    return pl.pallas_call(
        _mla_sample_kernel,
        grid=(b,),
        in_specs=[whole(q), whole(k_old), whole(v_old), whole(k_new), whole(v_new)],
        out_specs=pl.BlockSpec((1, t, MLA_WIDTH), lambda bi: (bi, 0, 0)),
        out_shape=jax.ShapeDtypeStruct((b, t, MLA_WIDTH), BF16),
        compiler_params=_params(1), name="mla_sample")(q, k_old, v_old, k_new, v_new)


def _mix_kernel(xn_ref, oa_ref, ob_ref, wga_ref, wgb_ref, wpa_ref, wpb_ref, o_ref):
    xn = xn_ref[...]
    a = jax.nn.sigmoid(_dot(xn, wga_ref[...])) * _dot(oa_ref[...], wpa_ref[...])
    b = jax.nn.sigmoid(_dot(xn, wgb_ref[...])) * _dot(ob_ref[...], wpb_ref[...])
    o_ref[...] = (a + b).astype(BF16)


def _mix(xn, oa, ob, wga, wgb, wpa, wpb, tm, tn=1024):
    m = xn.shape[0]
    return pl.pallas_call(
        _mix_kernel,
        grid=(m // tm, D_MODEL // tn),
        in_specs=[pl.BlockSpec((tm, D_MODEL), lambda i, j: (i, 0)),
                  pl.BlockSpec((tm, BAND_WIDTH), lambda i, j: (i, 0)),
                  pl.BlockSpec((tm, MLA_WIDTH), lambda i, j: (i, 0)),
                  pl.BlockSpec((D_MODEL, tn), lambda i, j: (0, j)),
                  pl.BlockSpec((D_MODEL, tn), lambda i, j: (0, j)),
                  pl.BlockSpec((BAND_WIDTH, tn), lambda i, j: (0, j)),
                  pl.BlockSpec((MLA_WIDTH, tn), lambda i, j: (0, j))],
        out_specs=pl.BlockSpec((tm, tn), lambda i, j: (i, j)),
        out_shape=jax.ShapeDtypeStruct((m, D_MODEL), BF16),
        compiler_params=_params(2, 62 * 1024 * 1024), name="mix")(xn, oa, ob, wga, wgb, wpa, wpb)


def _outproj_kernel(x_ref, mix_ref, w_ref, h_ref):
    h_ref[...] = x_ref[...] + _dot(mix_ref[...], w_ref[...])


def _outproj(x, mix, w, tm):
    m = x.shape[0]
    rows = pl.BlockSpec((tm, D_MODEL), lambda i: (i, 0))
    return pl.pallas_call(
        _outproj_kernel,
        grid=(m // tm,),
        in_specs=[rows, rows, pl.BlockSpec((D_MODEL, D_MODEL), lambda i: (0, 0))],
        out_specs=rows,
        out_shape=jax.ShapeDtypeStruct((m, D_MODEL), F32),
        compiler_params=_params(1), name="outproj")(x, mix, w)


def _ffn_kernel(h_ref, g_ref, wu_ref, wd_ref, o_ref, hn_ref):
    def update(hn):
        u = jnp.maximum(_dot(hn, wu_ref[...]), 0.0)
        return _dot((u * u).astype(BF16), wd_ref[...])

    @pl.when(pl.program_id(1) == 0)
    def _():
        h = h_ref[...]
        hn = _rms(h, g_ref[...]).astype(BF16)
        hn_ref[...] = hn
        o_ref[...] = h + update(hn)

    @pl.when(pl.program_id(1) > 0)
    def _():
        o_ref[...] += update(hn_ref[...])


def _ffn(h, g, wu, wd, tm, tf=1024):
    m = h.shape[0]
    return pl.pallas_call(
        _ffn_kernel,
        grid=(m // tm, D_FF // tf),
        in_specs=[pl.BlockSpec((tm, D_MODEL), lambda i, f: (i, 0)),
                  pl.BlockSpec((1, D_MODEL), lambda i, f: (0, 0)),
                  pl.BlockSpec((D_MODEL, tf), lambda i, f: (0, f)),
                  pl.BlockSpec((tf, D_MODEL), lambda i, f: (f, 0))],
        out_specs=pl.BlockSpec((tm, D_MODEL), lambda i, f: (i, 0)),
        out_shape=jax.ShapeDtypeStruct((m, D_MODEL), F32),
        scratch_shapes=[pltpu.VMEM((tm, D_MODEL), BF16)],
        compiler_params=_params(2, 62 * 1024 * 1024), name="ffn")(h, g, wu, wd)


def _rope_table(pos, rows):
    half = MLA_ROPE // 2
    freqs = ROPE_THETA ** (-(jnp.arange(half, dtype=F32) / half))
    ang = pos.astype(F32)[:, None] * freqs[None, :]
    cos = jnp.cos(ang)
    sin = jnp.sin(ang)
    table = jnp.concatenate([cos, cos, -sin, sin], axis=1)
    return jnp.tile(table, (rows // table.shape[0], 1))


def _with_partner(w):
    half = MLA_ROPE // 2
    return jnp.concatenate([w, w[..., half:], w[..., :half]], axis=-1)


def _layer_weights(l, norm_mix_g, w_in, g_aq, g_ak, g_kv, g_kr, g_qn, g_qr, g_kn,
                   w_kv_b, w_pa, w_pb, w_out, norm_ffn_g, w_up, w_down):
    w = w_in[l].astype(BF16)
    c = 0
    parts = []
    for width in (3 * BAND_WIDTH, MLA_HEADS * MLA_QK, MLA_KV_RANK, MLA_ROPE, D_MODEL, D_MODEL):
        parts.append(w[:, c:c + width])
        c += width
    w_a, w_bq, w_ckv, w_kr, w_ga, w_gb = parts
    w_q = jnp.concatenate(
        [piece for h in range(MLA_HEADS)
         for piece in (w_bq[:, h * MLA_QK:h * MLA_QK + MLA_NOPE],
                       _with_partner(w_bq[:, h * MLA_QK + MLA_NOPE:(h + 1) * MLA_QK]))], axis=1)
    row = lambda g: g[l][None, :].astype(F32)
    return dict(
        gx=row(norm_mix_g),
        w_in=w,
        w_q=w_q,
        w_kr=_with_partner(w_kr),
        g_a=jnp.concatenate([jnp.tile(g_aq[l] * BAND_QSCALE, BAND_HEADS), jnp.tile(g_ak[l], BAND_HEADS),
                             jnp.ones((BAND_WIDTH,), F32)])[None, :],
        g_ak=row(g_ak),
        g_qn=row(g_qn),
        g_qr=_with_partner(g_qr[l])[None, :],
        g_kv=row(g_kv),
        g_kr=_with_partner(g_kr[l])[None, :],
        w_kv_b=w_kv_b[l].astype(BF16),
        g_kn=row(g_kn),
        w_ga=w_ga, w_gb=w_gb,
        w_pa=w_pa[l].astype(BF16), w_pb=w_pb[l].astype(BF16),
        w_out=w_out[l].astype(BF16),
        g_ffn=row(norm_ffn_g),
        w_up=w_up[l].astype(BF16), w_down=w_down[l].astype(BF16),
    )


def _merge_ffn(x2, xn, oa, ob, p, tm):
    mix = _mix(xn, oa, ob, p["w_ga"], p["w_gb"], p["w_pa"], p["w_pb"], tm)
    h = _outproj(x2, mix, p["w_out"], min(tm, 512))
    return _ffn(h, p["g_ffn"], p["w_up"], p["w_down"], tm)


def _cache_shape(kv, lead):
    return tuple(a.reshape(lead + (BAND_HEADS, BAND_HEAD_DIM)) for a in kv)


def kernel(x_prompt, x_sample, cache_a_k, cache_a_v, cache_mla_ckv, cache_mla_krope, norm_mix_g, w_in, g_aq, g_ak, rel_bias, g_kv, g_kr, g_qn, g_qr, g_kn, w_kv_b, w_pa, w_pb, w_out, norm_ffn_g, w_up, w_down):
    b, s, _ = x_prompt.shape
    bs, t, _ = x_sample.shape
    past = cache_mla_ckv.shape[2]
    n_band = cache_a_k.shape[2]
    depth = w_in.shape[0]
    keep = min(BAND_PAST, s)
    tm_p = 1024
    tq_mla = 256
    tm_s = bs * t
    assert s % tm_p == 0 and s % BAND_TQ == 0 and s % tq_mla == 0 and t == CHUNK and past % CHUNK == 0
    tm_cache = min(bs * past, 2048)
    assert (bs * past) % tm_cache == 0
    tm_keep = min(keep, 512)
    assert keep % tm_keep == 0 and s % tm_keep == 0 and (s - keep) % tm_keep == 0
    cs_p = _rope_table(jnp.arange(s), max(s, tm_p))
    cs_s = _rope_table(past + jnp.arange(t), tm_s)

    yp = x_prompt.reshape(b * s, D_MODEL)
    ys = x_sample.reshape(bs * t, D_MODEL)
    outs = [[] for _ in range(8)]
    for l in range(depth):
        p = _layer_weights(l, norm_mix_g, w_in, g_aq, g_ak, g_kv, g_kr, g_qn, g_qr, g_kn,
                           w_kv_b, w_pa, w_pb, w_out, norm_ffn_g, w_up, w_down)
        bias, bias_t = _band_bias(rel_bias[l])

        xn, qkv, vt, q_mla, ckv, ckv_bf, krr, kr = _proj(yp, p, cs_p, tm_p)
        kb = keep // tm_keep
        ak, av = _cache_shape(_kv_f32(
            yp, p["gx"], p["w_in"], p["g_ak"], tm_keep, b * keep,
            lambda i: (i // kb) * (s // tm_keep) + (s - keep) // tm_keep + i % kb), (b, keep))
        oa = _band_prompt(qkv.reshape(b, s, 3 * BAND_WIDTH), vt, bias_t)
        k_mla, vt_mla = _expand(ckv_bf, p["w_kv_b"], p["g_kn"], krr, tm_p, True, tq_mla)
        ob = _mla_prompt(q_mla.reshape(b, s, -1), k_mla.reshape(b, s, -1),
                         vt_mla.reshape(b, s // tq_mla, MLA_WIDTH, tq_mla), tq_mla)
        for o, val in zip(outs[:4], (ak, av, ckv.reshape(b, s, MLA_KV_RANK), kr.reshape(b, s, MLA_ROPE))):
            o.append(val)
        yp = _merge_ffn(yp, xn, oa.reshape(b * s, BAND_WIDTH), ob.reshape(b * s, MLA_WIDTH), p, tm_p)

        xn, qkv, _, q_mla, ckv, ckv_bf, krr, kr = _proj(ys, p, cs_s, tm_s)
        ak, av = _cache_shape(_kv_f32(ys, p["gx"], p["w_in"], p["g_ak"], tm_s, bs * t, lambda i: i), (bs, t))
        oa = _band_sample(qkv.reshape(bs, t, 3 * BAND_WIDTH),
                          cache_a_k[l].reshape(bs, n_band * BAND_HEADS, BAND_HEAD_DIM),
                          cache_a_v[l].reshape(bs, n_band * BAND_HEADS, BAND_HEAD_DIM), bias)
        k_old, v_old = _expand(cache_mla_ckv[l].reshape(bs * past, MLA_KV_RANK), p["w_kv_b"], p["g_kn"],
                               cache_mla_krope[l].reshape(bs * past, MLA_ROPE), tm_cache, False)
        k_new, v_new = _expand(ckv_bf, p["w_kv_b"], p["g_kn"], krr, tm_s, False)
        ob = _mla_sample(q_mla.reshape(bs, t, -1), k_old.reshape(bs, past, -1), v_old.reshape(bs, past, -1),
                         k_new.reshape(bs, t, -1), v_new.reshape(bs, t, -1))
        for o, val in zip(outs[4:], (ak, av, ckv.reshape(bs, t, MLA_KV_RANK), kr.reshape(bs, t, MLA_ROPE))):
            o.append(val)
        ys = _merge_ffn(ys, xn, oa.reshape(bs * t, BAND_WIDTH), ob.reshape(bs * t, MLA_WIDTH), p, tm_s)

    return (yp.reshape(b, s, D_MODEL), ys.reshape(bs, t, D_MODEL)) + tuple(jnp.stack(o) for o in outs)
```

```python
import functools

import jax
import jax.numpy as jnp
from jax import lax
from jax.experimental import pallas as pl
from jax.experimental.pallas import tpu as pltpu

F32 = jnp.float32
BF16 = jnp.bfloat16

D_MODEL = 2048
CHUNK = 64
BAND_PAST = 8 * CHUNK
BAND_HEADS = 8
BAND_HEAD_DIM = 128
BAND_WIDTH = BAND_HEADS * BAND_HEAD_DIM
REL_MAX = 256
REL_SIZE = (CHUNK - 1) + REL_MAX + 1
BAND_SCALE = BAND_HEAD_DIM ** -0.5
MLA_HEADS = 8
MLA_NOPE = 128
MLA_ROPE = 64
MLA_QK = MLA_NOPE + MLA_ROPE
MLA_V = 128
MLA_WIDTH = MLA_HEADS * MLA_V
MLA_KV_RANK = 512
MLA_SCALE = MLA_QK ** -0.5
ROPE_THETA = 10000.0
D_FF = 4 * D_MODEL
EPS = 1e-6
NEG = -1e30
LOG2E = 1.4426950408889634
MLA_QSCALE = MLA_SCALE * LOG2E
BAND_QSCALE = BAND_SCALE * LOG2E

LANES = 128
MLA_QK_PAD = 2 * LANES
BAND_TQ = 256
BAND_WIN = BAND_TQ + BAND_PAST
BAND_PIECES = BAND_WIN // BAND_TQ
VMEM_LIMIT = 56 * 1024 * 1024
PROJ_TN = 512
PROJ_ROW_CHUNKS = 4
SUM_ROWS = 16
BAND_QB = 4
BAND_N_BIAS = min(BAND_QB, BAND_PIECES)
BAND_SCORE_BUFFERS = 3
MLA_PV_GROUP = 4
MLA_SCORE_BUFFERS = 3
SAMPLE_SCORE_BUFFERS = 3


def _params(n_axes, vmem=VMEM_LIMIT):
    return pltpu.CompilerParams(dimension_semantics=("arbitrary",) * n_axes,
                                vmem_limit_bytes=vmem)


def _rms(a, g):
    return a * lax.rsqrt(jnp.mean(a * a, axis=-1, keepdims=True) + EPS) * g


def _dot(a, b):
    return jnp.dot(a, b, preferred_element_type=F32)


def _dot_nt(a, b):
    return lax.dot_general(a, b, (((1,), (1,)), ((), ())), preferred_element_type=F32)


def _rope_pair(ar, g, cs):
    lane = lax.broadcasted_iota(jnp.int32, (1, LANES), 1)
    first = (lane < MLA_ROPE).astype(F32)
    ss = jnp.sum(ar * ar * first, axis=-1, keepdims=True) / MLA_ROPE
    return ar * lax.rsqrt(ss + EPS) * g * cs


_NB_BAND_NORM = 2 * BAND_WIDTH // PROJ_TN
_NB_BAND = 3 * BAND_WIDTH // PROJ_TN
_NB_MLA_Q = MLA_HEADS * MLA_QK_PAD // PROJ_TN
_NB_PROJ = _NB_BAND + _NB_MLA_Q + 1
_CKV_BLOCK = (3 * BAND_WIDTH + MLA_HEADS * MLA_QK) // PROJ_TN
assert MLA_KV_RANK == PROJ_TN and _CKV_BLOCK * PROJ_TN == 3 * BAND_WIDTH + MLA_HEADS * MLA_QK


def _proj_kernel(x_ref, gx_ref, w_ref, wq_ref, wkr_ref, gh_ref, gqn_ref, gqr_ref, gkv_ref, gkr_ref, cs_ref,
                 xn_out_ref, qkv_ref, vt_ref, q_ref, ckv_ref, ckvb_ref, krr_ref, kr_ref, xn_ref):
    j = pl.program_id(1)
    tm = x_ref.shape[0]
    row_chunks = [slice(c * tm // PROJ_ROW_CHUNKS, (c + 1) * tm // PROJ_ROW_CHUNKS) for c in range(PROJ_ROW_CHUNKS)]

    def product(r, weights=w_ref):
        return _dot(xn_ref[r, :], weights[...])

    def band_norm(r, acc):
        for k in range(PROJ_TN // LANES):
            sl = slice(k * LANES, (k + 1) * LANES)
            qkv_ref[r, sl] = _rms(acc[:, sl], gh_ref[:, sl]).astype(BF16)

    @pl.when(j == 0)
    def _():
        for r in row_chunks:
            xn = _rms(x_ref[r, :], gx_ref[...]).astype(BF16)
            xn_ref[r, :] = xn
            xn_out_ref[r, :] = xn
            band_norm(r, _dot(xn, w_ref[...]))

    @pl.when(jnp.logical_and(j > 0, j < _NB_BAND_NORM))
    def _():
        for r in row_chunks:
            band_norm(r, product(r))

    @pl.when(jnp.logical_and(j >= _NB_BAND_NORM, j < _NB_BAND))
    def _():
        for r in row_chunks:
            acc = product(r)
            qkv_ref[r, :] = acc.astype(BF16)
            vt_ref[:, r] = acc.T.astype(BF16)

    @pl.when(jnp.logical_and(j >= _NB_BAND, j < _NB_BAND + _NB_MLA_Q))
    def _():
        for r in row_chunks:
            acc = product(r, wq_ref)
            cs = cs_ref[r, :]
            for k in range(PROJ_TN // MLA_QK_PAD):
                c0 = k * MLA_QK_PAD
                qn = _rms(acc[:, c0:c0 + LANES], gqn_ref[...]) * MLA_QSCALE
                qr = _rope_pair(acc[:, c0 + LANES:c0 + 2 * LANES], gqr_ref[...], cs) * MLA_QSCALE
                q_ref[r, c0:c0 + LANES] = qn.astype(BF16)
                q_ref[r, c0 + LANES:c0 + 2 * LANES] = qr.astype(BF16)

    @pl.when(j == _NB_BAND + _NB_MLA_Q)
    def _():
        for r in row_chunks:
            cn = _rms(product(r), gkv_ref[...])
            ckv_ref[r, :] = cn
            ckvb_ref[r, :] = cn.astype(BF16)
            t = _rope_pair(product(r, wkr_ref), gkr_ref[...], cs_ref[r, :])
            krr = t + pltpu.roll(t, MLA_ROPE, 1)
            krr_ref[r, :] = krr
            kr_ref[r, :] = krr[:, :MLA_ROPE]


def _proj(x, p, cs, tm):
    m = x.shape[0]
    tn = PROJ_TN
    n_cs = cs.shape[0] // tm
    const = lambda i, j: (0, 0)
    rows = lambda i, j: (i, 0)
    return pl.pallas_call(
        _proj_kernel,
        grid=(m // tm, _NB_PROJ),
        in_specs=[pl.BlockSpec((tm, D_MODEL), rows),
                  pl.BlockSpec((1, D_MODEL), const),
                  pl.BlockSpec((D_MODEL, tn), lambda i, j: (0, jnp.where(
                      j < _NB_BAND, j, jnp.where(j == _NB_PROJ - 1, _CKV_BLOCK, _NB_BAND - 1)))),
                  pl.BlockSpec((D_MODEL, tn), lambda i, j: (0, jnp.clip(j - _NB_BAND, 0, _NB_MLA_Q - 1))),
                  pl.BlockSpec((D_MODEL, LANES), const),
                  pl.BlockSpec((1, tn), lambda i, j: (0, jnp.minimum(j, _NB_BAND - 1))),
                  pl.BlockSpec((1, LANES), const),
                  pl.BlockSpec((1, LANES), const),
                  pl.BlockSpec((1, MLA_KV_RANK), const),
                  pl.BlockSpec((1, LANES), const),
                  pl.BlockSpec((tm, LANES), lambda i, j: (i % n_cs, 0))],
        out_specs=[pl.BlockSpec((tm, D_MODEL), rows),
                   pl.BlockSpec((tm, tn), lambda i, j: (i, jnp.minimum(j, _NB_BAND - 1))),
                   pl.BlockSpec((tn, tm), lambda i, j: (jnp.clip(j - _NB_BAND_NORM, 0, _NB_BAND - _NB_BAND_NORM - 1), i)),
                   pl.BlockSpec((tm, tn), lambda i, j: (i, jnp.clip(j - _NB_BAND, 0, _NB_MLA_Q - 1))),
                   pl.BlockSpec((tm, MLA_KV_RANK), rows),
                   pl.BlockSpec((tm, MLA_KV_RANK), rows),
                   pl.BlockSpec((tm, LANES), rows),
                   pl.BlockSpec((tm, MLA_ROPE), rows)],
        out_shape=[jax.ShapeDtypeStruct((m, D_MODEL), BF16),
                   jax.ShapeDtypeStruct((m, 3 * BAND_WIDTH), BF16),
                   jax.ShapeDtypeStruct((BAND_WIDTH, m), BF16),
                   jax.ShapeDtypeStruct((m, MLA_HEADS * MLA_QK_PAD), BF16),
                   jax.ShapeDtypeStruct((m, MLA_KV_RANK), F32),
                   jax.ShapeDtypeStruct((m, MLA_KV_RANK), BF16),
                   jax.ShapeDtypeStruct((m, LANES), F32),
                   jax.ShapeDtypeStruct((m, MLA_ROPE), F32)],
        scratch_shapes=[pltpu.VMEM((tm, D_MODEL), BF16)],
        compiler_params=_params(2, 62 * 1024 * 1024), name="proj")(
            x, p["gx"], p["w_in"], p["w_q"], p["w_kr"], p["g_a"], p["g_qn"], p["g_qr"], p["g_kv"], p["g_kr"], cs)


def _kv_f32_kernel(x_ref, gx_ref, w_ref, g_ref, k_ref, v_ref, xn_ref):
    j = pl.program_id(1)
    tm = x_ref.shape[0]

    def heads(o_ref, acc, norm):
        for h in range(BAND_HEADS):
            y = acc[:, h * BAND_HEAD_DIM:(h + 1) * BAND_HEAD_DIM]
            o_ref[pl.ds(h, tm, stride=BAND_HEADS), :] = _rms(y, g_ref[...]) if norm else y

    @pl.when(j == 0)
    def _():
        xn = _rms(x_ref[...], gx_ref[...]).astype(BF16)
        xn_ref[...] = xn
        heads(k_ref, _dot(xn, w_ref[...]), True)

    @pl.when(j == 1)
    def _():
        heads(v_ref, _dot(xn_ref[...], w_ref[...]), False)


def _kv_f32(x, gx, w_in, g, tm, n_rows, row_block):
    out = pl.BlockSpec((tm * BAND_HEADS, BAND_HEAD_DIM), lambda i, j: (i, 0))
    shape = jax.ShapeDtypeStruct((n_rows * BAND_HEADS, BAND_HEAD_DIM), F32)
    return pl.pallas_call(
        _kv_f32_kernel,
        grid=(n_rows // tm, 2),
        in_specs=[pl.BlockSpec((tm, D_MODEL), lambda i, j: (row_block(i), 0)),
                  pl.BlockSpec((1, D_MODEL), lambda i, j: (0, 0)),
                  pl.BlockSpec((D_MODEL, BAND_WIDTH), lambda i, j: (0, j + 1)),
                  pl.BlockSpec((1, LANES), lambda i, j: (0, 0))],
        out_specs=[out, out],
        out_shape=[shape, shape],
        scratch_shapes=[pltpu.VMEM((tm, D_MODEL), BF16)],
        compiler_params=_params(2), name="kv_f32")(x, gx, w_in, g)


def _expand_kernel(c_ref, w_ref, gkn_ref, krr_ref, k_ref, v_ref, *, v_transposed):
    c = c_ref[...].astype(BF16)
    krr = krr_ref[...].astype(BF16)
    if krr.shape[1] == MLA_ROPE:
        krr = jnp.concatenate([krr, krr], axis=1)
    for h in range(MLA_HEADS):
        a = _dot(c, w_ref[:, h * 2 * LANES:(h + 1) * 2 * LANES])
        k_ref[:, h * MLA_QK_PAD:h * MLA_QK_PAD + LANES] = _rms(a[:, :LANES], gkn_ref[...]).astype(BF16)
        k_ref[:, h * MLA_QK_PAD + LANES:(h + 1) * MLA_QK_PAD] = krr
        if v_transposed:
            slab = v_ref.shape[2]
            for n in range(v_ref.shape[0]):
                v_ref[n, h * MLA_V:(h + 1) * MLA_V, :] = a[n * slab:(n + 1) * slab, LANES:].T.astype(BF16)
        else:
            v_ref[:, h * MLA_V:(h + 1) * MLA_V] = a[:, LANES:].astype(BF16)


def _expand(c, w, gkn, krr, tm, v_transposed, slab=None):
    m = c.shape[0]
    if v_transposed:
        v_spec = pl.BlockSpec((tm // slab, MLA_WIDTH, slab), lambda i: (i, 0, 0))
        v_shape = jax.ShapeDtypeStruct((m // slab, MLA_WIDTH, slab), BF16)
    else:
        v_spec = pl.BlockSpec((tm, MLA_WIDTH), lambda i: (i, 0))
        v_shape = jax.ShapeDtypeStruct((m, MLA_WIDTH), BF16)
    return pl.pallas_call(
        functools.partial(_expand_kernel, v_transposed=v_transposed),
        grid=(m // tm,),
        in_specs=[pl.BlockSpec((tm, MLA_KV_RANK), lambda i: (i, 0)),
                  pl.BlockSpec(w.shape, lambda i: (0, 0)),
                  pl.BlockSpec((1, LANES), lambda i: (0, 0)),
                  pl.BlockSpec((tm, krr.shape[1]), lambda i: (i, 0))],
        out_specs=[pl.BlockSpec((tm, MLA_HEADS * MLA_QK_PAD), lambda i: (i, 0)), v_spec],
        out_shape=[jax.ShapeDtypeStruct((m, MLA_HEADS * MLA_QK_PAD), BF16), v_shape],
        compiler_params=_params(1), name="expand")(c, w, gkn, krr)


def _band_bias_kernel(b_ref, o_ref, ot_ref):
    t = pltpu.roll(jnp.broadcast_to(b_ref[0], (BAND_TQ, 2 * BAND_PAST)), 0, 1,
                   stride=1, stride_axis=0)[:, :BAND_WIN] * LOG2E
    qc = lax.broadcasted_iota(jnp.int32, (BAND_TQ, BAND_WIN), 0) // CHUNK
    kc = lax.broadcasted_iota(jnp.int32, (BAND_TQ, BAND_WIN), 1) // CHUNK
    t = jnp.where(kc >= qc, jnp.where(kc <= qc + BAND_PAST // CHUNK, t, NEG), NEG)
    o_ref[0] = t
    tt = t.T
    key = lax.broadcasted_iota(jnp.int32, (BAND_WIN, BAND_TQ), 0)
    for v in range(BAND_PIECES):
        n_missing = BAND_PIECES - 1 - v
        ot_ref[v, 0] = jnp.where(key >= n_missing * BAND_TQ, tt, NEG)


def _band_bias(rel_bias):
    far = jnp.broadcast_to(rel_bias[:, REL_SIZE - 1:], (BAND_HEADS, BAND_PAST - REL_MAX))
    near = jnp.broadcast_to(rel_bias[:, :1], (BAND_HEADS, BAND_WIN - BAND_PAST - CHUNK))
    wrap = jnp.broadcast_to(rel_bias[:, REL_SIZE - 1:], (BAND_HEADS, 2 * BAND_PAST - BAND_WIN))
    row0 = jnp.concatenate([far, rel_bias[:, ::-1], near, wrap], axis=1)
    assert row0.shape == (BAND_HEADS, 2 * BAND_PAST)
    return pl.pallas_call(
        _band_bias_kernel,
        grid=(BAND_HEADS,),
        in_specs=[pl.BlockSpec((1, 1, 2 * BAND_PAST), lambda h: (h, 0, 0))],
        out_specs=[pl.BlockSpec((1, BAND_TQ, BAND_WIN), lambda h: (h, 0, 0)),
                   pl.BlockSpec((BAND_PIECES, 1, BAND_WIN, BAND_TQ), lambda h: (0, h, 0, 0))],
        out_shape=[jax.ShapeDtypeStruct((BAND_HEADS, BAND_TQ, BAND_WIN), F32),
                   jax.ShapeDtypeStruct((BAND_PIECES, BAND_HEADS, BAND_WIN, BAND_TQ), F32)],
        compiler_params=_params(1), name="band_bias")(row0[:, None, :])


def _softmax_pv(s, v):
    m = functools.reduce(jnp.maximum, [jnp.max(sp, axis=-1, keepdims=True) for sp in s])
    e = [jnp.exp2(sp - m) for sp in s]
    l = functools.reduce(jnp.add, [jnp.sum(ep, axis=-1, keepdims=True) for ep in e])
    o = functools.reduce(jnp.add, [_dot(ep.astype(BF16), vp) for ep, vp in zip(e, v)])
    return (o / l).astype(BF16)


def _pipelined_heads(n_heads, scores, finish, bufs):
    ahead = len(bufs) - 1

    def issue(h):
        for ref, piece in zip(bufs[h % len(bufs)], scores(h)):
            ref[...] = piece

    for h in range(min(ahead, n_heads)):
        issue(h)
    for h in range(n_heads):
        if h + ahead < n_heads:
            issue(h + ahead)
        finish(h, [ref[...] for ref in bufs[h % len(bufs)]])


def _band_prompt_kernel(q_ref, *refs):
    n_piece = BAND_QB + BAND_PIECES - 1
    k_refs, vt_refs = refs[:n_piece], refs[n_piece:2 * n_piece]
    bias_refs = refs[2 * n_piece:2 * n_piece + BAND_N_BIAS]
    o_ref = refs[2 * n_piece + BAND_N_BIAS]
    bufs = refs[2 * n_piece + BAND_N_BIAS + 1:]

    def head(h):
        return slice(h * BAND_HEAD_DIM, (h + 1) * BAND_HEAD_DIM)

    def rows(n):
        return slice(n * BAND_TQ, (n + 1) * BAND_TQ)

    def scores(n, h):
        k = jnp.concatenate([r[0, :, head(h)] for r in k_refs[n:n + BAND_PIECES]], axis=0)
        return _dot_nt(k, q_ref[0, rows(n), head(h)]) + bias_refs[min(n, BAND_N_BIAS - 1)][0, h]

    units = [(n, h) for n in range(BAND_QB) for h in range(BAND_HEADS)]
    ahead = len(bufs) - 1
    for u in range(ahead):
        bufs[u][...] = scores(*units[u])
    for u, (n, h) in enumerate(units):
        if u + ahead < len(units):
            bufs[(u + ahead) % len(bufs)][...] = scores(*units[u + ahead])
        s = bufs[u % len(bufs)][...]
        vt = jnp.concatenate([r[head(h), :] for r in vt_refs[n:n + BAND_PIECES]], axis=1)
        vt = jnp.concatenate([vt, jnp.ones((SUM_ROWS, BAND_WIN), BF16)], axis=0)
        e = jnp.exp2(s - jnp.max(s, axis=0, keepdims=True))
        o = _dot(vt, e.astype(BF16))
        o = o[:BAND_HEAD_DIM] / o[BAND_HEAD_DIM:BAND_HEAD_DIM + 1]
        o_ref[0, rows(n), head(h)] = o.T.astype(BF16)


def _band_prompt(qkv, vt, bias_t):
    b, s, _ = qkv.shape
    nq = s // BAND_TQ
    assert nq % BAND_QB == 0
    n_piece = BAND_QB + BAND_PIECES - 1
    piece = (1, BAND_TQ, BAND_WIDTH)
    group = (1, BAND_QB * BAND_TQ, BAND_WIDTH)

    def first_block(i, p):
        return jnp.maximum(i * BAND_QB - (BAND_PIECES - 1) + p, 0)

    def k_spec(p):
        return pl.BlockSpec(piece, lambda bi, i: (bi, first_block(i, p), 1))

    def vt_spec(p):
        return pl.BlockSpec((BAND_WIDTH, BAND_TQ), lambda bi, i: (0, bi * nq + first_block(i, p)))

    def bias_spec(n):
        return pl.BlockSpec((1,) + bias_t.shape[1:],
                            lambda bi, i: (jnp.minimum(i * BAND_QB + n, BAND_PIECES - 1), 0, 0, 0))

    return pl.pallas_call(
        _band_prompt_kernel,
        grid=(b, nq // BAND_QB),
        in_specs=[pl.BlockSpec(group, lambda bi, i: (bi, i, 0))]
                 + [k_spec(p) for p in range(n_piece)]
                 + [vt_spec(p) for p in range(n_piece)]
                 + [bias_spec(n) for n in range(BAND_N_BIAS)],
        out_specs=pl.BlockSpec(group, lambda bi, i: (bi, i, 0)),
        out_shape=jax.ShapeDtypeStruct((b, s, BAND_WIDTH), BF16),
        scratch_shapes=[pltpu.VMEM((BAND_WIN, BAND_TQ), F32)] * BAND_SCORE_BUFFERS,
        compiler_params=_params(2, 62 * 1024 * 1024), name="band_prompt")(
            qkv, *([qkv] * n_piece), *([vt] * n_piece), *([bias_t] * BAND_N_BIAS))


def _band_sample_kernel(q_ref, kn_ref, vn_ref, kc_ref, vc_ref, bias_ref, o_ref, *bufs, n_past):
    t = q_ref.shape[1]

    def head(h):
        return slice(h * BAND_HEAD_DIM, (h + 1) * BAND_HEAD_DIM)

    def cached(ref, h):
        return ref[0, pl.ds(h, n_past, stride=BAND_HEADS), :].astype(BF16)

    def scores(h):
        q = q_ref[0, :, head(h)]
        return (_dot_nt(q, cached(kc_ref, h)) + bias_ref[h, :t, :n_past],
                _dot_nt(q, kn_ref[0, :, head(h)]) + bias_ref[h, :t, n_past:n_past + t])

    def finish(h, s):
        o_ref[0, :, head(h)] = _softmax_pv(s, [cached(vc_ref, h), vn_ref[0, :, head(h)]])

    _pipelined_heads(BAND_HEADS, scores, finish, list(zip(bufs[0::2], bufs[1::2])))


def _band_sample(qkv, k_cache, v_cache, bias):
    b, t, _ = qkv.shape
    n_past = k_cache.shape[1] // BAND_HEADS
    new = (1, t, BAND_WIDTH)
    old = (1, n_past * BAND_HEADS, BAND_HEAD_DIM)
    return pl.pallas_call(
        functools.partial(_band_sample_kernel, n_past=n_past),
        grid=(b,),
        in_specs=[pl.BlockSpec(new, lambda bi: (bi, 0, 0)),
                  pl.BlockSpec(new, lambda bi: (bi, 0, 1)),
                  pl.BlockSpec(new, lambda bi: (bi, 0, 2)),
                  pl.BlockSpec(old, lambda bi: (bi, 0, 0)),
                  pl.BlockSpec(old, lambda bi: (bi, 0, 0)),
                  pl.BlockSpec(bias.shape, lambda bi: (0, 0, 0))],
        out_specs=pl.BlockSpec(new, lambda bi: (bi, 0, 0)),
        out_shape=jax.ShapeDtypeStruct((b, t, BAND_WIDTH), BF16),
        scratch_shapes=[pltpu.VMEM((t, n_past), F32), pltpu.VMEM((t, t), F32)] * SAMPLE_SCORE_BUFFERS,
        compiler_params=_params(1), name="band_sample")(qkv, qkv, qkv, k_cache, v_cache, bias)


def _mla_prompt_kernel(q_ref, k_ref, vt_ref, o_ref, mask_ref, *bufs, tq, nq):
    def scores(i):
        return _dot_nt(k_ref[0, :(i + 1) * tq, :], q_ref[0, i * tq:(i + 1) * tq, :])

    kc = lax.broadcasted_iota(jnp.int32, (tq, tq), 0) // CHUNK
    qc = lax.broadcasted_iota(jnp.int32, (tq, tq), 1) // CHUNK
    mask_ref[...] = jnp.where(kc <= qc, 0.0, NEG)

    ahead = len(bufs) - 1
    for i in range(min(ahead, nq)):
        bufs[i][:(i + 1) * tq] = scores(i)
    for i in range(nq):
        if i + ahead < nq:
            bufs[(i + ahead) % len(bufs)][:(i + ahead + 1) * tq] = scores(i + ahead)
        buf = bufs[i % len(bufs)]
        buf[i * tq:(i + 1) * tq] = buf[i * tq:(i + 1) * tq] + mask_ref[...]
        m = jnp.max(buf[:(i + 1) * tq], axis=0, keepdims=True)
        acc = None
        for g in range(0, i + 1, MLA_PV_GROUP):
            n = min(MLA_PV_GROUP, i + 1 - g)
            vt = jnp.concatenate([vt_ref[0, j] for j in range(g, g + n)], axis=1)
            vt = jnp.concatenate([vt, jnp.ones((SUM_ROWS, n * tq), BF16)], axis=0)
            part = _dot(vt, jnp.exp2(buf[g * tq:(g + n) * tq] - m).astype(BF16))
            acc = part if acc is None else acc + part
        o_ref[0, i * tq:(i + 1) * tq, :] = (acc[:MLA_V] / acc[MLA_V:MLA_V + 1]).T.astype(BF16)


def _mla_prompt(q, k, vt, tq):
    b, s, _ = q.shape
    nq = s // tq
    return pl.pallas_call(
        functools.partial(_mla_prompt_kernel, tq=tq, nq=nq),
        grid=(b, MLA_HEADS),
        in_specs=[pl.BlockSpec((1, s, MLA_QK_PAD), lambda bi, h: (bi, 0, h)),
                  pl.BlockSpec((1, s, MLA_QK_PAD), lambda bi, h: (bi, 0, h)),
                  pl.BlockSpec((1, nq, MLA_V, tq), lambda bi, h: (bi, 0, h, 0))],
        out_specs=pl.BlockSpec((1, s, MLA_V), lambda bi, h: (bi, 0, h)),
        out_shape=jax.ShapeDtypeStruct((b, s, MLA_WIDTH), BF16),
        scratch_shapes=[pltpu.VMEM((tq, tq), F32)] + [pltpu.VMEM((s, tq), F32)] * MLA_SCORE_BUFFERS,
        compiler_params=_params(2), name="mla_prompt")(q, k, vt)


def _mla_sample_kernel(q_ref, ko_ref, vo_ref, kn_ref, vn_ref, o_ref, *bufs):
    def scores(h):
        qk = slice(h * MLA_QK_PAD, (h + 1) * MLA_QK_PAD)
        q = q_ref[0, :, qk]
        return _dot_nt(q, ko_ref[0, :, qk]), _dot_nt(q, kn_ref[0, :, qk])

    def finish(h, s):
        hv = slice(h * MLA_V, (h + 1) * MLA_V)
        o_ref[0, :, hv] = _softmax_pv(s, [vo_ref[0, :, hv], vn_ref[0, :, hv]])

    _pipelined_heads(MLA_HEADS, scores, finish, list(zip(bufs[0::2], bufs[1::2])))


def _mla_sample(q, k_old, v_old, k_new, v_new):
    b, t, _ = q.shape

    def whole(a):
        return pl.BlockSpec((1,) + a.shape[1:], lambda bi: (bi, 0, 0))

    return pl.pallas_call(
        _mla_sample_kernel,
        grid=(b,),
        in_specs=[whole(q), whole(k_old), whole(v_old), whole(k_new), whole(v_new)],
        out_specs=pl.BlockSpec((1, t, MLA_WIDTH), lambda bi: (bi, 0, 0)),
        out_shape=jax.ShapeDtypeStruct((b, t, MLA_WIDTH), BF16),
        scratch_shapes=[pltpu.VMEM((t, k_old.shape[1]), F32), pltpu.VMEM((t, k_new.shape[1]), F32)]
                       * SAMPLE_SCORE_BUFFERS,
        compiler_params=_params(1), name="mla_sample")(q, k_old, v_old, k_new, v_new)


def _mix_kernel(xn_ref, oa_ref, ob_ref, wga_ref, wgb_ref, wpa_ref, wpb_ref, o_ref):
    xn = xn_ref[...]
    a = jax.nn.sigmoid(_dot(xn, wga_ref[...])) * _dot(oa_ref[...], wpa_ref[...])
    b = jax.nn.sigmoid(_dot(xn, wgb_ref[...])) * _dot(ob_ref[...], wpb_ref[...])
    o_ref[...] = (a + b).astype(BF16)


def _mix(xn, oa, ob, wga, wgb, wpa, wpb, tm, tn=1024):
    m = xn.shape[0]
    return pl.pallas_call(
        _mix_kernel,
        grid=(m // tm, D_MODEL // tn),
        in_specs=[pl.BlockSpec((tm, D_MODEL), lambda i, j: (i, 0)),
                  pl.BlockSpec((tm, BAND_WIDTH), lambda i, j: (i, 0)),
                  pl.BlockSpec((tm, MLA_WIDTH), lambda i, j: (i, 0)),
                  pl.BlockSpec((D_MODEL, tn), lambda i, j: (0, j)),
                  pl.BlockSpec((D_MODEL, tn), lambda i, j: (0, j)),
                  pl.BlockSpec((BAND_WIDTH, tn), lambda i, j: (0, j)),
                  pl.BlockSpec((MLA_WIDTH, tn), lambda i, j: (0, j))],
        out_specs=pl.BlockSpec((tm, tn), lambda i, j: (i, j)),
        out_shape=jax.ShapeDtypeStruct((m, D_MODEL), BF16),
        compiler_params=_params(2, 62 * 1024 * 1024), name="mix")(xn, oa, ob, wga, wgb, wpa, wpb)


def _outproj_kernel(x_ref, mix_ref, w_ref, h_ref):
    h_ref[...] = x_ref[...] + _dot(mix_ref[...], w_ref[...])


def _outproj(x, mix, w, tm):
    m = x.shape[0]
    rows = pl.BlockSpec((tm, D_MODEL), lambda i: (i, 0))
    return pl.pallas_call(
        _outproj_kernel,
        grid=(m // tm,),
        in_specs=[rows, rows, pl.BlockSpec((D_MODEL, D_MODEL), lambda i: (0, 0))],
        out_specs=rows,
        out_shape=jax.ShapeDtypeStruct((m, D_MODEL), F32),
        compiler_params=_params(1), name="outproj")(x, mix, w)


def _ffn_kernel(h_ref, g_ref, wu_ref, wd_ref, o_ref, hn_ref):
    def update(hn):
        u = jnp.maximum(_dot(hn, wu_ref[...]), 0.0)
        return _dot((u * u).astype(BF16), wd_ref[...])

    @pl.when(pl.program_id(1) == 0)
    def _():
        h = h_ref[...]
        hn = _rms(h, g_ref[...]).astype(BF16)
        hn_ref[...] = hn
        o_ref[...] = h + update(hn)

    @pl.when(pl.program_id(1) > 0)
    def _():
        o_ref[...] += update(hn_ref[...])


def _ffn(h, g, wu, wd, tm, tf=1024):
    m = h.shape[0]
    return pl.pallas_call(
        _ffn_kernel,
        grid=(m // tm, D_FF // tf),
        in_specs=[pl.BlockSpec((tm, D_MODEL), lambda i, f: (i, 0)),
                  pl.BlockSpec((1, D_MODEL), lambda i, f: (0, 0)),
                  pl.BlockSpec((D_MODEL, tf), lambda i, f: (0, f)),
                  pl.BlockSpec((tf, D_MODEL), lambda i, f: (f, 0))],
        out_specs=pl.BlockSpec((tm, D_MODEL), lambda i, f: (i, 0)),
        out_shape=jax.ShapeDtypeStruct((m, D_MODEL), F32),
        scratch_shapes=[pltpu.VMEM((tm, D_MODEL), BF16)],
        compiler_params=_params(2, 62 * 1024 * 1024), name="ffn")(h, g, wu, wd)


def _rope_table(pos, rows):
    half = MLA_ROPE // 2
    freqs = ROPE_THETA ** (-(jnp.arange(half, dtype=F32) / half))
    ang = pos.astype(F32)[:, None] * freqs[None, :]
    cos = jnp.cos(ang)
    sin = jnp.sin(ang)
    table = jnp.concatenate([cos, cos, -sin, sin], axis=1)
    return jnp.tile(table, (rows // table.shape[0], 1))


def _with_partner(w):
    half = MLA_ROPE // 2
    return jnp.concatenate([w, w[..., half:], w[..., :half]], axis=-1)


def _layer_weights(l, norm_mix_g, w_in, g_aq, g_ak, g_kv, g_kr, g_qn, g_qr, g_kn,
                   w_kv_b, w_pa, w_pb, w_out, norm_ffn_g, w_up, w_down):
    w = w_in[l].astype(BF16)
    c = 0
    parts = []
    for width in (3 * BAND_WIDTH, MLA_HEADS * MLA_QK, MLA_KV_RANK, MLA_ROPE, D_MODEL, D_MODEL):
        parts.append(w[:, c:c + width])
        c += width
    w_a, w_bq, w_ckv, w_kr, w_ga, w_gb = parts
    w_q = jnp.concatenate(
        [piece for h in range(MLA_HEADS)
         for piece in (w_bq[:, h * MLA_QK:h * MLA_QK + MLA_NOPE],
                       _with_partner(w_bq[:, h * MLA_QK + MLA_NOPE:(h + 1) * MLA_QK]))], axis=1)
    row = lambda g: g[l][None, :].astype(F32)
    return dict(
        gx=row(norm_mix_g),
        w_in=w,
        w_q=w_q,
        w_kr=_with_partner(w_kr),
        g_a=jnp.concatenate([jnp.tile(g_aq[l] * BAND_QSCALE, BAND_HEADS), jnp.tile(g_ak[l], BAND_HEADS),
                             jnp.ones((BAND_WIDTH,), F32)])[None, :],
        g_ak=row(g_ak),
        g_qn=row(g_qn),
        g_qr=_with_partner(g_qr[l])[None, :],
        g_kv=row(g_kv),
        g_kr=_with_partner(g_kr[l])[None, :],
        w_kv_b=w_kv_b[l].astype(BF16),
        g_kn=row(g_kn),
        w_ga=w_ga, w_gb=w_gb,
        w_pa=w_pa[l].astype(BF16), w_pb=w_pb[l].astype(BF16),
        w_out=w_out[l].astype(BF16),
        g_ffn=row(norm_ffn_g),
        w_up=w_up[l].astype(BF16), w_down=w_down[l].astype(BF16),
    )


def _merge_ffn(x2, xn, oa, ob, p, tm):
    mix = _mix(xn, oa, ob, p["w_ga"], p["w_gb"], p["w_pa"], p["w_pb"], tm)
    h = _outproj(x2, mix, p["w_out"], min(tm, 512))
    return _ffn(h, p["g_ffn"], p["w_up"], p["w_down"], tm)


def _cache_shape(kv, lead):
    return tuple(a.reshape(lead + (BAND_HEADS, BAND_HEAD_DIM)) for a in kv)


def kernel(x_prompt, x_sample, cache_a_k, cache_a_v, cache_mla_ckv, cache_mla_krope, norm_mix_g, w_in, g_aq, g_ak, rel_bias, g_kv, g_kr, g_qn, g_qr, g_kn, w_kv_b, w_pa, w_pb, w_out, norm_ffn_g, w_up, w_down):
    b, s, _ = x_prompt.shape
    bs, t, _ = x_sample.shape
    past = cache_mla_ckv.shape[2]
    n_band = cache_a_k.shape[2]
    depth = w_in.shape[0]
    keep = min(BAND_PAST, s)
    tm_p = 1024
    tq_mla = 256
    tm_s = bs * t
    assert s % tm_p == 0 and s % BAND_TQ == 0 and s % tq_mla == 0 and t == CHUNK and past % CHUNK == 0
    tm_cache = min(bs * past, 2048)
    assert (bs * past) % tm_cache == 0
    tm_keep = min(keep, 512)
    assert keep % tm_keep == 0 and s % tm_keep == 0 and (s - keep) % tm_keep == 0
    cs_p = _rope_table(jnp.arange(s), max(s, tm_p))
    cs_s = _rope_table(past + jnp.arange(t), tm_s)

    yp = x_prompt.reshape(b * s, D_MODEL)
    ys = x_sample.reshape(bs * t, D_MODEL)
    outs = [[] for _ in range(8)]
    for l in range(depth):
        p = _layer_weights(l, norm_mix_g, w_in, g_aq, g_ak, g_kv, g_kr, g_qn, g_qr, g_kn,
                           w_kv_b, w_pa, w_pb, w_out, norm_ffn_g, w_up, w_down)
        bias, bias_t = _band_bias(rel_bias[l])

        xn, qkv, vt, q_mla, ckv, ckv_bf, krr, kr = _proj(yp, p, cs_p, tm_p)
        kb = keep // tm_keep
        ak, av = _cache_shape(_kv_f32(
            yp, p["gx"], p["w_in"], p["g_ak"], tm_keep, b * keep,
            lambda i: (i // kb) * (s // tm_keep) + (s - keep) // tm_keep + i % kb), (b, keep))
        oa = _band_prompt(qkv.reshape(b, s, 3 * BAND_WIDTH), vt, bias_t)
        k_mla, vt_mla = _expand(ckv_bf, p["w_kv_b"], p["g_kn"], krr, tm_p, True, tq_mla)
        ob = _mla_prompt(q_mla.reshape(b, s, -1), k_mla.reshape(b, s, -1),
                         vt_mla.reshape(b, s // tq_mla, MLA_WIDTH, tq_mla), tq_mla)
        for o, val in zip(outs[:4], (ak, av, ckv.reshape(b, s, MLA_KV_RANK), kr.reshape(b, s, MLA_ROPE))):
            o.append(val)
        yp = _merge_ffn(yp, xn, oa.reshape(b * s, BAND_WIDTH), ob.reshape(b * s, MLA_WIDTH), p, tm_p)

        xn, qkv, _, q_mla, ckv, ckv_bf, krr, kr = _proj(ys, p, cs_s, tm_s)
        ak, av = _cache_shape(_kv_f32(ys, p["gx"], p["w_in"], p["g_ak"], tm_s, bs * t, lambda i: i), (bs, t))
        oa = _band_sample(qkv.reshape(bs, t, 3 * BAND_WIDTH),
                          cache_a_k[l].reshape(bs, n_band * BAND_HEADS, BAND_HEAD_DIM),
                          cache_a_v[l].reshape(bs, n_band * BAND_HEADS, BAND_HEAD_DIM), bias)
        k_old, v_old = _expand(cache_mla_ckv[l].reshape(bs * past, MLA_KV_RANK), p["w_kv_b"], p["g_kn"],
                               cache_mla_krope[l].reshape(bs * past, MLA_ROPE), tm_cache, False)
        k_new, v_new = _expand(ckv_bf, p["w_kv_b"], p["g_kn"], krr, tm_s, False)
        ob = _mla_sample(q_mla.reshape(bs, t, -1), k_old.reshape(bs, past, -1), v_old.reshape(bs, past, -1),
                         k_new.reshape(bs, t, -1), v_new.reshape(bs, t, -1))
        for o, val in zip(outs[4:], (ak, av, ckv.reshape(bs, t, MLA_KV_RANK), kr.reshape(bs, t, MLA_ROPE))):
            o.append(val)
        ys = _merge_ffn(ys, xn, oa.reshape(bs * t, BAND_WIDTH), ob.reshape(bs * t, MLA_WIDTH), p, tm_s)

    return (yp.reshape(b, s, D_MODEL), ys.reshape(bs, t, D_MODEL)) + tuple(jnp.stack(o) for o in outs)
```

```python
import functools

import jax
import jax.numpy as jnp
from jax import lax
from jax.experimental import pallas as pl
from jax.experimental.pallas import tpu as pltpu

F32 = jnp.float32
BF16 = jnp.bfloat16

D_MODEL = 2048
CHUNK = 64
BAND_PAST = 8 * CHUNK
BAND_HEADS = 8
BAND_HEAD_DIM = 128
BAND_WIDTH = BAND_HEADS * BAND_HEAD_DIM
REL_MAX = 256
REL_SIZE = (CHUNK - 1) + REL_MAX + 1
BAND_SCALE = BAND_HEAD_DIM ** -0.5
MLA_HEADS = 8
MLA_NOPE = 128
MLA_ROPE = 64
MLA_QK = MLA_NOPE + MLA_ROPE
MLA_V = 128
MLA_WIDTH = MLA_HEADS * MLA_V
MLA_KV_RANK = 512
MLA_SCALE = MLA_QK ** -0.5
ROPE_THETA = 10000.0
D_FF = 4 * D_MODEL
EPS = 1e-6
NEG = -1e30
LOG2E = 1.4426950408889634
MLA_QSCALE = MLA_SCALE * LOG2E
BAND_QSCALE = BAND_SCALE * LOG2E

LANES = 128
MLA_QK_PAD = 2 * LANES
BAND_TQ = 256
BAND_WIN = BAND_TQ + BAND_PAST
BAND_PIECES = BAND_WIN // BAND_TQ
VMEM_LIMIT = 56 * 1024 * 1024
PROJ_TN = 512
PROJ_ROW_CHUNKS = 4
SUM_ROWS = 16
BAND_QB = 4
BAND_N_BIAS = min(BAND_QB, BAND_PIECES)
BAND_SCORE_BUFFERS = 3
MLA_EXPAND_SLABS = 4
MLA_PV_GROUP = 4
MLA_SCORE_BUFFERS = 3
SAMPLE_SCORE_BUFFERS = 3


def _params(n_axes, vmem=VMEM_LIMIT):
    return pltpu.CompilerParams(dimension_semantics=("arbitrary",) * n_axes,
                                vmem_limit_bytes=vmem)


def _rms(a, g):
    return a * lax.rsqrt(jnp.mean(a * a, axis=-1, keepdims=True) + EPS) * g


def _dot(a, b):
    return jnp.dot(a, b, preferred_element_type=F32)


def _dot_nt(a, b):
    return lax.dot_general(a, b, (((1,), (1,)), ((), ())), preferred_element_type=F32)


def _rope_pair(ar, g, cs):
    lane = lax.broadcasted_iota(jnp.int32, (1, LANES), 1)
    first = (lane < MLA_ROPE).astype(F32)
    ss = jnp.sum(ar * ar * first, axis=-1, keepdims=True) / MLA_ROPE
    return ar * lax.rsqrt(ss + EPS) * g * cs


_NB_BAND_NORM = 2 * BAND_WIDTH // PROJ_TN
_NB_BAND = 3 * BAND_WIDTH // PROJ_TN
_NB_MLA_Q = MLA_HEADS * MLA_QK_PAD // PROJ_TN
_NB_PROJ = _NB_BAND + _NB_MLA_Q + 1
_CKV_BLOCK = (3 * BAND_WIDTH + MLA_HEADS * MLA_QK) // PROJ_TN
assert MLA_KV_RANK == PROJ_TN and _CKV_BLOCK * PROJ_TN == 3 * BAND_WIDTH + MLA_HEADS * MLA_QK


def _proj_kernel(x_ref, gx_ref, w_ref, wq_ref, wkr_ref, gh_ref, gqn_ref, gqr_ref, gkv_ref, gkr_ref, cs_ref,
                 xn_out_ref, qkv_ref, vt_ref, q_ref, ckv_ref, ckvb_ref, krr_ref, kr_ref, xn_ref):
    j = pl.program_id(1)
    tm = x_ref.shape[0]
    row_chunks = [slice(c * tm // PROJ_ROW_CHUNKS, (c + 1) * tm // PROJ_ROW_CHUNKS) for c in range(PROJ_ROW_CHUNKS)]

    def product(r, weights=w_ref):
        return _dot(xn_ref[r, :], weights[...])

    def band_norm(r, acc):
        for k in range(PROJ_TN // LANES):
            sl = slice(k * LANES, (k + 1) * LANES)
            qkv_ref[r, sl] = _rms(acc[:, sl], gh_ref[:, sl]).astype(BF16)

    @pl.when(j == 0)
    def _():
        for r in row_chunks:
            xn = _rms(x_ref[r, :], gx_ref[...]).astype(BF16)
            xn_ref[r, :] = xn
            xn_out_ref[r, :] = xn
            band_norm(r, _dot(xn, w_ref[...]))

    @pl.when(jnp.logical_and(j > 0, j < _NB_BAND_NORM))
    def _():
        for r in row_chunks:
            band_norm(r, product(r))

    @pl.when(jnp.logical_and(j >= _NB_BAND_NORM, j < _NB_BAND))
    def _():
        for r in row_chunks:
            acc = product(r)
            qkv_ref[r, :] = acc.astype(BF16)
            vt_ref[:, r] = acc.T.astype(BF16)

    @pl.when(jnp.logical_and(j >= _NB_BAND, j < _NB_BAND + _NB_MLA_Q))
    def _():
        for r in row_chunks:
            acc = product(r, wq_ref)
            cs = cs_ref[r, :]
            for k in range(PROJ_TN // MLA_QK_PAD):
                c0 = k * MLA_QK_PAD
                qn = _rms(acc[:, c0:c0 + LANES], gqn_ref[...]) * MLA_QSCALE
                qr = _rope_pair(acc[:, c0 + LANES:c0 + 2 * LANES], gqr_ref[...], cs) * MLA_QSCALE
                q_ref[r, c0:c0 + LANES] = qn.astype(BF16)
                q_ref[r, c0 + LANES:c0 + 2 * LANES] = qr.astype(BF16)

    @pl.when(j == _NB_BAND + _NB_MLA_Q)
    def _():
        for r in row_chunks:
            cn = _rms(product(r), gkv_ref[...])
            ckv_ref[r, :] = cn
            ckvb_ref[r, :] = cn.astype(BF16)
            t = _rope_pair(product(r, wkr_ref), gkr_ref[...], cs_ref[r, :])
            krr = t + pltpu.roll(t, MLA_ROPE, 1)
            krr_ref[r, :] = krr
            kr_ref[r, :] = krr[:, :MLA_ROPE]


def _proj(x, p, cs, tm):
    m = x.shape[0]
    tn = PROJ_TN
    n_cs = cs.shape[0] // tm
    const = lambda i, j: (0, 0)
    rows = lambda i, j: (i, 0)
    return pl.pallas_call(
        _proj_kernel,
        grid=(m // tm, _NB_PROJ),
        in_specs=[pl.BlockSpec((tm, D_MODEL), rows),
                  pl.BlockSpec((1, D_MODEL), const),
                  pl.BlockSpec((D_MODEL, tn), lambda i, j: (0, jnp.where(
                      j < _NB_BAND, j, jnp.where(j == _NB_PROJ - 1, _CKV_BLOCK, _NB_BAND - 1)))),
                  pl.BlockSpec((D_MODEL, tn), lambda i, j: (0, jnp.clip(j - _NB_BAND, 0, _NB_MLA_Q - 1))),
                  pl.BlockSpec((D_MODEL, LANES), const),
                  pl.BlockSpec((1, tn), lambda i, j: (0, jnp.minimum(j, _NB_BAND - 1))),
                  pl.BlockSpec((1, LANES), const),
                  pl.BlockSpec((1, LANES), const),
                  pl.BlockSpec((1, MLA_KV_RANK), const),
                  pl.BlockSpec((1, LANES), const),
                  pl.BlockSpec((tm, LANES), lambda i, j: (i % n_cs, 0))],
        out_specs=[pl.BlockSpec((tm, D_MODEL), rows),
                   pl.BlockSpec((tm, tn), lambda i, j: (i, jnp.minimum(j, _NB_BAND - 1))),
                   pl.BlockSpec((tn, tm), lambda i, j: (jnp.clip(j - _NB_BAND_NORM, 0, _NB_BAND - _NB_BAND_NORM - 1), i)),
                   pl.BlockSpec((tm, tn), lambda i, j: (i, jnp.clip(j - _NB_BAND, 0, _NB_MLA_Q - 1))),
                   pl.BlockSpec((tm, MLA_KV_RANK), rows),
                   pl.BlockSpec((tm, MLA_KV_RANK), rows),
                   pl.BlockSpec((tm, LANES), rows),
                   pl.BlockSpec((tm, MLA_ROPE), rows)],
        out_shape=[jax.ShapeDtypeStruct((m, D_MODEL), BF16),
                   jax.ShapeDtypeStruct((m, 3 * BAND_WIDTH), BF16),
                   jax.ShapeDtypeStruct((BAND_WIDTH, m), BF16),
                   jax.ShapeDtypeStruct((m, MLA_HEADS * MLA_QK_PAD), BF16),
                   jax.ShapeDtypeStruct((m, MLA_KV_RANK), F32),
                   jax.ShapeDtypeStruct((m, MLA_KV_RANK), BF16),
                   jax.ShapeDtypeStruct((m, LANES), F32),
                   jax.ShapeDtypeStruct((m, MLA_ROPE), F32)],
        scratch_shapes=[pltpu.VMEM((tm, D_MODEL), BF16)],
        compiler_params=_params(2, 62 * 1024 * 1024), name="proj")(
            x, p["gx"], p["w_in"], p["w_q"], p["w_kr"], p["g_a"], p["g_qn"], p["g_qr"], p["g_kv"], p["g_kr"], cs)


def _kv_f32_kernel(x_ref, gx_ref, w_ref, g_ref, k_ref, v_ref, xn_ref):
    j = pl.program_id(1)
    tm = x_ref.shape[0]

    def heads(o_ref, acc, norm):
        for h in range(BAND_HEADS):
            y = acc[:, h * BAND_HEAD_DIM:(h + 1) * BAND_HEAD_DIM]
            o_ref[pl.ds(h, tm, stride=BAND_HEADS), :] = _rms(y, g_ref[...]) if norm else y

    @pl.when(j == 0)
    def _():
        xn = _rms(x_ref[...], gx_ref[...]).astype(BF16)
        xn_ref[...] = xn
        heads(k_ref, _dot(xn, w_ref[...]), True)

    @pl.when(j == 1)
    def _():
        heads(v_ref, _dot(xn_ref[...], w_ref[...]), False)


def _kv_f32(x, gx, w_in, g, tm, n_rows, row_block):
    out = pl.BlockSpec((tm * BAND_HEADS, BAND_HEAD_DIM), lambda i, j: (i, 0))
    shape = jax.ShapeDtypeStruct((n_rows * BAND_HEADS, BAND_HEAD_DIM), F32)
    return pl.pallas_call(
        _kv_f32_kernel,
        grid=(n_rows // tm, 2),
        in_specs=[pl.BlockSpec((tm, D_MODEL), lambda i, j: (row_block(i), 0)),
                  pl.BlockSpec((1, D_MODEL), lambda i, j: (0, 0)),
                  pl.BlockSpec((D_MODEL, BAND_WIDTH), lambda i, j: (0, j + 1)),
                  pl.BlockSpec((1, LANES), lambda i, j: (0, 0))],
        out_specs=[out, out],
        out_shape=[shape, shape],
        scratch_shapes=[pltpu.VMEM((tm, D_MODEL), BF16)],
        compiler_params=_params(2), name="kv_f32")(x, gx, w_in, g)


def _expand_kernel(c_ref, w_ref, gkn_ref, krr_ref, k_ref, v_ref, *, v_transposed):
    c = c_ref[...].astype(BF16)
    krr = krr_ref[...].astype(BF16)
    if krr.shape[1] == MLA_ROPE:
        krr = jnp.concatenate([krr, krr], axis=1)
    for h in range(MLA_HEADS):
        a = _dot(c, w_ref[:, h * 2 * LANES:(h + 1) * 2 * LANES])
        k_ref[:, h * MLA_QK_PAD:h * MLA_QK_PAD + LANES] = _rms(a[:, :LANES], gkn_ref[...]).astype(BF16)
        k_ref[:, h * MLA_QK_PAD + LANES:(h + 1) * MLA_QK_PAD] = krr
        if v_transposed:
            slab = v_ref.shape[2]
            for n in range(v_ref.shape[0]):
                v_ref[n, h * MLA_V:(h + 1) * MLA_V, :] = a[n * slab:(n + 1) * slab, LANES:].T.astype(BF16)
        else:
            v_ref[:, h * MLA_V:(h + 1) * MLA_V] = a[:, LANES:].astype(BF16)


def _expand(c, w, gkn, krr, tm, v_transposed, slab=None):
    m = c.shape[0]
    if v_transposed:
        v_spec = pl.BlockSpec((tm // slab, MLA_WIDTH, slab), lambda i: (i, 0, 0))
        v_shape = jax.ShapeDtypeStruct((m // slab, MLA_WIDTH, slab), BF16)
    else:
        v_spec = pl.BlockSpec((tm, MLA_WIDTH), lambda i: (i, 0))
        v_shape = jax.ShapeDtypeStruct((m, MLA_WIDTH), BF16)
    return pl.pallas_call(
        functools.partial(_expand_kernel, v_transposed=v_transposed),
        grid=(m // tm,),
        in_specs=[pl.BlockSpec((tm, MLA_KV_RANK), lambda i: (i, 0)),
                  pl.BlockSpec(w.shape, lambda i: (0, 0)),
                  pl.BlockSpec((1, LANES), lambda i: (0, 0)),
                  pl.BlockSpec((tm, krr.shape[1]), lambda i: (i, 0))],
        out_specs=[pl.BlockSpec((tm, MLA_HEADS * MLA_QK_PAD), lambda i: (i, 0)), v_spec],
        out_shape=[jax.ShapeDtypeStruct((m, MLA_HEADS * MLA_QK_PAD), BF16), v_shape],
        compiler_params=_params(1), name="expand")(c, w, gkn, krr)


def _band_bias_kernel(b_ref, o_ref, ot_ref):
    t = pltpu.roll(jnp.broadcast_to(b_ref[0], (BAND_TQ, 2 * BAND_PAST)), 0, 1,
                   stride=1, stride_axis=0)[:, :BAND_WIN] * LOG2E
    qc = lax.broadcasted_iota(jnp.int32, (BAND_TQ, BAND_WIN), 0) // CHUNK
    kc = lax.broadcasted_iota(jnp.int32, (BAND_TQ, BAND_WIN), 1) // CHUNK
    t = jnp.where(kc >= qc, jnp.where(kc <= qc + BAND_PAST // CHUNK, t, NEG), NEG)
    o_ref[0] = t
    tt = t.T
    key = lax.broadcasted_iota(jnp.int32, (BAND_WIN, BAND_TQ), 0)
    for v in range(BAND_PIECES):
        n_missing = BAND_PIECES - 1 - v
        ot_ref[v, 0] = jnp.where(key >= n_missing * BAND_TQ, tt, NEG)


def _band_bias(rel_bias):
    far = jnp.broadcast_to(rel_bias[:, REL_SIZE - 1:], (BAND_HEADS, BAND_PAST - REL_MAX))
    near = jnp.broadcast_to(rel_bias[:, :1], (BAND_HEADS, BAND_WIN - BAND_PAST - CHUNK))
    wrap = jnp.broadcast_to(rel_bias[:, REL_SIZE - 1:], (BAND_HEADS, 2 * BAND_PAST - BAND_WIN))
    row0 = jnp.concatenate([far, rel_bias[:, ::-1], near, wrap], axis=1)
    assert row0.shape == (BAND_HEADS, 2 * BAND_PAST)
    return pl.pallas_call(
        _band_bias_kernel,
        grid=(BAND_HEADS,),
        in_specs=[pl.BlockSpec((1, 1, 2 * BAND_PAST), lambda h: (h, 0, 0))],
        out_specs=[pl.BlockSpec((1, BAND_TQ, BAND_WIN), lambda h: (h, 0, 0)),
                   pl.BlockSpec((BAND_PIECES, 1, BAND_WIN, BAND_TQ), lambda h: (0, h, 0, 0))],
        out_shape=[jax.ShapeDtypeStruct((BAND_HEADS, BAND_TQ, BAND_WIN), F32),
                   jax.ShapeDtypeStruct((BAND_PIECES, BAND_HEADS, BAND_WIN, BAND_TQ), F32)],
        compiler_params=_params(1), name="band_bias")(row0[:, None, :])


def _softmax_pv(s, v):
    m = functools.reduce(jnp.maximum, [jnp.max(sp, axis=-1, keepdims=True) for sp in s])
    e = [jnp.exp2(sp - m) for sp in s]
    l = functools.reduce(jnp.add, [jnp.sum(ep, axis=-1, keepdims=True) for ep in e])
    o = functools.reduce(jnp.add, [_dot(ep.astype(BF16), vp) for ep, vp in zip(e, v)])
    return (o / l).astype(BF16)


def _pipelined_heads(n_heads, scores, finish, bufs):
    ahead = len(bufs) - 1

    def issue(h):
        for ref, piece in zip(bufs[h % len(bufs)], scores(h)):
            ref[...] = piece

    for h in range(min(ahead, n_heads)):
        issue(h)
    for h in range(n_heads):
        if h + ahead < n_heads:
            issue(h + ahead)
        finish(h, [ref[...] for ref in bufs[h % len(bufs)]])


def _band_prompt_kernel(q_ref, *refs):
    n_piece = BAND_QB + BAND_PIECES - 1
    k_refs, vt_refs = refs[:n_piece], refs[n_piece:2 * n_piece]
    bias_refs = refs[2 * n_piece:2 * n_piece + BAND_N_BIAS]
    o_ref = refs[2 * n_piece + BAND_N_BIAS]
    bufs = refs[2 * n_piece + BAND_N_BIAS + 1:]

    def head(h):
        return slice(h * BAND_HEAD_DIM, (h + 1) * BAND_HEAD_DIM)

    def rows(n):
        return slice(n * BAND_TQ, (n + 1) * BAND_TQ)

    def scores(n, h):
        k = jnp.concatenate([r[0, :, head(h)] for r in k_refs[n:n + BAND_PIECES]], axis=0)
        return _dot_nt(k, q_ref[0, rows(n), head(h)]) + bias_refs[min(n, BAND_N_BIAS - 1)][0, h]

    units = [(n, h) for n in range(BAND_QB) for h in range(BAND_HEADS)]
    ahead = len(bufs) - 1
    for u in range(ahead):
        bufs[u][...] = scores(*units[u])
    for u, (n, h) in enumerate(units):
        if u + ahead < len(units):
            bufs[(u + ahead) % len(bufs)][...] = scores(*units[u + ahead])
        s = bufs[u % len(bufs)][...]
        vt = jnp.concatenate([r[head(h), :] for r in vt_refs[n:n + BAND_PIECES]], axis=1)
        vt = jnp.concatenate([vt, jnp.ones((SUM_ROWS, BAND_WIN), BF16)], axis=0)
        e = jnp.exp2(s - jnp.max(s, axis=0, keepdims=True))
        o = _dot(vt, e.astype(BF16))
        o = o[:BAND_HEAD_DIM] / o[BAND_HEAD_DIM:BAND_HEAD_DIM + 1]
        o_ref[0, rows(n), head(h)] = o.T.astype(BF16)


def _band_prompt(qkv, vt, bias_t):
    b, s, _ = qkv.shape
    nq = s // BAND_TQ
    assert nq % BAND_QB == 0
    n_piece = BAND_QB + BAND_PIECES - 1
    piece = (1, BAND_TQ, BAND_WIDTH)
    group = (1, BAND_QB * BAND_TQ, BAND_WIDTH)

    def first_block(i, p):
        return jnp.maximum(i * BAND_QB - (BAND_PIECES - 1) + p, 0)

    def k_spec(p):
        return pl.BlockSpec(piece, lambda bi, i: (bi, first_block(i, p), 1))

    def vt_spec(p):
        return pl.BlockSpec((BAND_WIDTH, BAND_TQ), lambda bi, i: (0, bi * nq + first_block(i, p)))

    def bias_spec(n):
        return pl.BlockSpec((1,) + bias_t.shape[1:],
                            lambda bi, i: (jnp.minimum(i * BAND_QB + n, BAND_PIECES - 1), 0, 0, 0))

    return pl.pallas_call(
        _band_prompt_kernel,
        grid=(b, nq // BAND_QB),
        in_specs=[pl.BlockSpec(group, lambda bi, i: (bi, i, 0))]
                 + [k_spec(p) for p in range(n_piece)]
                 + [vt_spec(p) for p in range(n_piece)]
                 + [bias_spec(n) for n in range(BAND_N_BIAS)],
        out_specs=pl.BlockSpec(group, lambda bi, i: (bi, i, 0)),
        out_shape=jax.ShapeDtypeStruct((b, s, BAND_WIDTH), BF16),
        scratch_shapes=[pltpu.VMEM((BAND_WIN, BAND_TQ), F32)] * BAND_SCORE_BUFFERS,
        compiler_params=_params(2, 62 * 1024 * 1024), name="band_prompt")(
            qkv, *([qkv] * n_piece), *([vt] * n_piece), *([bias_t] * BAND_N_BIAS))


def _band_sample_kernel(q_ref, kn_ref, vn_ref, kc_ref, vc_ref, bias_ref, o_ref, *bufs, n_past):
    t = q_ref.shape[1]

    def head(h):
        return slice(h * BAND_HEAD_DIM, (h + 1) * BAND_HEAD_DIM)

    def cached(ref, h):
        return ref[0, pl.ds(h, n_past, stride=BAND_HEADS), :].astype(BF16)

    def scores(h):
        q = q_ref[0, :, head(h)]
        return (_dot_nt(q, cached(kc_ref, h)) + bias_ref[h, :t, :n_past],
                _dot_nt(q, kn_ref[0, :, head(h)]) + bias_ref[h, :t, n_past:n_past + t])

    def finish(h, s):
        o_ref[0, :, head(h)] = _softmax_pv(s, [cached(vc_ref, h), vn_ref[0, :, head(h)]])

    _pipelined_heads(BAND_HEADS, scores, finish, list(zip(bufs[0::2], bufs[1::2])))


def _band_sample(qkv, k_cache, v_cache, bias):
    b, t, _ = qkv.shape
    n_past = k_cache.shape[1] // BAND_HEADS
    new = (1, t, BAND_WIDTH)
    old = (1, n_past * BAND_HEADS, BAND_HEAD_DIM)
    return pl.pallas_call(
        functools.partial(_band_sample_kernel, n_past=n_past),
        grid=(b,),
        in_specs=[pl.BlockSpec(new, lambda bi: (bi, 0, 0)),
                  pl.BlockSpec(new, lambda bi: (bi, 0, 1)),
                  pl.BlockSpec(new, lambda bi: (bi, 0, 2)),
                  pl.BlockSpec(old, lambda bi: (bi, 0, 0)),
                  pl.BlockSpec(old, lambda bi: (bi, 0, 0)),
                  pl.BlockSpec(bias.shape, lambda bi: (0, 0, 0))],
        out_specs=pl.BlockSpec(new, lambda bi: (bi, 0, 0)),
        out_shape=jax.ShapeDtypeStruct((b, t, BAND_WIDTH), BF16),
        scratch_shapes=[pltpu.VMEM((t, n_past), F32), pltpu.VMEM((t, t), F32)] * SAMPLE_SCORE_BUFFERS,
        compiler_params=_params(1), name="band_sample")(qkv, qkv, qkv, k_cache, v_cache, bias)


def _mla_prompt_kernel(q_ref, c_ref, w_ref, gkn_ref, krr_ref, o_ref, mask_ref, k_ref, vt_ref, *bufs, tq, nq):
    for n in range(nq // MLA_EXPAND_SLABS):
        r = slice(n * MLA_EXPAND_SLABS * tq, (n + 1) * MLA_EXPAND_SLABS * tq)
        a = _dot(c_ref[0, r, :], w_ref[...])
        k_ref[0, r, :LANES] = _rms(a[:, :LANES], gkn_ref[...]).astype(BF16)
        k_ref[0, r, LANES:] = krr_ref[0, r, :].astype(BF16)
        for m in range(MLA_EXPAND_SLABS):
            vt_ref[0, n * MLA_EXPAND_SLABS + m] = a[m * tq:(m + 1) * tq, LANES:].T.astype(BF16)

    def scores(i):
        return _dot_nt(k_ref[0, :(i + 1) * tq, :], q_ref[0, i * tq:(i + 1) * tq, :])

    kc = lax.broadcasted_iota(jnp.int32, (tq, tq), 0) // CHUNK
    qc = lax.broadcasted_iota(jnp.int32, (tq, tq), 1) // CHUNK
    mask_ref[...] = jnp.where(kc <= qc, 0.0, NEG)

    ahead = len(bufs) - 1
    for i in range(min(ahead, nq)):
        bufs[i][:(i + 1) * tq] = scores(i)
    for i in range(nq):
        if i + ahead < nq:
            bufs[(i + ahead) % len(bufs)][:(i + ahead + 1) * tq] = scores(i + ahead)
        buf = bufs[i % len(bufs)]
        buf[i * tq:(i + 1) * tq] = buf[i * tq:(i + 1) * tq] + mask_ref[...]
        m = jnp.max(buf[:(i + 1) * tq], axis=0, keepdims=True)
        acc = None
        for g in range(0, i + 1, MLA_PV_GROUP):
            n = min(MLA_PV_GROUP, i + 1 - g)
            vt = jnp.concatenate([vt_ref[0, j] for j in range(g, g + n)], axis=1)
            vt = jnp.concatenate([vt, jnp.ones((SUM_ROWS, n * tq), BF16)], axis=0)
            part = _dot(vt, jnp.exp2(buf[g * tq:(g + n) * tq] - m).astype(BF16))
            acc = part if acc is None else acc + part
        o_ref[0, i * tq:(i + 1) * tq, :] = (acc[:MLA_V] / acc[MLA_V:MLA_V + 1]).T.astype(BF16)


def _mla_prompt(q, c, w, gkn, krr, tq):
    b, s, _ = q.shape
    nq = s // tq
    assert nq % MLA_EXPAND_SLABS == 0
    return pl.pallas_call(
        functools.partial(_mla_prompt_kernel, tq=tq, nq=nq),
        grid=(b, MLA_HEADS),
        in_specs=[pl.BlockSpec((1, s, MLA_QK_PAD), lambda bi, h: (bi, 0, h)),
                  pl.BlockSpec((1, s, MLA_KV_RANK), lambda bi, h: (bi, 0, 0)),
                  pl.BlockSpec((MLA_KV_RANK, 2 * LANES), lambda bi, h: (0, h)),
                  pl.BlockSpec((1, LANES), lambda bi, h: (0, 0)),
                  pl.BlockSpec((1, s, LANES), lambda bi, h: (bi, 0, 0))],
        out_specs=pl.BlockSpec((1, s, MLA_V), lambda bi, h: (bi, 0, h)),
        out_shape=jax.ShapeDtypeStruct((b, s, MLA_WIDTH), BF16),
        scratch_shapes=[pltpu.VMEM((tq, tq), F32), pltpu.VMEM((1, s, MLA_QK_PAD), BF16),
                        pltpu.VMEM((1, nq, MLA_V, tq), BF16)]
                       + [pltpu.VMEM((s, tq), F32)] * MLA_SCORE_BUFFERS,
        compiler_params=_params(2), name="mla_prompt")(q, c, w, gkn, krr)


def _mla_sample_kernel(q_ref, ko_ref, vo_ref, kn_ref, vn_ref, o_ref, *bufs):
    def scores(h):
        qk = slice(h * MLA_QK_PAD, (h + 1) * MLA_QK_PAD)
        q = q_ref[0, :, qk]
        return _dot_nt(q, ko_ref[0, :, qk]), _dot_nt(q, kn_ref[0, :, qk])

    def finish(h, s):
        hv = slice(h * MLA_V, (h + 1) * MLA_V)
        o_ref[0, :, hv] = _softmax_pv(s, [vo_ref[0, :, hv], vn_ref[0, :, hv]])

    _pipelined_heads(MLA_HEADS, scores, finish, list(zip(bufs[0::2], bufs[1::2])))


def _mla_sample(q, k_old, v_old, k_new, v_new):
    b, t, _ = q.shape

    def whole(a):
        return pl.BlockSpec((1,) + a.shape[1:], lambda bi: (bi, 0, 0))

    return pl.pallas_call(
        _mla_sample_kernel,
        grid=(b,),
        in_specs=[whole(q), whole(k_old), whole(v_old), whole(k_new), whole(v_new)],
        out_specs=pl.BlockSpec((1, t, MLA_WIDTH), lambda bi: (bi, 0, 0)),
        out_shape=jax.ShapeDtypeStruct((b, t, MLA_WIDTH), BF16),
        scratch_shapes=[pltpu.VMEM((t, k_old.shape[1]), F32), pltpu.VMEM((t, k_new.shape[1]), F32)]
                       * SAMPLE_SCORE_BUFFERS,
        compiler_params=_params(1), name="mla_sample")(q, k_old, v_old, k_new, v_new)


def _mix_kernel(xn_ref, oa_ref, ob_ref, wga_ref, wgb_ref, wpa_ref, wpb_ref, o_ref):
    xn = xn_ref[...]
    a = jax.nn.sigmoid(_dot(xn, wga_ref[...])) * _dot(oa_ref[...], wpa_ref[...])
    b = jax.nn.sigmoid(_dot(xn, wgb_ref[...])) * _dot(ob_ref[...], wpb_ref[...])
    o_ref[...] = (a + b).astype(BF16)


def _mix(xn, oa, ob, wga, wgb, wpa, wpb, tm, tn=1024):
    m = xn.shape[0]
    return pl.pallas_call(
        _mix_kernel,
        grid=(m // tm, D_MODEL // tn),
        in_specs=[pl.BlockSpec((tm, D_MODEL), lambda i, j: (i, 0)),
                  pl.BlockSpec((tm, BAND_WIDTH), lambda i, j: (i, 0)),
                  pl.BlockSpec((tm, MLA_WIDTH), lambda i, j: (i, 0)),
                  pl.BlockSpec((D_MODEL, tn), lambda i, j: (0, j)),
                  pl.BlockSpec((D_MODEL, tn), lambda i, j: (0, j)),
                  pl.BlockSpec((BAND_WIDTH, tn), lambda i, j: (0, j)),
                  pl.BlockSpec((MLA_WIDTH, tn), lambda i, j: (0, j))],
        out_specs=pl.BlockSpec((tm, tn), lambda i, j: (i, j)),
        out_shape=jax.ShapeDtypeStruct((m, D_MODEL), BF16),
        compiler_params=_params(2, 62 * 1024 * 1024), name="mix")(xn, oa, ob, wga, wgb, wpa, wpb)


def _outproj_kernel(x_ref, mix_ref, w_ref, h_ref):
    h_ref[...] = x_ref[...] + _dot(mix_ref[...], w_ref[...])


def _outproj(x, mix, w, tm):
    m = x.shape[0]
    rows = pl.BlockSpec((tm, D_MODEL), lambda i: (i, 0))
    return pl.pallas_call(
        _outproj_kernel,
        grid=(m // tm,),
        in_specs=[rows, rows, pl.BlockSpec((D_MODEL, D_MODEL), lambda i: (0, 0))],
        out_specs=rows,
        out_shape=jax.ShapeDtypeStruct((m, D_MODEL), F32),
        compiler_params=_params(1), name="outproj")(x, mix, w)


def _ffn_kernel(h_ref, g_ref, wu_ref, wd_ref, o_ref, hn_ref):
    def update(hn):
        u = jnp.maximum(_dot(hn, wu_ref[...]), 0.0)
        return _dot((u * u).astype(BF16), wd_ref[...])

    @pl.when(pl.program_id(1) == 0)
    def _():
        h = h_ref[...]
        hn = _rms(h, g_ref[...]).astype(BF16)
        hn_ref[...] = hn
        o_ref[...] = h + update(hn)

    @pl.when(pl.program_id(1) > 0)
    def _():
        o_ref[...] += update(hn_ref[...])


def _ffn(h, g, wu, wd, tm, tf=1024):
    m = h.shape[0]
    return pl.pallas_call(
        _ffn_kernel,
        grid=(m // tm, D_FF // tf),
        in_specs=[pl.BlockSpec((tm, D_MODEL), lambda i, f: (i, 0)),
                  pl.BlockSpec((1, D_MODEL), lambda i, f: (0, 0)),
                  pl.BlockSpec((D_MODEL, tf), lambda i, f: (0, f)),
                  pl.BlockSpec((tf, D_MODEL), lambda i, f: (f, 0))],
        out_specs=pl.BlockSpec((tm, D_MODEL), lambda i, f: (i, 0)),
        out_shape=jax.ShapeDtypeStruct((m, D_MODEL), F32),
        scratch_shapes=[pltpu.VMEM((tm, D_MODEL), BF16)],
        compiler_params=_params(2, 62 * 1024 * 1024), name="ffn")(h, g, wu, wd)


def _rope_table(pos, rows):
    half = MLA_ROPE // 2
    freqs = ROPE_THETA ** (-(jnp.arange(half, dtype=F32) / half))
    ang = pos.astype(F32)[:, None] * freqs[None, :]
    cos = jnp.cos(ang)
    sin = jnp.sin(ang)
    table = jnp.concatenate([cos, cos, -sin, sin], axis=1)
    return jnp.tile(table, (rows // table.shape[0], 1))


def _with_partner(w):
    half = MLA_ROPE // 2
    return jnp.concatenate([w, w[..., half:], w[..., :half]], axis=-1)


def _layer_weights(l, norm_mix_g, w_in, g_aq, g_ak, g_kv, g_kr, g_qn, g_qr, g_kn,
                   w_kv_b, w_pa, w_pb, w_out, norm_ffn_g, w_up, w_down):
    w = w_in[l].astype(BF16)
    c = 0
    parts = []
    for width in (3 * BAND_WIDTH, MLA_HEADS * MLA_QK, MLA_KV_RANK, MLA_ROPE, D_MODEL, D_MODEL):
        parts.append(w[:, c:c + width])
        c += width
    w_a, w_bq, w_ckv, w_kr, w_ga, w_gb = parts
    w_q = jnp.concatenate(
        [piece for h in range(MLA_HEADS)
         for piece in (w_bq[:, h * MLA_QK:h * MLA_QK + MLA_NOPE],
                       _with_partner(w_bq[:, h * MLA_QK + MLA_NOPE:(h + 1) * MLA_QK]))], axis=1)
    row = lambda g: g[l][None, :].astype(F32)
    return dict(
        gx=row(norm_mix_g),
        w_in=w,
        w_q=w_q,
        w_kr=_with_partner(w_kr),
        g_a=jnp.concatenate([jnp.tile(g_aq[l] * BAND_QSCALE, BAND_HEADS), jnp.tile(g_ak[l], BAND_HEADS),
                             jnp.ones((BAND_WIDTH,), F32)])[None, :],
        g_ak=row(g_ak),
        g_qn=row(g_qn),
        g_qr=_with_partner(g_qr[l])[None, :],
        g_kv=row(g_kv),
        g_kr=_with_partner(g_kr[l])[None, :],
        w_kv_b=w_kv_b[l].astype(BF16),
        g_kn=row(g_kn),
        w_ga=w_ga, w_gb=w_gb,
        w_pa=w_pa[l].astype(BF16), w_pb=w_pb[l].astype(BF16),
        w_out=w_out[l].astype(BF16),
        g_ffn=row(norm_ffn_g),
        w_up=w_up[l].astype(BF16), w_down=w_down[l].astype(BF16),
    )


def _merge_ffn(x2, xn, oa, ob, p, tm):
    mix = _mix(xn, oa, ob, p["w_ga"], p["w_gb"], p["w_pa"], p["w_pb"], tm)
    h = _outproj(x2, mix, p["w_out"], min(tm, 512))
    return _ffn(h, p["g_ffn"], p["w_up"], p["w_down"], tm)


def _cache_shape(kv, lead):
    return tuple(a.reshape(lead + (BAND_HEADS, BAND_HEAD_DIM)) for a in kv)


def kernel(x_prompt, x_sample, cache_a_k, cache_a_v, cache_mla_ckv, cache_mla_krope, norm_mix_g, w_in, g_aq, g_ak, rel_bias, g_kv, g_kr, g_qn, g_qr, g_kn, w_kv_b, w_pa, w_pb, w_out, norm_ffn_g, w_up, w_down):
    b, s, _ = x_prompt.shape
    bs, t, _ = x_sample.shape
    past = cache_mla_ckv.shape[2]
    n_band = cache_a_k.shape[2]
    depth = w_in.shape[0]
    keep = min(BAND_PAST, s)
    tm_p = 1024
    tq_mla = 256
    tm_s = bs * t
    assert s % tm_p == 0 and s % BAND_TQ == 0 and s % tq_mla == 0 and t == CHUNK and past % CHUNK == 0
    tm_cache = min(bs * past, 2048)
    assert (bs * past) % tm_cache == 0
    tm_keep = min(keep, 512)
    assert keep % tm_keep == 0 and s % tm_keep == 0 and (s - keep) % tm_keep == 0
    cs_p = _rope_table(jnp.arange(s), max(s, tm_p))
    cs_s = _rope_table(past + jnp.arange(t), tm_s)

    yp = x_prompt.reshape(b * s, D_MODEL)
    ys = x_sample.reshape(bs * t, D_MODEL)
    outs = [[] for _ in range(8)]
    for l in range(depth):
        p = _layer_weights(l, norm_mix_g, w_in, g_aq, g_ak, g_kv, g_kr, g_qn, g_qr, g_kn,
                           w_kv_b, w_pa, w_pb, w_out, norm_ffn_g, w_up, w_down)
        bias, bias_t = _band_bias(rel_bias[l])

        xn, qkv, vt, q_mla, ckv, ckv_bf, krr, kr = _proj(yp, p, cs_p, tm_p)
        kb = keep // tm_keep
        ak, av = _cache_shape(_kv_f32(
            yp, p["gx"], p["w_in"], p["g_ak"], tm_keep, b * keep,
            lambda i: (i // kb) * (s // tm_keep) + (s - keep) // tm_keep + i % kb), (b, keep))
        oa = _band_prompt(qkv.reshape(b, s, 3 * BAND_WIDTH), vt, bias_t)
        ob = _mla_prompt(q_mla.reshape(b, s, -1), ckv_bf.reshape(b, s, MLA_KV_RANK), p["w_kv_b"], p["g_kn"],
                         krr.reshape(b, s, LANES), tq_mla)
        for o, val in zip(outs[:4], (ak, av, ckv.reshape(b, s, MLA_KV_RANK), kr.reshape(b, s, MLA_ROPE))):
            o.append(val)
        yp = _merge_ffn(yp, xn, oa.reshape(b * s, BAND_WIDTH), ob.reshape(b * s, MLA_WIDTH), p, tm_p)

        xn, qkv, _, q_mla, ckv, ckv_bf, krr, kr = _proj(ys, p, cs_s, tm_s)
        ak, av = _cache_shape(_kv_f32(ys, p["gx"], p["w_in"], p["g_ak"], tm_s, bs * t, lambda i: i), (bs, t))
        oa = _band_sample(qkv.reshape(bs, t, 3 * BAND_WIDTH),
                          cache_a_k[l].reshape(bs, n_band * BAND_HEADS, BAND_HEAD_DIM),
                          cache_a_v[l].reshape(bs, n_band * BAND_HEADS, BAND_HEAD_DIM), bias)
        k_old, v_old = _expand(cache_mla_ckv[l].reshape(bs * past, MLA_KV_RANK), p["w_kv_b"], p["g_kn"],
                               cache_mla_krope[l].reshape(bs * past, MLA_ROPE), tm_cache, False)
        k_new, v_new = _expand(ckv_bf, p["w_kv_b"], p["g_kn"], krr, tm_s, False)
        ob = _mla_sample(q_mla.reshape(bs, t, -1), k_old.reshape(bs, past, -1), v_old.reshape(bs, past, -1),
                         k_new.reshape(bs, t, -1), v_new.reshape(bs, t, -1))
        for o, val in zip(outs[4:], (ak, av, ckv.reshape(bs, t, MLA_KV_RANK), kr.reshape(bs, t, MLA_ROPE))):
            o.append(val)
        ys = _merge_ffn(ys, xn, oa.reshape(bs * t, BAND_WIDTH), ob.reshape(bs * t, MLA_WIDTH), p, tm_s)

    return (yp.reshape(b, s, D_MODEL), ys.reshape(bs, t, D_MODEL)) + tuple(jnp.stack(o) for o in outs)
```

```python
import functools

import jax
import jax.numpy as jnp
from jax import lax
from jax.experimental import pallas as pl
from jax.experimental.pallas import tpu as pltpu

F32 = jnp.float32
BF16 = jnp.bfloat16

D_MODEL = 2048
CHUNK = 64
BAND_PAST = 8 * CHUNK
BAND_HEADS = 8
BAND_HEAD_DIM = 128
BAND_WIDTH = BAND_HEADS * BAND_HEAD_DIM
REL_MAX = 256
REL_SIZE = (CHUNK - 1) + REL_MAX + 1
BAND_SCALE = BAND_HEAD_DIM ** -0.5
MLA_HEADS = 8
MLA_NOPE = 128
MLA_ROPE = 64
MLA_QK = MLA_NOPE + MLA_ROPE
MLA_V = 128
MLA_WIDTH = MLA_HEADS * MLA_V
MLA_KV_RANK = 512
MLA_SCALE = MLA_QK ** -0.5
ROPE_THETA = 10000.0
D_FF = 4 * D_MODEL
EPS = 1e-6
NEG = -1e30
LOG2E = 1.4426950408889634
MLA_QSCALE = MLA_SCALE * LOG2E
BAND_QSCALE = BAND_SCALE * LOG2E

LANES = 128
MLA_QK_PAD = 2 * LANES
BAND_TQ = 256
BAND_WIN = BAND_TQ + BAND_PAST
BAND_PIECES = BAND_WIN // BAND_TQ
VMEM_LIMIT = 56 * 1024 * 1024
PROJ_TN = 512
PROJ_ROW_CHUNKS = 4
SUM_ROWS = 16
BAND_QB = 4
BAND_N_BIAS = min(BAND_QB, BAND_PIECES)
BAND_SCORE_BUFFERS = 3
MLA_EXPAND_SLABS = 4
MLA_PV_GROUP = 4
MLA_SCORE_BUFFERS = 3
SAMPLE_SCORE_BUFFERS = 3


def _params(n_axes, vmem=VMEM_LIMIT):
    return pltpu.CompilerParams(dimension_semantics=("arbitrary",) * n_axes,
                                vmem_limit_bytes=vmem)


def _rms(a, g):
    return a * lax.rsqrt(jnp.mean(a * a, axis=-1, keepdims=True) + EPS) * g


def _dot(a, b):
    return jnp.dot(a, b, preferred_element_type=F32)


def _dot_nt(a, b):
    return lax.dot_general(a, b, (((1,), (1,)), ((), ())), preferred_element_type=F32)


def _rope_pair(ar, g, cs):
    lane = lax.broadcasted_iota(jnp.int32, (1, LANES), 1)
    first = (lane < MLA_ROPE).astype(F32)
    ss = jnp.sum(ar * ar * first, axis=-1, keepdims=True) / MLA_ROPE
    return ar * lax.rsqrt(ss + EPS) * g * cs


_NB_BAND_NORM = 2 * BAND_WIDTH // PROJ_TN
_NB_BAND = 3 * BAND_WIDTH // PROJ_TN
_NB_MLA_Q = MLA_HEADS * MLA_QK_PAD // PROJ_TN
_NB_PROJ = _NB_BAND + _NB_MLA_Q + 1
_CKV_BLOCK = (3 * BAND_WIDTH + MLA_HEADS * MLA_QK) // PROJ_TN
assert MLA_KV_RANK == PROJ_TN and _CKV_BLOCK * PROJ_TN == 3 * BAND_WIDTH + MLA_HEADS * MLA_QK


def _proj_kernel(x_ref, gx_ref, w_ref, wq_ref, wkr_ref, gh_ref, gqn_ref, gqr_ref, gkv_ref, gkr_ref, cs_ref,
                 xn_out_ref, qkv_ref, vt_ref, q_ref, ckv_ref, ckvb_ref, krr_ref, kr_ref, xn_ref):
    j = pl.program_id(1)
    tm = x_ref.shape[0]
    row_chunks = [slice(c * tm // PROJ_ROW_CHUNKS, (c + 1) * tm // PROJ_ROW_CHUNKS) for c in range(PROJ_ROW_CHUNKS)]

    def product(r, weights=w_ref):
        return _dot(xn_ref[r, :], weights[...])

    def band_norm(r, acc):
        for k in range(PROJ_TN // LANES):
            sl = slice(k * LANES, (k + 1) * LANES)
            qkv_ref[r, sl] = _rms(acc[:, sl], gh_ref[:, sl]).astype(BF16)

    @pl.when(j == 0)
    def _():
        for r in row_chunks:
            xn = _rms(x_ref[r, :], gx_ref[...]).astype(BF16)
            xn_ref[r, :] = xn
            xn_out_ref[r, :] = xn
            band_norm(r, _dot(xn, w_ref[...]))

    @pl.when(jnp.logical_and(j > 0, j < _NB_BAND_NORM))
    def _():
        for r in row_chunks:
            band_norm(r, product(r))

    @pl.when(jnp.logical_and(j >= _NB_BAND_NORM, j < _NB_BAND))
    def _():
        for r in row_chunks:
            acc = product(r)
            qkv_ref[r, :] = acc.astype(BF16)
            vt_ref[:, r] = acc.T.astype(BF16)

    @pl.when(jnp.logical_and(j >= _NB_BAND, j < _NB_BAND + _NB_MLA_Q))
    def _():
        for r in row_chunks:
            acc = product(r, wq_ref)
            cs = cs_ref[r, :]
            for k in range(PROJ_TN // MLA_QK_PAD):
                c0 = k * MLA_QK_PAD
                qn = _rms(acc[:, c0:c0 + LANES], gqn_ref[...]) * MLA_QSCALE
                qr = _rope_pair(acc[:, c0 + LANES:c0 + 2 * LANES], gqr_ref[...], cs) * MLA_QSCALE
                q_ref[r, c0:c0 + LANES] = qn.astype(BF16)
                q_ref[r, c0 + LANES:c0 + 2 * LANES] = qr.astype(BF16)

    @pl.when(j == _NB_BAND + _NB_MLA_Q)
    def _():
        for r in row_chunks:
            cn = _rms(product(r), gkv_ref[...])
            ckv_ref[r, :] = cn
            ckvb_ref[r, :] = cn.astype(BF16)
            t = _rope_pair(product(r, wkr_ref), gkr_ref[...], cs_ref[r, :])
            krr = t + pltpu.roll(t, MLA_ROPE, 1)
            krr_ref[r, :] = krr
            kr_ref[r, :] = krr[:, :MLA_ROPE]


def _proj(x, p, cs, tm):
    m = x.shape[0]
    tn = PROJ_TN
    n_cs = cs.shape[0] // tm
    const = lambda i, j: (0, 0)
    rows = lambda i, j: (i, 0)
    return pl.pallas_call(
        _proj_kernel,
        grid=(m // tm, _NB_PROJ),
        in_specs=[pl.BlockSpec((tm, D_MODEL), rows),
                  pl.BlockSpec((1, D_MODEL), const),
                  pl.BlockSpec((D_MODEL, tn), lambda i, j: (0, jnp.where(
                      j < _NB_BAND, j, jnp.where(j == _NB_PROJ - 1, _CKV_BLOCK, _NB_BAND - 1)))),
                  pl.BlockSpec((D_MODEL, tn), lambda i, j: (0, jnp.clip(j - _NB_BAND, 0, _NB_MLA_Q - 1))),
                  pl.BlockSpec((D_MODEL, LANES), const),
                  pl.BlockSpec((1, tn), lambda i, j: (0, jnp.minimum(j, _NB_BAND - 1))),
                  pl.BlockSpec((1, LANES), const),
                  pl.BlockSpec((1, LANES), const),
                  pl.BlockSpec((1, MLA_KV_RANK), const),
                  pl.BlockSpec((1, LANES), const),
                  pl.BlockSpec((tm, LANES), lambda i, j: (i % n_cs, 0))],
        out_specs=[pl.BlockSpec((tm, D_MODEL), rows),
                   pl.BlockSpec((tm, tn), lambda i, j: (i, jnp.minimum(j, _NB_BAND - 1))),
                   pl.BlockSpec((tn, tm), lambda i, j: (jnp.clip(j - _NB_BAND_NORM, 0, _NB_BAND - _NB_BAND_NORM - 1), i)),
                   pl.BlockSpec((tm, tn), lambda i, j: (i, jnp.clip(j - _NB_BAND, 0, _NB_MLA_Q - 1))),
                   pl.BlockSpec((tm, MLA_KV_RANK), rows),
                   pl.BlockSpec((tm, MLA_KV_RANK), rows),
                   pl.BlockSpec((tm, LANES), rows),
                   pl.BlockSpec((tm, MLA_ROPE), rows)],
        out_shape=[jax.ShapeDtypeStruct((m, D_MODEL), BF16),
                   jax.ShapeDtypeStruct((m, 3 * BAND_WIDTH), BF16),
                   jax.ShapeDtypeStruct((BAND_WIDTH, m), BF16),
                   jax.ShapeDtypeStruct((m, MLA_HEADS * MLA_QK_PAD), BF16),
                   jax.ShapeDtypeStruct((m, MLA_KV_RANK), F32),
                   jax.ShapeDtypeStruct((m, MLA_KV_RANK), BF16),
                   jax.ShapeDtypeStruct((m, LANES), F32),
                   jax.ShapeDtypeStruct((m, MLA_ROPE), F32)],
        scratch_shapes=[pltpu.VMEM((tm, D_MODEL), BF16)],
        compiler_params=_params(2, 62 * 1024 * 1024), name="proj")(
            x, p["gx"], p["w_in"], p["w_q"], p["w_kr"], p["g_a"], p["g_qn"], p["g_qr"], p["g_kv"], p["g_kr"], cs)


def _kv_f32_kernel(x_ref, gx_ref, w_ref, g_ref, k_ref, v_ref, xn_ref):
    j = pl.program_id(1)
    tm = x_ref.shape[0]

    def heads(o_ref, acc, norm):
        for h in range(BAND_HEADS):
            y = acc[:, h * BAND_HEAD_DIM:(h + 1) * BAND_HEAD_DIM]
            o_ref[pl.ds(h, tm, stride=BAND_HEADS), :] = _rms(y, g_ref[...]) if norm else y

    @pl.when(j == 0)
    def _():
        xn = _rms(x_ref[...], gx_ref[...]).astype(BF16)
        xn_ref[...] = xn
        heads(k_ref, _dot(xn, w_ref[...]), True)

    @pl.when(j == 1)
    def _():
        heads(v_ref, _dot(xn_ref[...], w_ref[...]), False)


def _kv_f32(x, gx, w_in, g, tm, n_rows, row_block):
    out = pl.BlockSpec((tm * BAND_HEADS, BAND_HEAD_DIM), lambda i, j: (i, 0))
    shape = jax.ShapeDtypeStruct((n_rows * BAND_HEADS, BAND_HEAD_DIM), F32)
    return pl.pallas_call(
        _kv_f32_kernel,
        grid=(n_rows // tm, 2),
        in_specs=[pl.BlockSpec((tm, D_MODEL), lambda i, j: (row_block(i), 0)),
                  pl.BlockSpec((1, D_MODEL), lambda i, j: (0, 0)),
                  pl.BlockSpec((D_MODEL, BAND_WIDTH), lambda i, j: (0, j + 1)),
                  pl.BlockSpec((1, LANES), lambda i, j: (0, 0))],
        out_specs=[out, out],
        out_shape=[shape, shape],
        scratch_shapes=[pltpu.VMEM((tm, D_MODEL), BF16)],
        compiler_params=_params(2), name="kv_f32")(x, gx, w_in, g)


def _expand_kernel(c_ref, w_ref, gkn_ref, krr_ref, k_ref, v_ref, *, v_transposed):
    c = c_ref[...].astype(BF16)
    krr = krr_ref[...].astype(BF16)
    if krr.shape[1] == MLA_ROPE:
        krr = jnp.concatenate([krr, krr], axis=1)
    for h in range(MLA_HEADS):
        a = _dot(c, w_ref[:, h * 2 * LANES:(h + 1) * 2 * LANES])
        k_ref[:, h * MLA_QK_PAD:h * MLA_QK_PAD + LANES] = _rms(a[:, :LANES], gkn_ref[...]).astype(BF16)
        k_ref[:, h * MLA_QK_PAD + LANES:(h + 1) * MLA_QK_PAD] = krr
        if v_transposed:
            slab = v_ref.shape[2]
            for n in range(v_ref.shape[0]):
                v_ref[n, h * MLA_V:(h + 1) * MLA_V, :] = a[n * slab:(n + 1) * slab, LANES:].T.astype(BF16)
        else:
            v_ref[:, h * MLA_V:(h + 1) * MLA_V] = a[:, LANES:].astype(BF16)


def _expand(c, w, gkn, krr, tm, v_transposed, slab=None):
    m = c.shape[0]
    if v_transposed:
        v_spec = pl.BlockSpec((tm // slab, MLA_WIDTH, slab), lambda i: (i, 0, 0))
        v_shape = jax.ShapeDtypeStruct((m // slab, MLA_WIDTH, slab), BF16)
    else:
        v_spec = pl.BlockSpec((tm, MLA_WIDTH), lambda i: (i, 0))
        v_shape = jax.ShapeDtypeStruct((m, MLA_WIDTH), BF16)
    return pl.pallas_call(
        functools.partial(_expand_kernel, v_transposed=v_transposed),
        grid=(m // tm,),
        in_specs=[pl.BlockSpec((tm, MLA_KV_RANK), lambda i: (i, 0)),
                  pl.BlockSpec(w.shape, lambda i: (0, 0)),
                  pl.BlockSpec((1, LANES), lambda i: (0, 0)),
                  pl.BlockSpec((tm, krr.shape[1]), lambda i: (i, 0))],
        out_specs=[pl.BlockSpec((tm, MLA_HEADS * MLA_QK_PAD), lambda i: (i, 0)), v_spec],
        out_shape=[jax.ShapeDtypeStruct((m, MLA_HEADS * MLA_QK_PAD), BF16), v_shape],
        compiler_params=_params(1), name="expand")(c, w, gkn, krr)


def _band_bias_kernel(b_ref, o_ref, ot_ref):
    t = pltpu.roll(jnp.broadcast_to(b_ref[0], (BAND_TQ, 2 * BAND_PAST)), 0, 1,
                   stride=1, stride_axis=0)[:, :BAND_WIN] * LOG2E
    qc = lax.broadcasted_iota(jnp.int32, (BAND_TQ, BAND_WIN), 0) // CHUNK
    kc = lax.broadcasted_iota(jnp.int32, (BAND_TQ, BAND_WIN), 1) // CHUNK
    t = jnp.where(kc >= qc, jnp.where(kc <= qc + BAND_PAST // CHUNK, t, NEG), NEG)
    o_ref[0] = t
    tt = t.T
    key = lax.broadcasted_iota(jnp.int32, (BAND_WIN, BAND_TQ), 0)
    for v in range(BAND_PIECES):
        n_missing = BAND_PIECES - 1 - v
        ot_ref[v, 0] = jnp.where(key >= n_missing * BAND_TQ, tt, NEG)


def _band_bias(rel_bias):
    far = jnp.broadcast_to(rel_bias[:, REL_SIZE - 1:], (BAND_HEADS, BAND_PAST - REL_MAX))
    near = jnp.broadcast_to(rel_bias[:, :1], (BAND_HEADS, BAND_WIN - BAND_PAST - CHUNK))
    wrap = jnp.broadcast_to(rel_bias[:, REL_SIZE - 1:], (BAND_HEADS, 2 * BAND_PAST - BAND_WIN))
    row0 = jnp.concatenate([far, rel_bias[:, ::-1], near, wrap], axis=1)
    assert row0.shape == (BAND_HEADS, 2 * BAND_PAST)
    return pl.pallas_call(
        _band_bias_kernel,
        grid=(BAND_HEADS,),
        in_specs=[pl.BlockSpec((1, 1, 2 * BAND_PAST), lambda h: (h, 0, 0))],
        out_specs=[pl.BlockSpec((1, BAND_TQ, BAND_WIN), lambda h: (h, 0, 0)),
                   pl.BlockSpec((BAND_PIECES, 1, BAND_WIN, BAND_TQ), lambda h: (0, h, 0, 0))],
        out_shape=[jax.ShapeDtypeStruct((BAND_HEADS, BAND_TQ, BAND_WIN), F32),
                   jax.ShapeDtypeStruct((BAND_PIECES, BAND_HEADS, BAND_WIN, BAND_TQ), F32)],
        compiler_params=_params(1), name="band_bias")(row0[:, None, :])


def _softmax_pv(s, v):
    m = functools.reduce(jnp.maximum, [jnp.max(sp, axis=-1, keepdims=True) for sp in s])
    e = [jnp.exp2(sp - m) for sp in s]
    l = functools.reduce(jnp.add, [jnp.sum(ep, axis=-1, keepdims=True) for ep in e])
    o = functools.reduce(jnp.add, [_dot(ep.astype(BF16), vp) for ep, vp in zip(e, v)])
    return (o / l).astype(BF16)


def _pipelined_heads(n_heads, scores, finish, bufs):
    ahead = len(bufs) - 1

    def issue(h):
        for ref, piece in zip(bufs[h % len(bufs)], scores(h)):
            ref[...] = piece

    for h in range(min(ahead, n_heads)):
        issue(h)
    for h in range(n_heads):
        if h + ahead < n_heads:
            issue(h + ahead)
        finish(h, [ref[...] for ref in bufs[h % len(bufs)]])


def _band_prompt_kernel(q_ref, *refs):
    n_piece = BAND_QB + BAND_PIECES - 1
    k_refs, vt_refs = refs[:n_piece], refs[n_piece:2 * n_piece]
    bias_refs = refs[2 * n_piece:2 * n_piece + BAND_N_BIAS]
    o_ref = refs[2 * n_piece + BAND_N_BIAS]
    bufs = refs[2 * n_piece + BAND_N_BIAS + 1:]

    def head(h):
        return slice(h * BAND_HEAD_DIM, (h + 1) * BAND_HEAD_DIM)

    def rows(n):
        return slice(n * BAND_TQ, (n + 1) * BAND_TQ)

    def scores(n, h):
        k = jnp.concatenate([r[0, :, head(h)] for r in k_refs[n:n + BAND_PIECES]], axis=0)
        return _dot_nt(k, q_ref[0, rows(n), head(h)]) + bias_refs[min(n, BAND_N_BIAS - 1)][0, h]

    units = [(n, h) for n in range(BAND_QB) for h in range(BAND_HEADS)]
    ahead = len(bufs) - 1
    for u in range(ahead):
        bufs[u][...] = scores(*units[u])
    for u, (n, h) in enumerate(units):
        if u + ahead < len(units):
            bufs[(u + ahead) % len(bufs)][...] = scores(*units[u + ahead])
        s = bufs[u % len(bufs)][...]
        vt = jnp.concatenate([r[head(h), :] for r in vt_refs[n:n + BAND_PIECES]], axis=1)
        vt = jnp.concatenate([vt, jnp.ones((SUM_ROWS, BAND_WIN), BF16)], axis=0)
        e = jnp.exp2(s - jnp.max(s, axis=0, keepdims=True))
        o = _dot(vt, e.astype(BF16))
        o = o[:BAND_HEAD_DIM] / o[BAND_HEAD_DIM:BAND_HEAD_DIM + 1]
        o_ref[0, rows(n), head(h)] = o.T.astype(BF16)


def _band_prompt(qkv, vt, bias_t):
    b, s, _ = qkv.shape
    nq = s // BAND_TQ
    assert nq % BAND_QB == 0
    n_piece = BAND_QB + BAND_PIECES - 1
    piece = (1, BAND_TQ, BAND_WIDTH)
    group = (1, BAND_QB * BAND_TQ, BAND_WIDTH)

    def first_block(i, p):
        return jnp.maximum(i * BAND_QB - (BAND_PIECES - 1) + p, 0)

    def k_spec(p):
        return pl.BlockSpec(piece, lambda bi, i: (bi, first_block(i, p), 1))

    def vt_spec(p):
        return pl.BlockSpec((BAND_WIDTH, BAND_TQ), lambda bi, i: (0, bi * nq + first_block(i, p)))

    def bias_spec(n):
        return pl.BlockSpec((1,) + bias_t.shape[1:],
                            lambda bi, i: (jnp.minimum(i * BAND_QB + n, BAND_PIECES - 1), 0, 0, 0))

    return pl.pallas_call(
        _band_prompt_kernel,
        grid=(b, nq // BAND_QB),
        in_specs=[pl.BlockSpec(group, lambda bi, i: (bi, i, 0))]
                 + [k_spec(p) for p in range(n_piece)]
                 + [vt_spec(p) for p in range(n_piece)]
                 + [bias_spec(n) for n in range(BAND_N_BIAS)],
        out_specs=pl.BlockSpec(group, lambda bi, i: (bi, i, 0)),
        out_shape=jax.ShapeDtypeStruct((b, s, BAND_WIDTH), BF16),
        scratch_shapes=[pltpu.VMEM((BAND_WIN, BAND_TQ), F32)] * BAND_SCORE_BUFFERS,
        compiler_params=_params(2, 62 * 1024 * 1024), name="band_prompt")(
            qkv, *([qkv] * n_piece), *([vt] * n_piece), *([bias_t] * BAND_N_BIAS))


def _band_sample_kernel(q_ref, kn_ref, vn_ref, kc_ref, vc_ref, bias_ref, o_ref, *bufs, n_past):
    t = q_ref.shape[1]

    def head(h):
        return slice(h * BAND_HEAD_DIM, (h + 1) * BAND_HEAD_DIM)

    def cached(ref, h):
        return ref[0, pl.ds(h, n_past, stride=BAND_HEADS), :].astype(BF16)

    def scores(h):
        q = q_ref[0, :, head(h)]
        return (_dot_nt(q, cached(kc_ref, h)) + bias_ref[h, :t, :n_past],
                _dot_nt(q, kn_ref[0, :, head(h)]) + bias_ref[h, :t, n_past:n_past + t])

    def finish(h, s):
        o_ref[0, :, head(h)] = _softmax_pv(s, [cached(vc_ref, h), vn_ref[0, :, head(h)]])

    _pipelined_heads(BAND_HEADS, scores, finish, list(zip(bufs[0::2], bufs[1::2])))


def _band_sample(qkv, k_cache, v_cache, bias):
    b, t, _ = qkv.shape
    n_past = k_cache.shape[1] // BAND_HEADS
    new = (1, t, BAND_WIDTH)
    old = (1, n_past * BAND_HEADS, BAND_HEAD_DIM)
    return pl.pallas_call(
        functools.partial(_band_sample_kernel, n_past=n_past),
        grid=(b,),
        in_specs=[pl.BlockSpec(new, lambda bi: (bi, 0, 0)),
                  pl.BlockSpec(new, lambda bi: (bi, 0, 1)),
                  pl.BlockSpec(new, lambda bi: (bi, 0, 2)),
                  pl.BlockSpec(old, lambda bi: (bi, 0, 0)),
                  pl.BlockSpec(old, lambda bi: (bi, 0, 0)),
                  pl.BlockSpec(bias.shape, lambda bi: (0, 0, 0))],
        out_specs=pl.BlockSpec(new, lambda bi: (bi, 0, 0)),
        out_shape=jax.ShapeDtypeStruct((b, t, BAND_WIDTH), BF16),
        scratch_shapes=[pltpu.VMEM((t, n_past), F32), pltpu.VMEM((t, t), F32)] * SAMPLE_SCORE_BUFFERS,
        compiler_params=_params(1), name="band_sample")(qkv, qkv, qkv, k_cache, v_cache, bias)


def _mla_prompt_kernel(q_ref, c_ref, w_ref, gkn_ref, krr_ref, o_ref, mask_ref, k_ref, vt_ref, *bufs, tq, nq):
    for n in range(nq // MLA_EXPAND_SLABS):
        r = slice(n * MLA_EXPAND_SLABS * tq, (n + 1) * MLA_EXPAND_SLABS * tq)
        a = _dot(c_ref[0, r, :], w_ref[...])
        k_ref[0, r, :LANES] = _rms(a[:, :LANES], gkn_ref[...]).astype(BF16)
        k_ref[0, r, LANES:] = krr_ref[0, r, :].astype(BF16)
        for m in range(MLA_EXPAND_SLABS):
            vt_ref[0, n * MLA_EXPAND_SLABS + m] = a[m * tq:(m + 1) * tq, LANES:].T.astype(BF16)

    def scores(i):
        return _dot_nt(k_ref[0, :(i + 1) * tq, :], q_ref[0, i * tq:(i + 1) * tq, :])

    kc = lax.broadcasted_iota(jnp.int32, (tq, tq), 0) // CHUNK
    qc = lax.broadcasted_iota(jnp.int32, (tq, tq), 1) // CHUNK
    mask_ref[...] = jnp.where(kc <= qc, 0.0, NEG)

    ahead = len(bufs) - 1
    for i in range(min(ahead, nq)):
        bufs[i][:(i + 1) * tq] = scores(i)
    for i in range(nq):
        if i + ahead < nq:
            bufs[(i + ahead) % len(bufs)][:(i + ahead + 1) * tq] = scores(i + ahead)
        buf = bufs[i % len(bufs)]
        buf[i * tq:(i + 1) * tq] = buf[i * tq:(i + 1) * tq] + mask_ref[...]
        m = jnp.max(buf[:(i + 1) * tq], axis=0, keepdims=True)
        acc = None
        for g in range(0, i + 1, MLA_PV_GROUP):
            n = min(MLA_PV_GROUP, i + 1 - g)
            vt = jnp.concatenate([vt_ref[0, j] for j in range(g, g + n)], axis=1)
            vt = jnp.concatenate([vt, jnp.ones((SUM_ROWS, n * tq), BF16)], axis=0)
            part = _dot(vt, jnp.exp2(buf[g * tq:(g + n) * tq] - m).astype(BF16))
            acc = part if acc is None else acc + part
        o_ref[0, i * tq:(i + 1) * tq, :] = (acc[:MLA_V] / acc[MLA_V:MLA_V + 1]).T.astype(BF16)


def _mla_prompt(q, c, w, gkn, krr, tq):
    b, s, _ = q.shape
    nq = s // tq
    assert nq % MLA_EXPAND_SLABS == 0
    return pl.pallas_call(
        functools.partial(_mla_prompt_kernel, tq=tq, nq=nq),
        grid=(b, MLA_HEADS),
        in_specs=[pl.BlockSpec((1, s, MLA_QK_PAD), lambda bi, h: (bi, 0, h)),
                  pl.BlockSpec((1, s, MLA_KV_RANK), lambda bi, h: (bi, 0, 0)),
                  pl.BlockSpec((MLA_KV_RANK, 2 * LANES), lambda bi, h: (0, h)),
                  pl.BlockSpec((1, LANES), lambda bi, h: (0, 0)),
                  pl.BlockSpec((1, s, LANES), lambda bi, h: (bi, 0, 0))],
        out_specs=pl.BlockSpec((1, s, MLA_V), lambda bi, h: (bi, 0, h)),
        out_shape=jax.ShapeDtypeStruct((b, s, MLA_WIDTH), BF16),
        scratch_shapes=[pltpu.VMEM((tq, tq), F32), pltpu.VMEM((1, s, MLA_QK_PAD), BF16),
                        pltpu.VMEM((1, nq, MLA_V, tq), BF16)]
                       + [pltpu.VMEM((s, tq), F32)] * MLA_SCORE_BUFFERS,
        compiler_params=_params(2), name="mla_prompt")(q, c, w, gkn, krr)


def _mla_sample_kernel(q_ref, c_ref, w_ref, gkn_ref, kr_ref, kn_ref, vn_ref, o_ref, ko_ref, vo_ref, *bufs):
    c = c_ref[0].astype(BF16)
    kr = kr_ref[0].astype(BF16)
    krr = jnp.concatenate([kr, kr], axis=1)
    for h in range(MLA_HEADS):
        a = _dot(c, w_ref[:, h * 2 * LANES:(h + 1) * 2 * LANES])
        ko_ref[0, :, h * MLA_QK_PAD:h * MLA_QK_PAD + LANES] = _rms(a[:, :LANES], gkn_ref[...]).astype(BF16)
        ko_ref[0, :, h * MLA_QK_PAD + LANES:(h + 1) * MLA_QK_PAD] = krr
        vo_ref[0, :, h * MLA_V:(h + 1) * MLA_V] = a[:, LANES:].astype(BF16)

    def scores(h):
        qk = slice(h * MLA_QK_PAD, (h + 1) * MLA_QK_PAD)
        q = q_ref[0, :, qk]
        return _dot_nt(q, ko_ref[0, :, qk]), _dot_nt(q, kn_ref[0, :, qk])

    def finish(h, s):
        hv = slice(h * MLA_V, (h + 1) * MLA_V)
        o_ref[0, :, hv] = _softmax_pv(s, [vo_ref[0, :, hv], vn_ref[0, :, hv]])

    _pipelined_heads(MLA_HEADS, scores, finish, list(zip(bufs[0::2], bufs[1::2])))


def _mla_sample(q, c_old, w, gkn, kr_old, k_new, v_new):
    b, t, _ = q.shape
    n_old = c_old.shape[1]

    def whole(a):
        return pl.BlockSpec((1,) + a.shape[1:], lambda bi: (bi, 0, 0))

    def const(a):
        return pl.BlockSpec(a.shape, lambda bi: (0, 0))

    return pl.pallas_call(
        _mla_sample_kernel,
        grid=(b,),
        in_specs=[whole(q), whole(c_old), const(w), const(gkn), whole(kr_old), whole(k_new), whole(v_new)],
        out_specs=pl.BlockSpec((1, t, MLA_WIDTH), lambda bi: (bi, 0, 0)),
        out_shape=jax.ShapeDtypeStruct((b, t, MLA_WIDTH), BF16),
        scratch_shapes=[pltpu.VMEM((1, n_old, MLA_HEADS * MLA_QK_PAD), BF16),
                        pltpu.VMEM((1, n_old, MLA_WIDTH), BF16)]
                       + [pltpu.VMEM((t, n_old), F32), pltpu.VMEM((t, k_new.shape[1]), F32)] * SAMPLE_SCORE_BUFFERS,
        compiler_params=_params(1), name="mla_sample")(q, c_old, w, gkn, kr_old, k_new, v_new)


def _mix_kernel(xn_ref, oa_ref, ob_ref, wga_ref, wgb_ref, wpa_ref, wpb_ref, o_ref):
    xn = xn_ref[...]
    a = jax.nn.sigmoid(_dot(xn, wga_ref[...])) * _dot(oa_ref[...], wpa_ref[...])
    b = jax.nn.sigmoid(_dot(xn, wgb_ref[...])) * _dot(ob_ref[...], wpb_ref[...])
    o_ref[...] = (a + b).astype(BF16)


def _mix(xn, oa, ob, wga, wgb, wpa, wpb, tm, tn=1024):
    m = xn.shape[0]
    return pl.pallas_call(
        _mix_kernel,
        grid=(m // tm, D_MODEL // tn),
        in_specs=[pl.BlockSpec((tm, D_MODEL), lambda i, j: (i, 0)),
                  pl.BlockSpec((tm, BAND_WIDTH), lambda i, j: (i, 0)),
                  pl.BlockSpec((tm, MLA_WIDTH), lambda i, j: (i, 0)),
                  pl.BlockSpec((D_MODEL, tn), lambda i, j: (0, j)),
                  pl.BlockSpec((D_MODEL, tn), lambda i, j: (0, j)),
                  pl.BlockSpec((BAND_WIDTH, tn), lambda i, j: (0, j)),
                  pl.BlockSpec((MLA_WIDTH, tn), lambda i, j: (0, j))],
        out_specs=pl.BlockSpec((tm, tn), lambda i, j: (i, j)),
        out_shape=jax.ShapeDtypeStruct((m, D_MODEL), BF16),
        compiler_params=_params(2, 62 * 1024 * 1024), name="mix")(xn, oa, ob, wga, wgb, wpa, wpb)


def _outproj_kernel(x_ref, mix_ref, w_ref, h_ref):
    h_ref[...] = x_ref[...] + _dot(mix_ref[...], w_ref[...])


def _outproj(x, mix, w, tm):
    m = x.shape[0]
    rows = pl.BlockSpec((tm, D_MODEL), lambda i: (i, 0))
    return pl.pallas_call(
        _outproj_kernel,
        grid=(m // tm,),
        in_specs=[rows, rows, pl.BlockSpec((D_MODEL, D_MODEL), lambda i: (0, 0))],
        out_specs=rows,
        out_shape=jax.ShapeDtypeStruct((m, D_MODEL), F32),
        compiler_params=_params(1), name="outproj")(x, mix, w)


def _ffn_kernel(h_ref, g_ref, wu_ref, wd_ref, o_ref, hn_ref):
    def update(hn):
        u = jnp.maximum(_dot(hn, wu_ref[...]), 0.0)
        return _dot((u * u).astype(BF16), wd_ref[...])

    @pl.when(pl.program_id(1) == 0)
    def _():
        h = h_ref[...]
        hn = _rms(h, g_ref[...]).astype(BF16)
        hn_ref[...] = hn
        o_ref[...] = h + update(hn)

    @pl.when(pl.program_id(1) > 0)
    def _():
        o_ref[...] += update(hn_ref[...])


def _ffn(h, g, wu, wd, tm, tf=1024):
    m = h.shape[0]
    return pl.pallas_call(
        _ffn_kernel,
        grid=(m // tm, D_FF // tf),
        in_specs=[pl.BlockSpec((tm, D_MODEL), lambda i, f: (i, 0)),
                  pl.BlockSpec((1, D_MODEL), lambda i, f: (0, 0)),
                  pl.BlockSpec((D_MODEL, tf), lambda i, f: (0, f)),
                  pl.BlockSpec((tf, D_MODEL), lambda i, f: (f, 0))],
        out_specs=pl.BlockSpec((tm, D_MODEL), lambda i, f: (i, 0)),
        out_shape=jax.ShapeDtypeStruct((m, D_MODEL), F32),
        scratch_shapes=[pltpu.VMEM((tm, D_MODEL), BF16)],
        compiler_params=_params(2, 62 * 1024 * 1024), name="ffn")(h, g, wu, wd)


def _rope_table(pos, rows):
    half = MLA_ROPE // 2
    freqs = ROPE_THETA ** (-(jnp.arange(half, dtype=F32) / half))
    ang = pos.astype(F32)[:, None] * freqs[None, :]
    cos = jnp.cos(ang)
    sin = jnp.sin(ang)
    table = jnp.concatenate([cos, cos, -sin, sin], axis=1)
    return jnp.tile(table, (rows // table.shape[0], 1))


def _with_partner(w):
    half = MLA_ROPE // 2
    return jnp.concatenate([w, w[..., half:], w[..., :half]], axis=-1)


def _layer_weights(l, norm_mix_g, w_in, g_aq, g_ak, g_kv, g_kr, g_qn, g_qr, g_kn,
                   w_kv_b, w_pa, w_pb, w_out, norm_ffn_g, w_up, w_down):
    w = w_in[l].astype(BF16)
    c = 0
    parts = []
    for width in (3 * BAND_WIDTH, MLA_HEADS * MLA_QK, MLA_KV_RANK, MLA_ROPE, D_MODEL, D_MODEL):
        parts.append(w[:, c:c + width])
        c += width
    w_a, w_bq, w_ckv, w_kr, w_ga, w_gb = parts
    w_q = jnp.concatenate(
        [piece for h in range(MLA_HEADS)
         for piece in (w_bq[:, h * MLA_QK:h * MLA_QK + MLA_NOPE],
                       _with_partner(w_bq[:, h * MLA_QK + MLA_NOPE:(h + 1) * MLA_QK]))], axis=1)
    row = lambda g: g[l][None, :].astype(F32)
    return dict(
        gx=row(norm_mix_g),
        w_in=w,
        w_q=w_q,
        w_kr=_with_partner(w_kr),
        g_a=jnp.concatenate([jnp.tile(g_aq[l] * BAND_QSCALE, BAND_HEADS), jnp.tile(g_ak[l], BAND_HEADS),
                             jnp.ones((BAND_WIDTH,), F32)])[None, :],
        g_ak=row(g_ak),
        g_qn=row(g_qn),
        g_qr=_with_partner(g_qr[l])[None, :],
        g_kv=row(g_kv),
        g_kr=_with_partner(g_kr[l])[None, :],
        w_kv_b=w_kv_b[l].astype(BF16),
        g_kn=row(g_kn),
        w_ga=w_ga, w_gb=w_gb,
        w_pa=w_pa[l].astype(BF16), w_pb=w_pb[l].astype(BF16),
        w_out=w_out[l].astype(BF16),
        g_ffn=row(norm_ffn_g),
        w_up=w_up[l].astype(BF16), w_down=w_down[l].astype(BF16),
    )


def _merge_ffn(x2, xn, oa, ob, p, tm):
    mix = _mix(xn, oa, ob, p["w_ga"], p["w_gb"], p["w_pa"], p["w_pb"], tm)
    h = _outproj(x2, mix, p["w_out"], min(tm, 512))
    return _ffn(h, p["g_ffn"], p["w_up"], p["w_down"], tm)


def _cache_shape(kv, lead):
    return tuple(a.reshape(lead + (BAND_HEADS, BAND_HEAD_DIM)) for a in kv)


def kernel(x_prompt, x_sample, cache_a_k, cache_a_v, cache_mla_ckv, cache_mla_krope, norm_mix_g, w_in, g_aq, g_ak, rel_bias, g_kv, g_kr, g_qn, g_qr, g_kn, w_kv_b, w_pa, w_pb, w_out, norm_ffn_g, w_up, w_down):
    b, s, _ = x_prompt.shape
    bs, t, _ = x_sample.shape
    past = cache_mla_ckv.shape[2]
    n_band = cache_a_k.shape[2]
    depth = w_in.shape[0]
    keep = min(BAND_PAST, s)
    tm_p = 1024
    tq_mla = 256
    tm_s = bs * t
    assert s % tm_p == 0 and s % BAND_TQ == 0 and s % tq_mla == 0 and t == CHUNK and past % CHUNK == 0
    tm_cache = min(bs * past, 2048)
    assert (bs * past) % tm_cache == 0
    tm_keep = min(keep, 512)
    assert keep % tm_keep == 0 and s % tm_keep == 0 and (s - keep) % tm_keep == 0
    cs_p = _rope_table(jnp.arange(s), max(s, tm_p))
    cs_s = _rope_table(past + jnp.arange(t), tm_s)

    yp = x_prompt.reshape(b * s, D_MODEL)
    ys = x_sample.reshape(bs * t, D_MODEL)
    outs = [[] for _ in range(8)]
    for l in range(depth):
        p = _layer_weights(l, norm_mix_g, w_in, g_aq, g_ak, g_kv, g_kr, g_qn, g_qr, g_kn,
                           w_kv_b, w_pa, w_pb, w_out, norm_ffn_g, w_up, w_down)
        bias, bias_t = _band_bias(rel_bias[l])

        xn, qkv, vt, q_mla, ckv, ckv_bf, krr, kr = _proj(yp, p, cs_p, tm_p)
        kb = keep // tm_keep
        ak, av = _cache_shape(_kv_f32(
            yp, p["gx"], p["w_in"], p["g_ak"], tm_keep, b * keep,
            lambda i: (i // kb) * (s // tm_keep) + (s - keep) // tm_keep + i % kb), (b, keep))
        oa = _band_prompt(qkv.reshape(b, s, 3 * BAND_WIDTH), vt, bias_t)
        ob = _mla_prompt(q_mla.reshape(b, s, -1), ckv_bf.reshape(b, s, MLA_KV_RANK), p["w_kv_b"], p["g_kn"],
                         krr.reshape(b, s, LANES), tq_mla)
        for o, val in zip(outs[:4], (ak, av, ckv.reshape(b, s, MLA_KV_RANK), kr.reshape(b, s, MLA_ROPE))):
            o.append(val)
        yp = _merge_ffn(yp, xn, oa.reshape(b * s, BAND_WIDTH), ob.reshape(b * s, MLA_WIDTH), p, tm_p)

        xn, qkv, _, q_mla, ckv, ckv_bf, krr, kr = _proj(ys, p, cs_s, tm_s)
        ak, av = _cache_shape(_kv_f32(ys, p["gx"], p["w_in"], p["g_ak"], tm_s, bs * t, lambda i: i), (bs, t))
        oa = _band_sample(qkv.reshape(bs, t, 3 * BAND_WIDTH),
                          cache_a_k[l].reshape(bs, n_band * BAND_HEADS, BAND_HEAD_DIM),
                          cache_a_v[l].reshape(bs, n_band * BAND_HEADS, BAND_HEAD_DIM), bias)
        k_new, v_new = _expand(ckv_bf, p["w_kv_b"], p["g_kn"], krr, tm_s, False)
        ob = _mla_sample(q_mla.reshape(bs, t, -1), cache_mla_ckv[l], p["w_kv_b"], p["g_kn"], cache_mla_krope[l],
                         k_new.reshape(bs, t, -1), v_new.reshape(bs, t, -1))
        for o, val in zip(outs[4:], (ak, av, ckv.reshape(bs, t, MLA_KV_RANK), kr.reshape(bs, t, MLA_ROPE))):
            o.append(val)
        ys = _merge_ffn(ys, xn, oa.reshape(bs * t, BAND_WIDTH), ob.reshape(bs * t, MLA_WIDTH), p, tm_s)

    return (yp.reshape(b, s, D_MODEL), ys.reshape(bs, t, D_MODEL)) + tuple(jnp.stack(o) for o in outs)
```
